```python
import jax, jax.numpy as jnp
from jax import lax
import numpy as np

D_MODEL = 1024
BATCH = 16
SEQ = 2048
DEPTH = 1
DEC_BATCH = 4
DEC_SEQ = 4096
PAST_LEN = 128

GRID_W = 64
N_MEM = 256
ATT_HEADS = 8
ATT_KV_HEADS = 2
HEAD_DIM = 64
ATT_WIDTH = ATT_HEADS * HEAD_DIM
KV_WIDTH = ATT_KV_HEADS * HEAD_DIM
Q_BLOCK = 128
ROPE_THETA = 10000.0
ROPE_AXIS_DIM = HEAD_DIM // 2
CONV_WIDTH = D_MODEL - ATT_WIDTH
CONV_KSIZE = 31
MIX_WIDTH = ATT_WIDTH + CONV_WIDTH
IN_WIDTH = ATT_WIDTH + 2 * KV_WIDTH + 2 * CONV_WIDTH
MEM_HEADS = 4
MEM_HEAD_DIM = D_MODEL // MEM_HEADS
N_GROUPS = 4
EXPERTS_PER_GROUP = 8
N_EXPERTS = N_GROUPS * EXPERTS_PER_GROUP
TOP_K = 2
D_EXPERT = 256
LN_EPS = 1e-5
RMS_EPS = 1e-6
DEEPNORM_ALPHA = (2.0 * DEPTH) ** 0.25
DEEPNORM_BETA = (8.0 * DEPTH) ** -0.25

kernel_name = 'hybrid_bidir_encoder_gqa_conformer_hmoe'


def _layer_norm(x, g, b):
    xf = x.astype(jnp.float32)
    mu = jnp.mean(xf, -1, keepdims=True)
    var = jnp.mean(jnp.square(xf - mu), -1, keepdims=True)
    return ((xf - mu) * lax.rsqrt(var + LN_EPS) * g + b).astype(x.dtype)


def _rms_norm(x, g):
    xf = x.astype(jnp.float32)
    return (xf * lax.rsqrt(jnp.mean(xf * xf, -1, keepdims=True) + RMS_EPS) * g).astype(x.dtype)


def _axial_rope_tables(n_tok):
    rows = n_tok // GRID_W
    row = jnp.repeat(jnp.arange(rows, dtype=jnp.float32), GRID_W)
    col = jnp.tile(jnp.arange(GRID_W, dtype=jnp.float32), rows)
    inv_freq = ROPE_THETA ** (-jnp.arange(0, ROPE_AXIS_DIM, 2, dtype=jnp.float32) / ROPE_AXIS_DIM)
    ang = jnp.concatenate([row[:, None] * inv_freq, col[:, None] * inv_freq], -1)
    return jnp.cos(ang), jnp.sin(ang)


def _apply_rope(x, cos, sin):
    xf = x.astype(jnp.float32).reshape(x.shape[:-1] + (HEAD_DIM // 2, 2))
    x0, x1 = xf[..., 0], xf[..., 1]
    c, s = cos[None, :, None, :], sin[None, :, None, :]
    out = jnp.stack([x0 * c - x1 * s, x0 * s + x1 * c], -1)
    return out.reshape(x.shape).astype(x.dtype)


def _blocked_gqa(q, k, v):
    b, s = q.shape[:2]
    rep = ATT_HEADS // ATT_KV_HEADS
    nblk = s // Q_BLOCK
    qb = q.reshape(b, nblk, Q_BLOCK, ATT_KV_HEADS, rep, HEAD_DIM).transpose(1, 0, 2, 3, 4, 5)
    scale = HEAD_DIM ** -0.5

    def one_block(q_blk):
        sc = jnp.einsum('bqhgd,bkhd->bhgqk', q_blk, k, preferred_element_type=jnp.float32) * scale
        p = jax.nn.softmax(sc, axis=-1).astype(v.dtype)
        return jnp.einsum('bhgqk,bkhd->bqhgd', p, v)

    out = lax.map(one_block, qb)
    return out.transpose(1, 0, 2, 3, 4, 5).reshape(b, s, ATT_WIDTH)


def _depthwise_conv(u, w, bias):
    pad = CONV_KSIZE // 2
    out = lax.conv_general_dilated(u, w[:, None, :], window_strides=(1,), padding=[(pad, pad)],
                                   dimension_numbers=('NWC', 'WIO', 'NWC'), feature_group_count=CONV_WIDTH)
    return out + bias


def _parallel_mixer(h, cos, sin, w_in, q_g, k_g, conv_w, conv_b, cln_g, cln_b, w_out):
    b, s, _ = h.shape
    z = h @ w_in
    i1 = ATT_WIDTH
    i2 = i1 + KV_WIDTH
    i3 = i2 + KV_WIDTH
    i4 = i3 + CONV_WIDTH
    q, k, v, a, gate = jnp.split(z, [i1, i2, i3, i4], axis=-1)
    q = _apply_rope(_rms_norm(q.reshape(b, s, ATT_HEADS, HEAD_DIM), q_g), cos, sin)
    k = _apply_rope(_rms_norm(k.reshape(b, s, ATT_KV_HEADS, HEAD_DIM), k_g), cos, sin)
    v = v.reshape(b, s, ATT_KV_HEADS, HEAD_DIM)
    att = _blocked_gqa(q, k, v)
    u = a * jax.nn.sigmoid(gate)
    u = jax.nn.silu(_layer_norm(_depthwise_conv(u, conv_w, conv_b), cln_g, cln_b))
    return jnp.concatenate([att, u], -1) @ w_out


def _memory_xattn(h, mem, w_q, w_k, w_v, w_o):
    b, s, _ = h.shape
    m = mem.shape[1]
    q = (h @ w_q).reshape(b, s, MEM_HEADS, MEM_HEAD_DIM)
    k = (mem @ w_k).reshape(b, m, MEM_HEADS, MEM_HEAD_DIM)
    v = (mem @ w_v).reshape(b, m, MEM_HEADS, MEM_HEAD_DIM)
    sc = jnp.einsum('bqhd,bkhd->bhqk', q, k, preferred_element_type=jnp.float32) * (MEM_HEAD_DIM ** -0.5)
    p = jax.nn.softmax(sc, axis=-1).astype(v.dtype)
    o = jnp.einsum('bhqk,bkhd->bqhd', p, v).reshape(b, s, D_MODEL)
    return o @ w_o


def _hier_moe(h, w_rg, b_rg, w_re, b_re, w_gate, w_up, w_down):
    b, s, d = h.shape
    t = h.reshape(b * s, d)
    g_logits = (t @ w_rg).astype(jnp.float32) + b_rg
    g_prob = jax.nn.softmax(g_logits, axis=-1)
    grp_1h = jax.nn.one_hot(jnp.argmax(g_logits, -1), N_GROUPS, dtype=jnp.float32)
    p_grp = jnp.sum(g_prob * grp_1h, -1, keepdims=True)
    e_logits = jnp.einsum('td,gde->tge', t, w_re).astype(jnp.float32) + b_re
    e_logits = jnp.einsum('tg,tge->te', grp_1h, e_logits)
    top_v, top_i = lax.top_k(e_logits, TOP_K)
    top_w = jax.nn.softmax(top_v, axis=-1) * p_grp
    within = jnp.sum(jax.nn.one_hot(top_i, EXPERTS_PER_GROUP, dtype=jnp.float32) * top_w[..., None], 1)
    comb = (grp_1h[:, :, None] * within[:, None, :]).reshape(-1, N_EXPERTS).astype(t.dtype)
    y = jnp.zeros_like(t)
    for e in range(N_EXPERTS):
        he = jax.nn.silu(t @ w_gate[e]) * (t @ w_up[e])
        y = y + comb[:, e:e + 1] * (he @ w_down[e])
    return y.reshape(b, s, d)


def _encoder(x, mem, ln_in_g, ln_in_b, ln_mem_g, ln_mem_b, w_in, q_norm_g, k_norm_g, conv_w, conv_b,
             conv_ln_g, conv_ln_b, w_mix_out, ln1_g, ln1_b, w_cq, w_ck, w_cv, w_co, ln2_g, ln2_b,
             w_router_g, b_router_g, w_router_e, b_router_e, w_e_gate, w_e_up, w_e_down, ln3_g, ln3_b):
    cos, sin = _axial_rope_tables(x.shape[1])
    h = _layer_norm(x, ln_in_g, ln_in_b)
    m = _layer_norm(mem, ln_mem_g, ln_mem_b)
    for l in range(DEPTH):
        mix = _parallel_mixer(h, cos, sin, w_in[l], q_norm_g[l], k_norm_g[l], conv_w[l], conv_b[l],
                              conv_ln_g[l], conv_ln_b[l], w_mix_out[l])
        h = _layer_norm(DEEPNORM_ALPHA * h + mix, ln1_g[l], ln1_b[l])
        xa = _memory_xattn(h, m, w_cq[l], w_ck[l], w_cv[l], w_co[l])
        h = _layer_norm(DEEPNORM_ALPHA * h + xa, ln2_g[l], ln2_b[l])
        ff = _hier_moe(h, w_router_g[l], b_router_g[l], w_router_e[l], b_router_e[l],
                       w_e_gate[l], w_e_up[l], w_e_down[l])
        h = _layer_norm(DEEPNORM_ALPHA * h + ff, ln3_g[l], ln3_b[l])
    return h


def setup_inputs(seed: int = 0) -> dict:
    key = jax.random.key(seed)
    ks = iter(jax.random.split(key, 40))
    f32 = jnp.float32

    def nrm(shape, scale):
        return jax.random.normal(next(ks), shape, f32) * scale

    def gain(shape):
        return 1.0 + nrm(shape, 0.02)

    d = D_MODEL
    col_scale = jnp.concatenate([
        jnp.ones((ATT_WIDTH + KV_WIDTH,), f32),
        jnp.full((KV_WIDTH + 2 * CONV_WIDTH,), DEEPNORM_BETA, f32)])
    return {
        'x_prompt': nrm((BATCH, SEQ, d), 1.0),
        'x_sample': nrm((DEC_BATCH, DEC_SEQ, d), 1.0),
        'mem_prompt': nrm((BATCH, N_MEM, d), 1.0),
        'mem_sample': nrm((DEC_BATCH, N_MEM, d), 1.0),
        'ln_in_g': gain((d,)),
        'ln_in_b': nrm((d,), 0.02),
        'ln_mem_g': gain((d,)),
        'ln_mem_b': nrm((d,), 0.02),
        'w_in': nrm((DEPTH, d, IN_WIDTH), d ** -0.5) * col_scale,
        'q_norm_g': gain((DEPTH, HEAD_DIM)),
        'k_norm_g': gain((DEPTH, HEAD_DIM)),
        'conv_w': nrm((DEPTH, CONV_KSIZE, CONV_WIDTH), CONV_KSIZE ** -0.5),
        'conv_b': nrm((DEPTH, CONV_WIDTH), 0.02),
        'conv_ln_g': gain((DEPTH, CONV_WIDTH)),
        'conv_ln_b': nrm((DEPTH, CONV_WIDTH), 0.02),
        'w_mix_out': nrm((DEPTH, MIX_WIDTH, d), MIX_WIDTH ** -0.5 * DEEPNORM_BETA),
        'ln1_g': gain((DEPTH, d)),
        'ln1_b': nrm((DEPTH, d), 0.02),
        'w_cq': nrm((DEPTH, d, d), d ** -0.5),
        'w_ck': nrm((DEPTH, d, d), d ** -0.5),
        'w_cv': nrm((DEPTH, d, d), d ** -0.5 * DEEPNORM_BETA),
        'w_co': nrm((DEPTH, d, d), d ** -0.5 * DEEPNORM_BETA),
        'ln2_g': gain((DEPTH, d)),
        'ln2_b': nrm((DEPTH, d), 0.02),
        'w_router_g': nrm((DEPTH, d, N_GROUPS), d ** -0.5),
        'b_router_g': nrm((DEPTH, N_GROUPS), 0.01),
        'w_router_e': nrm((DEPTH, N_GROUPS, d, EXPERTS_PER_GROUP), d ** -0.5),
        'b_router_e': nrm((DEPTH, N_GROUPS, EXPERTS_PER_GROUP), 0.01),
        'w_e_gate': nrm((DEPTH, N_EXPERTS, d, D_EXPERT), d ** -0.5),
        'w_e_up': nrm((DEPTH, N_EXPERTS, d, D_EXPERT), d ** -0.5 * DEEPNORM_BETA),
        'w_e_down': nrm((DEPTH, N_EXPERTS, D_EXPERT, d), D_EXPERT ** -0.5 * DEEPNORM_BETA),
        'ln3_g': gain((DEPTH, d)),
        'ln3_b': nrm((DEPTH, d), 0.02),
    }


def reference(x_prompt, x_sample, mem_prompt, mem_sample, ln_in_g, ln_in_b, ln_mem_g, ln_mem_b, w_in,
              q_norm_g, k_norm_g, conv_w, conv_b, conv_ln_g, conv_ln_b, w_mix_out, ln1_g, ln1_b,
              w_cq, w_ck, w_cv, w_co, ln2_g, ln2_b, w_router_g, b_router_g, w_router_e, b_router_e,
              w_e_gate, w_e_up, w_e_down, ln3_g, ln3_b):
    y_prompt = _encoder(x_prompt, mem_prompt, ln_in_g, ln_in_b, ln_mem_g, ln_mem_b, w_in, q_norm_g, k_norm_g,
                        conv_w, conv_b, conv_ln_g, conv_ln_b, w_mix_out, ln1_g, ln1_b, w_cq, w_ck, w_cv, w_co,
                        ln2_g, ln2_b, w_router_g, b_router_g, w_router_e, b_router_e, w_e_gate, w_e_up,
                        w_e_down, ln3_g, ln3_b)
    y_sample = _encoder(x_sample, mem_sample, ln_in_g, ln_in_b, ln_mem_g, ln_mem_b, w_in, q_norm_g, k_norm_g,
                        conv_w, conv_b, conv_ln_g, conv_ln_b, w_mix_out, ln1_g, ln1_b, w_cq, w_ck, w_cv, w_co,
                        ln2_g, ln2_b, w_router_g, b_router_g, w_router_e, b_router_e, w_e_gate, w_e_up,
                        w_e_down, ln3_g, ln3_b)
    return (y_prompt, y_sample)
```

```python
import functools

import jax
import jax.numpy as jnp
from jax import lax
from jax.experimental import pallas as pl
from jax.experimental.pallas import tpu as pltpu

F32 = jnp.float32
BF16 = jnp.bfloat16

D_MODEL = 1024
DEPTH = 1
GRID_W = 64
N_MEM = 256
ATT_HEADS = 8
ATT_KV_HEADS = 2
HEAD_DIM = 64
ATT_WIDTH = ATT_HEADS * HEAD_DIM
KV_WIDTH = ATT_KV_HEADS * HEAD_DIM
ROPE_THETA = 10000.0
ROPE_AXIS_DIM = HEAD_DIM // 2
CONV_WIDTH = D_MODEL - ATT_WIDTH
CONV_KSIZE = 31
CONV_PAD = CONV_KSIZE // 2
IN_WIDTH = ATT_WIDTH + 2 * KV_WIDTH + 2 * CONV_WIDTH
MEM_HEADS = 4
MEM_HEAD_DIM = D_MODEL // MEM_HEADS
N_GROUPS = 4
EXPERTS_PER_GROUP = 8
N_EXPERTS = N_GROUPS * EXPERTS_PER_GROUP
D_EXPERT = 256
LN_EPS = 1e-5
RMS_EPS = 1e-6
DEEPNORM_ALPHA = (2.0 * DEPTH) ** 0.25

LANES = 128
HALO = 16
VMEM_LIMIT = 56 * 1024 * 1024

TM_IN = 512
TC_CONV = 128
TQ_ATT = 128
TM_POST = 256
TM_MOE = 512


def _layer_norm(x, g, b):
    mu = jnp.mean(x, -1, keepdims=True)
    xc = x - mu
    var = jnp.mean(xc * xc, -1, keepdims=True)
    return xc * lax.rsqrt(var + LN_EPS) * g + b


def _sigmoid(x):
    return 1.0 / (1.0 + jnp.exp(-x))


def _rope(x, cos, sin_signed):
    lane = lax.broadcasted_iota(jnp.int32, x.shape, 1)
    nxt = pltpu.roll(x, LANES - 1, axis=1)
    prv = pltpu.roll(x, 1, axis=1)
    return x * cos + jnp.where((lane & 1) == 0, nxt, prv) * sin_signed


def _proj_in_kernel(x_ref, g_ref, b_ref, w_ref, qg_ref, kg_ref, cos_ref, sin_ref, hm_ref,
                    h_ref, q_ref, k_ref, v_ref, u_ref):
    h = _layer_norm(x_ref[...], g_ref[...], b_ref[...])
    h_ref[...] = h
    z = jnp.dot(h.astype(BF16), w_ref[...], preferred_element_type=F32)
    i1 = ATT_WIDTH
    i2 = i1 + KV_WIDTH
    i3 = i2 + KV_WIDTH
    i4 = i3 + CONV_WIDTH
    cos = cos_ref[...]
    sin = sin_ref[...]
    hm = hm_ref[...]
    for j in range(ATT_WIDTH // LANES):
        qj = z[:, j * LANES:(j + 1) * LANES]
        ms = jnp.dot((qj * qj).astype(BF16), hm, preferred_element_type=F32)
        qn = qj * lax.rsqrt(ms + RMS_EPS) * qg_ref[...]
        q_ref[:, j * LANES:(j + 1) * LANES] = (_rope(qn, cos, sin) * (HEAD_DIM ** -0.5)).astype(BF16)
    kj = z[:, i1:i2]
    ms = jnp.dot((kj * kj).astype(BF16), hm, preferred_element_type=F32)
    kn = kj * lax.rsqrt(ms + RMS_EPS) * kg_ref[...]
    k_ref[...] = _rope(kn, cos, sin).astype(BF16)
    v_ref[...] = z[:, i2:i3].astype(BF16)
    u_ref[...] = (z[:, i3:i4] * _sigmoid(z[:, i4:])).astype(BF16)


def _proj_in(x, ln_g, ln_b, w_in, qg, kg, cos, sin, hm, seq):
    t = x.shape[0]
    tm = TM_IN
    nseq = seq // tm
    row = lambda i: (i, 0)
    const = lambda i: (0, 0)
    return pl.pallas_call(
        _proj_in_kernel,
        grid=(t // tm,),
        in_specs=[
            pl.BlockSpec((tm, D_MODEL), row),
            pl.BlockSpec((1, D_MODEL), const),
            pl.BlockSpec((1, D_MODEL), const),
            pl.BlockSpec((D_MODEL, IN_WIDTH), const),
            pl.BlockSpec((1, LANES), const),
            pl.BlockSpec((1, LANES), const),
            pl.BlockSpec((tm, LANES), lambda i: (i % nseq, 0)),
            pl.BlockSpec((tm, LANES), lambda i: (i % nseq, 0)),
            pl.BlockSpec((LANES, LANES), const),
        ],
        out_specs=[
            pl.BlockSpec((tm, D_MODEL), row),
            pl.BlockSpec((tm, ATT_WIDTH), row),
            pl.BlockSpec((tm, KV_WIDTH), row),
            pl.BlockSpec((tm, KV_WIDTH), row),
            pl.BlockSpec((tm, CONV_WIDTH), row),
        ],
        out_shape=[
            jax.ShapeDtypeStruct((t, D_MODEL), F32),
            jax.ShapeDtypeStruct((t, ATT_WIDTH), BF16),
            jax.ShapeDtypeStruct((t, KV_WIDTH), BF16),
            jax.ShapeDtypeStruct((t, KV_WIDTH), BF16),
            jax.ShapeDtypeStruct((t, CONV_WIDTH), BF16),
        ],
        compiler_params=pltpu.CompilerParams(dimension_semantics=("parallel",), vmem_limit_bytes=VMEM_LIMIT),
        name="proj_in",
    )(x, ln_g, ln_b, w_in, qg, kg, cos, sin, hm)


def _conv_kernel(u_ref, w_ref, cb_ref, g_ref, b_ref, o_ref, win_ref, *, seq):
    i = pl.program_id(1)
    tc = TC_CONV
    t0 = pl.multiple_of(i * tc, tc)
    top_start = pl.multiple_of(jnp.maximum(t0 - HALO, 0), HALO)
    bot_start = pl.multiple_of(jnp.minimum(t0 + tc, seq - HALO), HALO)
    top = u_ref[pl.ds(top_start, HALO), :].astype(F32)
    bot = u_ref[pl.ds(bot_start, HALO), :].astype(F32)
    win_ref[0:HALO, :] = jnp.where(i > 0, top, 0.0)
    win_ref[HALO:HALO + tc, :] = u_ref[pl.ds(t0, tc), :].astype(F32)
    win_ref[HALO + tc:, :] = jnp.where(i < pl.num_programs(1) - 1, bot, 0.0)
    w = w_ref[...]
    cols = []
    for c in range(CONV_WIDTH // LANES):
        cs = slice(c * LANES, (c + 1) * LANES)
        acc = jnp.zeros((tc, LANES), F32)
        for j in range(CONV_KSIZE):
            off = HALO - CONV_PAD + j
            acc = acc + win_ref[off:off + tc, cs] * w[j:j + 1, cs]
        cols.append(acc)
    y = jnp.concatenate(cols, axis=1) + cb_ref[...]
    y = _layer_norm(y, g_ref[...], b_ref[...])
    o_ref[...] = (y * _sigmoid(y)).astype(BF16)


def _conv_branch(u, conv_w, conv_b, cln_g, cln_b):
    b, seq, _ = u.shape
    tc = TC_CONV
    const = lambda bi, i: (0, 0)
    return pl.pallas_call(
        functools.partial(_conv_kernel, seq=seq),
        grid=(b, seq // tc),
        in_specs=[
            pl.BlockSpec((None, seq, CONV_WIDTH), lambda bi, i: (bi, 0, 0)),
            pl.BlockSpec((CONV_KSIZE, CONV_WIDTH), const),
            pl.BlockSpec((1, CONV_WIDTH), const),
            pl.BlockSpec((1, CONV_WIDTH), const),
            pl.BlockSpec((1, CONV_WIDTH), const),
        ],
        out_specs=pl.BlockSpec((None, tc, CONV_WIDTH), lambda bi, i: (bi, i, 0)),
        out_shape=jax.ShapeDtypeStruct((b, seq, CONV_WIDTH), BF16),
        scratch_shapes=[pltpu.VMEM((tc + 2 * HALO, CONV_WIDTH), F32)],
        compiler_params=pltpu.CompilerParams(dimension_semantics=("parallel", "arbitrary"),
                                             vmem_limit_bytes=VMEM_LIMIT),
        name="conv_branch",
    )(u, conv_w, conv_b, cln_g, cln_b)


def _gqa_kernel(q_ref, k_ref, v_ref, o_ref):
    rep = ATT_HEADS // ATT_KV_HEADS
    outs = []
    for g in range(ATT_KV_HEADS):
        kg = k_ref[:, g * HEAD_DIM:(g + 1) * HEAD_DIM]
        vg = v_ref[:, g * HEAD_DIM:(g + 1) * HEAD_DIM]
        for r in range(rep):
            hd = g * rep + r
            qh = q_ref[:, hd * HEAD_DIM:(hd + 1) * HEAD_DIM]
            s = lax.dot_general(qh, kg, (((1,), (1,)), ((), ())), preferred_element_type=F32)
            p = jnp.exp(s - jnp.max(s, -1, keepdims=True))
            l = jnp.sum(p, -1, keepdims=True)
            o = jnp.dot(p.astype(BF16), vg, preferred_element_type=F32)
            outs.append(o / l)
    o_ref[...] = jnp.concatenate(outs, axis=1).astype(BF16)


def _gqa(q, k, v):
    b, seq, _ = q.shape
    tq = TQ_ATT
    return pl.pallas_call(
        _gqa_kernel,
        grid=(b, seq // tq),
        in_specs=[
            pl.BlockSpec((None, tq, ATT_WIDTH), lambda bi, i: (bi, i, 0)),
            pl.BlockSpec((None, seq, KV_WIDTH), lambda bi, i: (bi, 0, 0)),
            pl.BlockSpec((None, seq, KV_WIDTH), lambda bi, i: (bi, 0, 0)),
        ],
        out_specs=pl.BlockSpec((None, tq, ATT_WIDTH), lambda bi, i: (bi, i, 0)),
        out_shape=jax.ShapeDtypeStruct((b, seq, ATT_WIDTH), BF16),
        compiler_params=pltpu.CompilerParams(dimension_semantics=("parallel", "arbitrary"),
                                             vmem_limit_bytes=VMEM_LIMIT),
        name="gqa",
    )(q, k, v)


def _mem_kv_kernel(m_ref, g_ref, b_ref, wk_ref, wv_ref, k_ref, v_ref):
    m = _layer_norm(m_ref[...], g_ref[...], b_ref[...]).astype(BF16)
    k_ref[...] = jnp.dot(m, wk_ref[...], preferred_element_type=F32).astype(BF16)
    v_ref[...] = jnp.dot(m, wv_ref[...], preferred_element_type=F32).astype(BF16)


def _mem_kv(mem, ln_g, ln_b, w_ck, w_cv):
    b = mem.shape[0]
    const = lambda bi: (0, 0)
    blk = pl.BlockSpec((None, N_MEM, D_MODEL), lambda bi: (bi, 0, 0))
    return pl.pallas_call(
        _mem_kv_kernel,
        grid=(b,),
        in_specs=[blk, pl.BlockSpec((1, D_MODEL), const), pl.BlockSpec((1, D_MODEL), const),
                  pl.BlockSpec((D_MODEL, D_MODEL), const), pl.BlockSpec((D_MODEL, D_MODEL), const)],
        out_specs=[blk, blk],
        out_shape=[jax.ShapeDtypeStruct((b, N_MEM, D_MODEL), BF16)] * 2,
        compiler_params=pltpu.CompilerParams(dimension_semantics=("parallel",), vmem_limit_bytes=VMEM_LIMIT),
        name="mem_kv",
    )(mem, ln_g, ln_b, w_ck, w_cv)


def _post_kernel(att_ref, c_ref, h0_ref, km_ref, vm_ref, wo_ref, g1_ref, b1_ref, wq_ref, wco_ref,
                 g2_ref, b2_ref, wr_hi_ref, wr_lo_ref, br_ref, h2_ref, t_ref, comb_ref):
    mixed = jnp.concatenate([att_ref[...], c_ref[...]], axis=1)
    mix = jnp.dot(mixed, wo_ref[...], preferred_element_type=F32)
    h1 = _layer_norm(DEEPNORM_ALPHA * h0_ref[...] + mix, g1_ref[...], b1_ref[...])
    qx = (jnp.dot(h1.astype(BF16), wq_ref[...], preferred_element_type=F32) * (MEM_HEAD_DIM ** -0.5)).astype(BF16)
    heads = []
    for hd in range(MEM_HEADS):
        hs = slice(hd * MEM_HEAD_DIM, (hd + 1) * MEM_HEAD_DIM)
        s = lax.dot_general(qx[:, hs], km_ref[:, hs], (((1,), (1,)), ((), ())), preferred_element_type=F32)
        p = jnp.exp(s - jnp.max(s, -1, keepdims=True))
        l = jnp.sum(p, -1, keepdims=True)
        heads.append(jnp.dot(p.astype(BF16), vm_ref[:, hs], preferred_element_type=F32) / l)
    o = jnp.concatenate(heads, axis=1).astype(BF16)
    xa = jnp.dot(o, wco_ref[...], preferred_element_type=F32)
    h2 = _layer_norm(DEEPNORM_ALPHA * h1 + xa, g2_ref[...], b2_ref[...])
    h2_ref[...] = h2
    t_hi = h2.astype(BF16)
    t_ref[...] = t_hi
    t_lo = (h2 - t_hi.astype(F32)).astype(BF16)
    logits = (jnp.dot(t_hi, wr_hi_ref[...], preferred_element_type=F32)
              + jnp.dot(t_lo, wr_hi_ref[...], preferred_element_type=F32)
              + jnp.dot(t_hi, wr_lo_ref[...], preferred_element_type=F32)) + br_ref[...]
    lane = lax.broadcasted_iota(jnp.int32, logits.shape, 1)
    neg = -jnp.inf
    big = jnp.int32(LANES)
    gl = jnp.where(lane < N_GROUPS, logits, neg)
    gmax = jnp.max(gl, -1, keepdims=True)
    p_grp = 1.0 / jnp.sum(jnp.exp(gl - gmax), -1, keepdims=True)
    grp = jnp.min(jnp.where(gl == gmax, lane, big), -1, keepdims=True)
    lo = N_GROUPS + EXPERTS_PER_GROUP * grp
    el = jnp.where((lane >= lo) & (lane < lo + EXPERTS_PER_GROUP), logits, neg)
    v1 = jnp.max(el, -1, keepdims=True)
    i1 = jnp.min(jnp.where(el == v1, lane, big), -1, keepdims=True)
    el2 = jnp.where(lane == i1, neg, el)
    v2 = jnp.max(el2, -1, keepdims=True)
    i2 = jnp.min(jnp.where(el2 == v2, lane, big), -1, keepdims=True)
    e21 = jnp.exp(v2 - v1)
    w1 = p_grp / (1.0 + e21)
    w2 = p_grp * e21 / (1.0 + e21)
    comb_ref[...] = (jnp.where(lane == i1 - N_GROUPS, w1, 0.0) + jnp.where(lane == i2 - N_GROUPS, w2, 0.0))


def _post(att, c, h0, km, vm, w_out, g1, b1, w_cq, w_co, g2, b2, wr_hi, wr_lo, br):
    b, seq, _ = att.shape
    tm = TM_POST
    tok = lambda width: pl.BlockSpec((None, tm, width), lambda bi, i: (bi, i, 0))
    memb = pl.BlockSpec((None, N_MEM, D_MODEL), lambda bi, i: (bi, 0, 0))
    const = lambda shape: pl.BlockSpec(shape, lambda bi, i: (0, 0))
    vec = const((1, D_MODEL))
    sq = const((D_MODEL, D_MODEL))
    return pl.pallas_call(
        _post_kernel,
        grid=(b, seq // tm),
        in_specs=[tok(ATT_WIDTH), tok(CONV_WIDTH), tok(D_MODEL), memb, memb, sq, vec, vec, sq, sq, vec, vec,
                  const((D_MODEL, LANES)), const((D_MODEL, LANES)), const((1, LANES))],
        out_specs=[tok(D_MODEL), tok(D_MODEL), tok(LANES)],
        out_shape=[jax.ShapeDtypeStruct((b, seq, D_MODEL), F32),
                   jax.ShapeDtypeStruct((b, seq, D_MODEL), BF16),
                   jax.ShapeDtypeStruct((b, seq, LANES), F32)],
        compiler_params=pltpu.CompilerParams(dimension_semantics=("parallel", "arbitrary"),
                                             vmem_limit_bytes=VMEM_LIMIT),
        name="post",
    )(att, c, h0, km, vm, w_out, g1, b1, w_cq, w_co, g2, b2, wr_hi, wr_lo, br)


def _moe_kernel(t_ref, comb_ref, h2_ref, wg_ref, wu_ref, wd_ref, g3_ref, b3_ref, o_ref, acc_ref):
    g = pl.program_id(1)

    @pl.when(g == 0)
    def _():
        acc_ref[...] = jnp.zeros_like(acc_ref)

    t = t_ref[...]
    comb = comb_ref[...]
    lane = lax.broadcasted_iota(jnp.int32, comb.shape, 1)
    for e in range(EXPERTS_PER_GROUP):
        col = jnp.sum(jnp.where(lane == g * EXPERTS_PER_GROUP + e, comb, 0.0), -1, keepdims=True)
        gate = jnp.dot(t, wg_ref[e], preferred_element_type=F32)
        up = jnp.dot(t, wu_ref[e], preferred_element_type=F32)
        he = (gate * _sigmoid(gate) * up * col).astype(BF16)
        acc_ref[...] += jnp.dot(he, wd_ref[e], preferred_element_type=F32)

    @pl.when(g == pl.num_programs(1) - 1)
    def _():
        o_ref[...] = _layer_norm(DEEPNORM_ALPHA * h2_ref[...] + acc_ref[...], g3_ref[...], b3_ref[...])


def _moe(t, comb, h2, w_gate, w_up, w_down, g3, b3):
    n = t.shape[0]
    tm = TM_MOE
    row = lambda width: pl.BlockSpec((tm, width), lambda i, g: (i, 0))
    vec = pl.BlockSpec((1, D_MODEL), lambda i, g: (0, 0))
    return pl.pallas_call(
        _moe_kernel,
        grid=(n // tm, N_GROUPS),
        in_specs=[row(D_MODEL), row(LANES), row(D_MODEL),
                  pl.BlockSpec((EXPERTS_PER_GROUP, D_MODEL, D_EXPERT), lambda i, g: (g, 0, 0)),
                  pl.BlockSpec((EXPERTS_PER_GROUP, D_MODEL, D_EXPERT), lambda i, g: (g, 0, 0)),
                  pl.BlockSpec((EXPERTS_PER_GROUP, D_EXPERT, D_MODEL), lambda i, g: (g, 0, 0)),
                  vec, vec],
        out_specs=row(D_MODEL),
        out_shape=jax.ShapeDtypeStruct((n, D_MODEL), F32),
        scratch_shapes=[pltpu.VMEM((tm, D_MODEL), F32)],
        compiler_params=pltpu.CompilerParams(dimension_semantics=("parallel", "arbitrary"),
                                             vmem_limit_bytes=VMEM_LIMIT),
        name="moe",
    )(t, comb, h2, w_gate, w_up, w_down, g3, b3)


def _rope_tables(seq):
    rows = seq // GRID_W
    row = jnp.repeat(jnp.arange(rows, dtype=F32), GRID_W)
    col = jnp.tile(jnp.arange(GRID_W, dtype=F32), rows)
    inv_freq = ROPE_THETA ** (-jnp.arange(0, ROPE_AXIS_DIM, 2, dtype=F32) / ROPE_AXIS_DIM)
    ang = jnp.concatenate([row[:, None] * inv_freq, col[:, None] * inv_freq], -1)
    cos = jnp.repeat(jnp.cos(ang), 2, axis=-1)
    sin = jnp.repeat(jnp.sin(ang), 2, axis=-1)
    sign = jnp.tile(jnp.array([-1.0, 1.0], F32), HEAD_DIM // 2)
    sin = sin * sign
    return jnp.tile(cos, (1, LANES // HEAD_DIM)), jnp.tile(sin, (1, LANES // HEAD_DIM))


def _encoder(x, mem, p):
    b, seq, d = x.shape
    cos, sin = _rope_tables(seq)
    h0, q, k, v, u = _proj_in(x.reshape(b * seq, d), p["ln_in_g"], p["ln_in_b"], p["w_in"], p["qg"], p["kg"],
                              cos, sin, p["head_mean"], seq)
    c = _conv_branch(u.reshape(b, seq, CONV_WIDTH), p["conv_w"], p["conv_b"], p["cln_g"], p["cln_b"])
    att = _gqa(q.reshape(b, seq, ATT_WIDTH), k.reshape(b, seq, KV_WIDTH), v.reshape(b, seq, KV_WIDTH))
    km, vm = _mem_kv(mem, p["ln_mem_g"], p["ln_mem_b"], p["w_ck"], p["w_cv"])
    h2, t, comb = _post(att, c, h0.reshape(b, seq, d), km, vm, p["w_out"], p["ln1_g"], p["ln1_b"],
                        p["w_cq"], p["w_co"], p["ln2_g"], p["ln2_b"], p["wr_hi"], p["wr_lo"], p["br"])
    y = _moe(t.reshape(b * seq, d), comb.reshape(b * seq, LANES), h2.reshape(b * seq, d),
             p["w_gate"], p["w_up"], p["w_down"], p["ln3_g"], p["ln3_b"])
    return y.reshape(b, seq, d)


def kernel(x_prompt, x_sample, mem_prompt, mem_sample, ln_in_g, ln_in_b, ln_mem_g, ln_mem_b, w_in, q_norm_g, k_norm_g, conv_w, conv_b, conv_ln_g, conv_ln_b, w_mix_out, ln1_g, ln1_b, w_cq, w_ck, w_cv, w_co, ln2_g, ln2_b, w_router_g, b_router_g, w_router_e, b_router_e, w_e_gate, w_e_up, w_e_down, ln3_g, ln3_b):
    l = 0
    vec = lambda a: a.reshape(1, -1).astype(F32)
    w_r = jnp.concatenate([w_router_g[l], jnp.transpose(w_router_e[l], (1, 0, 2)).reshape(D_MODEL, N_EXPERTS)], axis=1)
    w_r = jnp.pad(w_r, ((0, 0), (0, LANES - w_r.shape[1])))
    wr_hi = w_r.astype(BF16)
    wr_lo = (w_r - wr_hi.astype(F32)).astype(BF16)
    b_r = jnp.concatenate([b_router_g[l], b_router_e[l].reshape(-1)])
    b_r = jnp.pad(b_r, (0, LANES - b_r.shape[0])).reshape(1, LANES)
    head = jnp.arange(LANES) // HEAD_DIM
    head_mean = jnp.where(head[:, None] == head[None, :], 1.0 / HEAD_DIM, 0.0).astype(BF16)
    p = {
        "ln_in_g": vec(ln_in_g), "ln_in_b": vec(ln_in_b), "ln_mem_g": vec(ln_mem_g), "ln_mem_b": vec(ln_mem_b),
        "w_in": w_in[l].astype(BF16),
        "qg": jnp.tile(q_norm_g[l], LANES // HEAD_DIM).reshape(1, LANES),
        "kg": jnp.tile(k_norm_g[l], LANES // HEAD_DIM).reshape(1, LANES),
        "head_mean": head_mean,
        "conv_w": conv_w[l], "conv_b": vec(conv_b[l]), "cln_g": vec(conv_ln_g[l]), "cln_b": vec(conv_ln_b[l]),
        "w_out": w_mix_out[l].astype(BF16), "ln1_g": vec(ln1_g[l]), "ln1_b": vec(ln1_b[l]),
        "w_cq": w_cq[l].astype(BF16), "w_ck": w_ck[l].astype(BF16), "w_cv": w_cv[l].astype(BF16),
        "w_co": w_co[l].astype(BF16), "ln2_g": vec(ln2_g[l]), "ln2_b": vec(ln2_b[l]),
        "wr_hi": wr_hi, "wr_lo": wr_lo, "br": b_r,
        "w_gate": w_e_gate[l].astype(BF16), "w_up": w_e_up[l].astype(BF16), "w_down": w_e_down[l].astype(BF16),
        "ln3_g": vec(ln3_g[l]), "ln3_b": vec(ln3_b[l]),
    }
    return (_encoder(x_prompt, mem_prompt, p), _encoder(x_sample, mem_sample, p))
```

```python
import functools

import jax
import jax.numpy as jnp
from jax import lax
from jax.experimental import pallas as pl
from jax.experimental.pallas import tpu as pltpu

F32 = jnp.float32
BF16 = jnp.bfloat16

D_MODEL = 1024
DEPTH = 1
GRID_W = 64
N_MEM = 256
ATT_HEADS = 8
ATT_KV_HEADS = 2
HEAD_DIM = 64
ATT_WIDTH = ATT_HEADS * HEAD_DIM
KV_WIDTH = ATT_KV_HEADS * HEAD_DIM
ROPE_THETA = 10000.0
ROPE_AXIS_DIM = HEAD_DIM // 2
CONV_WIDTH = D_MODEL - ATT_WIDTH
CONV_KSIZE = 31
CONV_PAD = CONV_KSIZE // 2
IN_WIDTH = ATT_WIDTH + 2 * KV_WIDTH + 2 * CONV_WIDTH
MEM_HEADS = 4
MEM_HEAD_DIM = D_MODEL // MEM_HEADS
N_GROUPS = 4
EXPERTS_PER_GROUP = 8
N_EXPERTS = N_GROUPS * EXPERTS_PER_GROUP
D_EXPERT = 256
LN_EPS = 1e-5
RMS_EPS = 1e-6
DEEPNORM_ALPHA = (2.0 * DEPTH) ** 0.25
LOG2_E = 1.4426950408889634
Q_SCALE = HEAD_DIM ** -0.5 * LOG2_E

LANES = 128
HALO = 16
VMEM_LIMIT = 56 * 1024 * 1024

TM_IN = 512
TC_CONV = 128
TQ_ATT = 128
TM_POST = 256
TM_MOE = 1024
MOE_ROWS = 288
MOE_ROWS_PAD = 384
CODE_STRIDE = 2048
GRP_LANE = N_EXPERTS


def _layer_norm(x, g, b):
    mu = jnp.mean(x, -1, keepdims=True)
    xc = x - mu
    var = jnp.mean(xc * xc, -1, keepdims=True)
    return xc * lax.rsqrt(var + LN_EPS) * g + b


def _sigmoid(x):
    return 1.0 / (1.0 + jnp.exp(-x))


def _rope(x, cos, sin_signed):
    lane = lax.broadcasted_iota(jnp.int32, x.shape, 1)
    nxt = pltpu.roll(x, LANES - 1, axis=1)
    prv = pltpu.roll(x, 1, axis=1)
    return x * cos + jnp.where((lane & 1) == 0, nxt, prv) * sin_signed


def _proj_in_kernel(x_ref, g_ref, b_ref, w_ref, qg_ref, kg_ref, cos_ref, sin_ref, hm_ref,
                    h_ref, q_ref, k_ref, v_ref, u_ref):
    h = _layer_norm(x_ref[...], g_ref[...], b_ref[...])
    h_ref[...] = h
    z = jnp.dot(h.astype(BF16), w_ref[...], preferred_element_type=F32)
    i1 = ATT_WIDTH
    i2 = i1 + KV_WIDTH
    i3 = i2 + KV_WIDTH
    i4 = i3 + CONV_WIDTH
    cos = cos_ref[...]
    sin = sin_ref[...]
    hm = hm_ref[...]
    for j in range(ATT_WIDTH // LANES):
        qj = z[:, j * LANES:(j + 1) * LANES]
        ms = jnp.dot((qj * qj).astype(BF16), hm, preferred_element_type=F32)
        qn = qj * lax.rsqrt(ms + RMS_EPS) * qg_ref[...]
        q_ref[:, j * LANES:(j + 1) * LANES] = (_rope(qn, cos, sin) * Q_SCALE).astype(BF16)
    kj = z[:, i1:i2]
    ms = jnp.dot((kj * kj).astype(BF16), hm, preferred_element_type=F32)
    kn = kj * lax.rsqrt(ms + RMS_EPS) * kg_ref[...]
    k_ref[...] = _rope(kn, cos, sin).astype(BF16)
    v_ref[...] = z[:, i2:i3].astype(BF16)
    u_ref[...] = (z[:, i3:i4] * _sigmoid(z[:, i4:])).astype(BF16)


def _proj_in(x, ln_g, ln_b, w_in, qg, kg, cos, sin, hm, seq):
    t = x.shape[0]
    tm = TM_IN
    nseq = seq // tm
    row = lambda i: (i, 0)
    const = lambda i: (0, 0)
    return pl.pallas_call(
        _proj_in_kernel,
        grid=(t // tm,),
        in_specs=[
            pl.BlockSpec((tm, D_MODEL), row),
            pl.BlockSpec((1, D_MODEL), const),
            pl.BlockSpec((1, D_MODEL), const),
            pl.BlockSpec((D_MODEL, IN_WIDTH), const),
            pl.BlockSpec((1, LANES), const),
            pl.BlockSpec((1, LANES), const),
            pl.BlockSpec((tm, LANES), lambda i: (i % nseq, 0)),
            pl.BlockSpec((tm, LANES), lambda i: (i % nseq, 0)),
            pl.BlockSpec((LANES, LANES), const),
        ],
        out_specs=[
            pl.BlockSpec((tm, D_MODEL), row),
            pl.BlockSpec((tm, ATT_WIDTH), row),
            pl.BlockSpec((tm, KV_WIDTH), row),
            pl.BlockSpec((tm, KV_WIDTH), row),
            pl.BlockSpec((tm, CONV_WIDTH), row),
        ],
        out_shape=[
            jax.ShapeDtypeStruct((t, D_MODEL), F32),
            jax.ShapeDtypeStruct((t, ATT_WIDTH), BF16),
            jax.ShapeDtypeStruct((t, KV_WIDTH), BF16),
            jax.ShapeDtypeStruct((t, KV_WIDTH), BF16),
            jax.ShapeDtypeStruct((t, CONV_WIDTH), BF16),
        ],
        compiler_params=pltpu.CompilerParams(dimension_semantics=("parallel",), vmem_limit_bytes=VMEM_LIMIT),
        name="proj_in",
    )(x, ln_g, ln_b, w_in, qg, kg, cos, sin, hm)


def _conv_kernel(u_ref, w_ref, cb_ref, g_ref, b_ref, o_ref, win_ref, *, seq):
    i = pl.program_id(1)
    tc = TC_CONV
    t0 = pl.multiple_of(i * tc, tc)
    top_start = pl.multiple_of(jnp.maximum(t0 - HALO, 0), HALO)
    bot_start = pl.multiple_of(jnp.minimum(t0 + tc, seq - HALO), HALO)
    top = u_ref[pl.ds(top_start, HALO), :].astype(F32)
    bot = u_ref[pl.ds(bot_start, HALO), :].astype(F32)
    win_ref[0:HALO, :] = jnp.where(i > 0, top, 0.0)
    win_ref[HALO:HALO + tc, :] = u_ref[pl.ds(t0, tc), :].astype(F32)
    win_ref[HALO + tc:, :] = jnp.where(i < pl.num_programs(1) - 1, bot, 0.0)
    w = w_ref[...]
    cols = []
    for c in range(CONV_WIDTH // LANES):
        cs = slice(c * LANES, (c + 1) * LANES)
        acc = jnp.zeros((tc, LANES), F32)
        for j in range(CONV_KSIZE):
            off = HALO - CONV_PAD + j
            acc = acc + win_ref[off:off + tc, cs] * w[j:j + 1, cs]
        cols.append(acc)
    y = jnp.concatenate(cols, axis=1) + cb_ref[...]
    y = _layer_norm(y, g_ref[...], b_ref[...])
    o_ref[...] = (y * _sigmoid(y)).astype(BF16)


def _conv_branch(u, conv_w, conv_b, cln_g, cln_b):
    b, seq, _ = u.shape
    tc = TC_CONV
    const = lambda bi, i: (0, 0)
    return pl.pallas_call(
        functools.partial(_conv_kernel, seq=seq),
        grid=(b, seq // tc),
        in_specs=[
            pl.BlockSpec((None, seq, CONV_WIDTH), lambda bi, i: (bi, 0, 0)),
            pl.BlockSpec((CONV_KSIZE, CONV_WIDTH), const),
            pl.BlockSpec((1, CONV_WIDTH), const),
            pl.BlockSpec((1, CONV_WIDTH), const),
            pl.BlockSpec((1, CONV_WIDTH), const),
        ],
        out_specs=pl.BlockSpec((None, tc, CONV_WIDTH), lambda bi, i: (bi, i, 0)),
        out_shape=jax.ShapeDtypeStruct((b, seq, CONV_WIDTH), BF16),
        scratch_shapes=[pltpu.VMEM((tc + 2 * HALO, CONV_WIDTH), F32)],
        compiler_params=pltpu.CompilerParams(dimension_semantics=("parallel", "arbitrary"),
                                             vmem_limit_bytes=VMEM_LIMIT),
        name="conv_branch",
    )(u, conv_w, conv_b, cln_g, cln_b)


def _gqa_kernel(q_ref, k_ref, v_ref, o_ref):
    rep = ATT_HEADS // ATT_KV_HEADS
    tq = q_ref.shape[0]
    outs = []
    for g in range(ATT_KV_HEADS):
        kg = k_ref[:, g * HEAD_DIM:(g + 1) * HEAD_DIM]
        vg = v_ref[:, g * HEAD_DIM:(g + 1) * HEAD_DIM]
        qs = jnp.concatenate([q_ref[:, (g * rep + r) * HEAD_DIM:(g * rep + r + 1) * HEAD_DIM] for r in range(rep)],
                             axis=0)
        s = lax.dot_general(qs, kg, (((1,), (1,)), ((), ())), preferred_element_type=F32)
        p = jnp.exp2(s - jnp.max(s, -1, keepdims=True))
        l = jnp.sum(p, -1, keepdims=True)
        o = jnp.dot(p.astype(BF16), vg, preferred_element_type=F32) / l
        outs.extend(o[r * tq:(r + 1) * tq] for r in range(rep))
    o_ref[...] = jnp.concatenate(outs, axis=1).astype(BF16)


def _gqa(q, k, v):
    b, seq, _ = q.shape
    tq = TQ_ATT
    return pl.pallas_call(
        _gqa_kernel,
        grid=(b, seq // tq),
        in_specs=[
            pl.BlockSpec((None, tq, ATT_WIDTH), lambda bi, i: (bi, i, 0)),
            pl.BlockSpec((None, seq, KV_WIDTH), lambda bi, i: (bi, 0, 0)),
            pl.BlockSpec((None, seq, KV_WIDTH), lambda bi, i: (bi, 0, 0)),
        ],
        out_specs=pl.BlockSpec((None, tq, ATT_WIDTH), lambda bi, i: (bi, i, 0)),
        out_shape=jax.ShapeDtypeStruct((b, seq, ATT_WIDTH), BF16),
        compiler_params=pltpu.CompilerParams(dimension_semantics=("parallel", "arbitrary"),
                                             vmem_limit_bytes=VMEM_LIMIT),
        name="gqa",
    )(q, k, v)


def _mem_kv_kernel(m_ref, g_ref, b_ref, wk_ref, wv_ref, k_ref, v_ref):
    m = _layer_norm(m_ref[...], g_ref[...], b_ref[...]).astype(BF16)
    k_ref[...] = jnp.dot(m, wk_ref[...], preferred_element_type=F32).astype(BF16)
    v_ref[...] = jnp.dot(m, wv_ref[...], preferred_element_type=F32).astype(BF16)


def _mem_kv(mem, ln_g, ln_b, w_ck, w_cv):
    b = mem.shape[0]
    const = lambda bi: (0, 0)
    blk = pl.BlockSpec((None, N_MEM, D_MODEL), lambda bi: (bi, 0, 0))
    return pl.pallas_call(
        _mem_kv_kernel,
        grid=(b,),
        in_specs=[blk, pl.BlockSpec((1, D_MODEL), const), pl.BlockSpec((1, D_MODEL), const),
                  pl.BlockSpec((D_MODEL, D_MODEL), const), pl.BlockSpec((D_MODEL, D_MODEL), const)],
        out_specs=[blk, blk],
        out_shape=[jax.ShapeDtypeStruct((b, N_MEM, D_MODEL), BF16)] * 2,
        compiler_params=pltpu.CompilerParams(dimension_semantics=("parallel",), vmem_limit_bytes=VMEM_LIMIT),
        name="mem_kv",
    )(mem, ln_g, ln_b, w_ck, w_cv)


def _post_kernel(att_ref, c_ref, h0_ref, km_ref, vm_ref, wo_ref, g1_ref, b1_ref, wq_ref, wco_ref,
                 g2_ref, b2_ref, wr_hi_ref, wr_lo_ref, br_ref, h2_ref, t_ref, comb_ref):
    mixed = jnp.concatenate([att_ref[...], c_ref[...]], axis=1)
    mix = jnp.dot(mixed, wo_ref[...], preferred_element_type=F32)
    h1 = _layer_norm(DEEPNORM_ALPHA * h0_ref[...] + mix, g1_ref[...], b1_ref[...])
    qx = (jnp.dot(h1.astype(BF16), wq_ref[...], preferred_element_type=F32) * (MEM_HEAD_DIM ** -0.5)).astype(BF16)
    heads = []
    for hd in range(MEM_HEADS):
        hs = slice(hd * MEM_HEAD_DIM, (hd + 1) * MEM_HEAD_DIM)
        s = lax.dot_general(qx[:, hs], km_ref[:, hs], (((1,), (1,)), ((), ())), preferred_element_type=F32)
        p = jnp.exp(s - jnp.max(s, -1, keepdims=True))
        l = jnp.sum(p, -1, keepdims=True)
        heads.append(jnp.dot(p.astype(BF16), vm_ref[:, hs], preferred_element_type=F32) / l)
    o = jnp.concatenate(heads, axis=1).astype(BF16)
    xa = jnp.dot(o, wco_ref[...], preferred_element_type=F32)
    h2 = _layer_norm(DEEPNORM_ALPHA * h1 + xa, g2_ref[...], b2_ref[...])
    h2_ref[...] = h2
    t_hi = h2.astype(BF16)
    t_ref[...] = t_hi
    t_lo = (h2 - t_hi.astype(F32)).astype(BF16)
    logits = (jnp.dot(t_hi, wr_hi_ref[...], preferred_element_type=F32)
              + jnp.dot(t_lo, wr_hi_ref[...], preferred_element_type=F32)
              + jnp.dot(t_hi, wr_lo_ref[...], preferred_element_type=F32)) + br_ref[...]
    lane = lax.broadcasted_iota(jnp.int32, logits.shape, 1)
    neg = -jnp.inf
    big = jnp.int32(LANES)
    gl = jnp.where(lane < N_GROUPS, logits, neg)
    gmax = jnp.max(gl, -1, keepdims=True)
    p_grp = 1.0 / jnp.sum(jnp.exp(gl - gmax), -1, keepdims=True)
    grp = jnp.min(jnp.where(gl == gmax, lane, big), -1, keepdims=True)
    lo = N_GROUPS + EXPERTS_PER_GROUP * grp
    el = jnp.where((lane >= lo) & (lane < lo + EXPERTS_PER_GROUP), logits, neg)
    v1 = jnp.max(el, -1, keepdims=True)
    i1 = jnp.min(jnp.where(el == v1, lane, big), -1, keepdims=True)
    el2 = jnp.where(lane == i1, neg, el)
    v2 = jnp.max(el2, -1, keepdims=True)
    i2 = jnp.min(jnp.where(el2 == v2, lane, big), -1, keepdims=True)
    e21 = jnp.exp(v2 - v1)
    w1 = p_grp / (1.0 + e21)
    w2 = p_grp * e21 / (1.0 + e21)
    comb_ref[...] = (jnp.where(lane == i1 - N_GROUPS, w1, 0.0) + jnp.where(lane == i2 - N_GROUPS, w2, 0.0)
                     + jnp.where(lane == GRP_LANE, grp.astype(F32), 0.0))


def _post(att, c, h0, km, vm, w_out, g1, b1, w_cq, w_co, g2, b2, wr_hi, wr_lo, br):
    b, seq, _ = att.shape
    tm = TM_POST
    tok = lambda width: pl.BlockSpec((None, tm, width), lambda bi, i: (bi, i, 0))
    memb = pl.BlockSpec((None, N_MEM, D_MODEL), lambda bi, i: (bi, 0, 0))
    const = lambda shape: pl.BlockSpec(shape, lambda bi, i: (0, 0))
    vec = const((1, D_MODEL))
    sq = const((D_MODEL, D_MODEL))
    return pl.pallas_call(
        _post_kernel,
        grid=(b, seq // tm),
        in_specs=[tok(ATT_WIDTH), tok(CONV_WIDTH), tok(D_MODEL), memb, memb, sq, vec, vec, sq, sq, vec, vec,
                  const((D_MODEL, LANES)), const((D_MODEL, LANES)), const((1, LANES))],
        out_specs=[tok(D_MODEL), tok(D_MODEL), tok(LANES)],
        out_shape=[jax.ShapeDtypeStruct((b, seq, D_MODEL), F32),
                   jax.ShapeDtypeStruct((b, seq, D_MODEL), BF16),
                   jax.ShapeDtypeStruct((b, seq, LANES), F32)],
        compiler_params=pltpu.CompilerParams(dimension_semantics=("parallel", "arbitrary"),
                                             vmem_limit_bytes=VMEM_LIMIT),
        name="post",
    )(att, c, h0, km, vm, w_out, g1, b1, w_cq, w_co, g2, b2, wr_hi, wr_lo, br)


def _moe_kernel(cnt_ref, t_ref, crow_ref, ccol_ref, comb_ref, h2_ref, wg_ref, wu_ref, wd_ref, g3_ref, b3_ref,
                o_ref):
    i = pl.program_id(0)
    g = pl.program_id(1)
    tm = t_ref.shape[0]

    @pl.when(g == 0)
    def _():
        o_ref[...] = jnp.zeros_like(o_ref)

    comb = comb_ref[...]
    comb_hi = comb.astype(BF16)
    comb_lo = (comb - comb_hi.astype(F32)).astype(BF16)
    n_chunks = (cnt_ref[i * N_GROUPS + g] + MOE_ROWS - 1) // MOE_ROWS

    def chunk(c, carry):
        base = g * CODE_STRIDE + c * MOE_ROWS
        sub = lax.broadcasted_iota(jnp.int32, (MOE_ROWS, tm), 0)
        pick = jnp.where(crow_ref[...] == base + sub, 1.0, 0.0).astype(BF16)
        xg = jnp.dot(pick, t_ref[...], preferred_element_type=F32).astype(BF16)
        cw = (jnp.dot(pick, comb_hi, preferred_element_type=F32)
              + jnp.dot(pick, comb_lo, preferred_element_type=F32))
        lane = lax.broadcasted_iota(jnp.int32, cw.shape, 1)
        yg = jnp.zeros((MOE_ROWS, D_MODEL), F32)
        for e in range(EXPERTS_PER_GROUP):
            col = jnp.sum(jnp.where(lane == g * EXPERTS_PER_GROUP + e, cw, 0.0), -1, keepdims=True)
            gate = jnp.dot(xg, wg_ref[e], preferred_element_type=F32)
            up = jnp.dot(xg, wu_ref[e], preferred_element_type=F32)
            he = (gate * _sigmoid(gate) * up * col).astype(BF16)
            yg = yg + jnp.dot(he, wd_ref[e], preferred_element_type=F32)
        ln = lax.broadcasted_iota(jnp.int32, (tm, MOE_ROWS_PAD), 1)
        place = jnp.where((ccol_ref[...] == base + ln) & (ln < MOE_ROWS), 1.0, 0.0).astype(BF16)
        yg_pad = jnp.concatenate([yg.astype(BF16), jnp.zeros((MOE_ROWS_PAD - MOE_ROWS, D_MODEL), BF16)], axis=0)
        o_ref[...] += jnp.dot(place, yg_pad, preferred_element_type=F32)
        return carry

    lax.fori_loop(0, n_chunks, chunk, 0)

    @pl.when(g == pl.num_programs(1) - 1)
    def _():
        o_ref[...] = _layer_norm(DEEPNORM_ALPHA * h2_ref[...] + o_ref[...], g3_ref[...], b3_ref[...])


def _moe(t, comb, h2, w_gate, w_up, w_down, g3, b3):
    n = t.shape[0]
    tm = TM_MOE
    nt = n // tm
    grp = comb[:, GRP_LANE].astype(jnp.int32).reshape(nt, tm)
    onehot = (grp[..., None] == jnp.arange(N_GROUPS, dtype=jnp.int32)).astype(jnp.int32)
    incl = jnp.cumsum(onehot, axis=1)
    rank = jnp.sum((incl - onehot) * onehot, axis=-1)
    code = grp * CODE_STRIDE + rank
    counts = incl[:, -1, :].reshape(nt * N_GROUPS)

    row = lambda width, **kw: pl.BlockSpec((tm, width), lambda i, g, cnt: (i, 0), **kw)
    vec = pl.BlockSpec((1, D_MODEL), lambda i, g, cnt: (0, 0))
    wspec = lambda shape: pl.BlockSpec((EXPERTS_PER_GROUP,) + shape, lambda i, g, cnt: (g, 0, 0))
    return pl.pallas_call(
        _moe_kernel,
        grid_spec=pltpu.PrefetchScalarGridSpec(
            num_scalar_prefetch=1,
            grid=(nt, N_GROUPS),
            in_specs=[row(D_MODEL),
                      pl.BlockSpec((None, 1, tm), lambda i, g, cnt: (i, 0, 0)),
                      row(1), row(LANES),
                      row(D_MODEL, pipeline_mode=pl.Buffered(1)),
                      wspec((D_MODEL, D_EXPERT)), wspec((D_MODEL, D_EXPERT)), wspec((D_EXPERT, D_MODEL)),
                      vec, vec],
            out_specs=row(D_MODEL),
        ),
        out_shape=jax.ShapeDtypeStruct((n, D_MODEL), F32),
        compiler_params=pltpu.CompilerParams(dimension_semantics=("parallel", "arbitrary"),
                                             vmem_limit_bytes=VMEM_LIMIT),
        name="moe",
    )(counts, t, code.reshape(nt, 1, tm), code.reshape(n, 1), comb, h2, w_gate, w_up, w_down, g3, b3)


def _rope_tables(seq):
    rows = seq // GRID_W
    row = jnp.repeat(jnp.arange(rows, dtype=F32), GRID_W)
    col = jnp.tile(jnp.arange(GRID_W, dtype=F32), rows)
    inv_freq = ROPE_THETA ** (-jnp.arange(0, ROPE_AXIS_DIM, 2, dtype=F32) / ROPE_AXIS_DIM)
    ang = jnp.concatenate([row[:, None] * inv_freq, col[:, None] * inv_freq], -1)
    cos = jnp.repeat(jnp.cos(ang), 2, axis=-1)
    sin = jnp.repeat(jnp.sin(ang), 2, axis=-1)
    sign = jnp.tile(jnp.array([-1.0, 1.0], F32), HEAD_DIM // 2)
    sin = sin * sign
    return jnp.tile(cos, (1, LANES // HEAD_DIM)), jnp.tile(sin, (1, LANES // HEAD_DIM))


def _encoder(x, mem, p):
    b, seq, d = x.shape
    cos, sin = _rope_tables(seq)
    h0, q, k, v, u = _proj_in(x.reshape(b * seq, d), p["ln_in_g"], p["ln_in_b"], p["w_in"], p["qg"], p["kg"],
                              cos, sin, p["head_mean"], seq)
    c = _conv_branch(u.reshape(b, seq, CONV_WIDTH), p["conv_w"], p["conv_b"], p["cln_g"], p["cln_b"])
    att = _gqa(q.reshape(b, seq, ATT_WIDTH), k.reshape(b, seq, KV_WIDTH), v.reshape(b, seq, KV_WIDTH))
    km, vm = _mem_kv(mem, p["ln_mem_g"], p["ln_mem_b"], p["w_ck"], p["w_cv"])
    h2, t, comb = _post(att, c, h0.reshape(b, seq, d), km, vm, p["w_out"], p["ln1_g"], p["ln1_b"],
                        p["w_cq"], p["w_co"], p["ln2_g"], p["ln2_b"], p["wr_hi"], p["wr_lo"], p["br"])
    y = _moe(t.reshape(b * seq, d), comb.reshape(b * seq, LANES), h2.reshape(b * seq, d),
             p["w_gate"], p["w_up"], p["w_down"], p["ln3_g"], p["ln3_b"])
    return y.reshape(b, seq, d)


def kernel(x_prompt, x_sample, mem_prompt, mem_sample, ln_in_g, ln_in_b, ln_mem_g, ln_mem_b, w_in, q_norm_g, k_norm_g, conv_w, conv_b, conv_ln_g, conv_ln_b, w_mix_out, ln1_g, ln1_b, w_cq, w_ck, w_cv, w_co, ln2_g, ln2_b, w_router_g, b_router_g, w_router_e, b_router_e, w_e_gate, w_e_up, w_e_down, ln3_g, ln3_b):
    l = 0
    vec = lambda a: a.reshape(1, -1).astype(F32)
    w_r = jnp.concatenate([w_router_g[l], jnp.transpose(w_router_e[l], (1, 0, 2)).reshape(D_MODEL, N_EXPERTS)], axis=1)
    w_r = jnp.pad(w_r, ((0, 0), (0, LANES - w_r.shape[1])))
    wr_hi = w_r.astype(BF16)
    wr_lo = (w_r - wr_hi.astype(F32)).astype(BF16)
    b_r = jnp.concatenate([b_router_g[l], b_router_e[l].reshape(-1)])
    b_r = jnp.pad(b_r, (0, LANES - b_r.shape[0])).reshape(1, LANES)
    head = jnp.arange(LANES) // HEAD_DIM
    head_mean = jnp.where(head[:, None] == head[None, :], 1.0 / HEAD_DIM, 0.0).astype(BF16)
    p = {
        "ln_in_g": vec(ln_in_g), "ln_in_b": vec(ln_in_b), "ln_mem_g": vec(ln_mem_g), "ln_mem_b": vec(ln_mem_b),
        "w_in": w_in[l].astype(BF16),
        "qg": jnp.tile(q_norm_g[l], LANES // HEAD_DIM).reshape(1, LANES),
        "kg": jnp.tile(k_norm_g[l], LANES // HEAD_DIM).reshape(1, LANES),
        "head_mean": head_mean,
        "conv_w": conv_w[l], "conv_b": vec(conv_b[l]), "cln_g": vec(conv_ln_g[l]), "cln_b": vec(conv_ln_b[l]),
        "w_out": w_mix_out[l].astype(BF16), "ln1_g": vec(ln1_g[l]), "ln1_b": vec(ln1_b[l]),
        "w_cq": w_cq[l].astype(BF16), "w_ck": w_ck[l].astype(BF16), "w_cv": w_cv[l].astype(BF16),
        "w_co": w_co[l].astype(BF16), "ln2_g": vec(ln2_g[l]), "ln2_b": vec(ln2_b[l]),
        "wr_hi": wr_hi, "wr_lo": wr_lo, "br": b_r,
        "w_gate": w_e_gate[l].astype(BF16), "w_up": w_e_up[l].astype(BF16), "w_down": w_e_down[l].astype(BF16),
        "ln3_g": vec(ln3_g[l]), "ln3_b": vec(ln3_b[l]),
    }
    return (_encoder(x_prompt, mem_prompt, p), _encoder(x_sample, mem_sample, p))
```

```python
import functools

import jax
import jax.numpy as jnp
from jax import lax
from jax.experimental import pallas as pl
from jax.experimental.pallas import tpu as pltpu

F32 = jnp.float32
BF16 = jnp.bfloat16

D_MODEL = 1024
DEPTH = 1
GRID_W = 64
N_MEM = 256
ATT_HEADS = 8
ATT_KV_HEADS = 2
HEAD_DIM = 64
ATT_WIDTH = ATT_HEADS * HEAD_DIM
KV_WIDTH = ATT_KV_HEADS * HEAD_DIM
ROPE_THETA = 10000.0
ROPE_AXIS_DIM = HEAD_DIM // 2
CONV_WIDTH = D_MODEL - ATT_WIDTH
CONV_KSIZE = 31
CONV_PAD = CONV_KSIZE // 2
IN_WIDTH = ATT_WIDTH + 2 * KV_WIDTH + 2 * CONV_WIDTH
MEM_HEADS = 4
MEM_HEAD_DIM = D_MODEL // MEM_HEADS
N_GROUPS = 4
EXPERTS_PER_GROUP = 8
N_EXPERTS = N_GROUPS * EXPERTS_PER_GROUP
D_EXPERT = 256
LN_EPS = 1e-5
RMS_EPS = 1e-6
DEEPNORM_ALPHA = (2.0 * DEPTH) ** 0.25
LOG2_E = 1.4426950408889634
Q_SCALE = HEAD_DIM ** -0.5 * LOG2_E

LANES = 128
SUBLANES = 8
HALO = 16
VMEM_LIMIT = 56 * 1024 * 1024

TM_IN = 512
TC_CONV = 128
TQ_ATT = 128
KV_CHUNK = 512
ATT_SAFE_SPAN = 50.0
TM_POST = 512
TM_MOE = 1024
MOE_ROWS = 288
MOE_ROWS_PAD = 384
CODE_STRIDE = 2048
CODE_LANE = N_EXPERTS


def _layer_norm(x, g, b):
    mu = jnp.mean(x, -1, keepdims=True)
    xc = x - mu
    var = jnp.mean(xc * xc, -1, keepdims=True)
    return xc * lax.rsqrt(var + LN_EPS) * g + b


def _sigmoid(x):
    return 1.0 / (1.0 + jnp.exp(-x))


def _rope(x, cos, sin_signed):
    lane = lax.broadcasted_iota(jnp.int32, x.shape, 1)
    nxt = pltpu.roll(x, LANES - 1, axis=1)
    prv = pltpu.roll(x, 1, axis=1)
    return x * cos + jnp.where((lane & 1) == 0, nxt, prv) * sin_signed


def _proj_in_kernel(x_ref, g_ref, b_ref, w_ref, qg_ref, kg_ref, cos_ref, sin_ref, hm_ref,
                    h_ref, q_ref, k_ref, v_ref, u_ref):
    h = _layer_norm(x_ref[...], g_ref[...], b_ref[...])
    h_ref[...] = h
    z = jnp.dot(h.astype(BF16), w_ref[...], preferred_element_type=F32)
    i1 = ATT_WIDTH
    i2 = i1 + KV_WIDTH
    i3 = i2 + KV_WIDTH
    i4 = i3 + CONV_WIDTH
    cos = cos_ref[...]
    sin = sin_ref[...]
    hm = hm_ref[...]
    heads_per_tile = LANES // HEAD_DIM
    for j in range(ATT_WIDTH // LANES):
        qj = z[:, j * LANES:(j + 1) * LANES]
        ms = jnp.dot((qj * qj).astype(BF16), hm, preferred_element_type=F32)
        qn = qj * lax.rsqrt(ms + RMS_EPS) * qg_ref[...]
        qr = (_rope(qn, cos, sin) * Q_SCALE).astype(BF16)
        for r in range(heads_per_tile):
            q_ref[j * heads_per_tile + r] = qr[:, r * HEAD_DIM:(r + 1) * HEAD_DIM]
    kj = z[:, i1:i2]
    ms = jnp.dot((kj * kj).astype(BF16), hm, preferred_element_type=F32)
    kn = kj * lax.rsqrt(ms + RMS_EPS) * kg_ref[...]
    kr = _rope(kn, cos, sin).astype(BF16)
    vj = z[:, i2:i3].astype(BF16)
    ones = jnp.ones((vj.shape[0], HEAD_DIM), BF16)
    for g in range(ATT_KV_HEADS):
        k_ref[g] = kr[:, g * HEAD_DIM:(g + 1) * HEAD_DIM]
    v_ref[...] = jnp.concatenate([vj[:, :HEAD_DIM], ones, vj[:, HEAD_DIM:], ones], axis=1)
    u_ref[...] = (z[:, i3:i4] * _sigmoid(z[:, i4:])).astype(BF16)


def _proj_in(x, ln_g, ln_b, w_in, qg, kg, cos, sin, hm, seq):
    t = x.shape[0]
    tm = TM_IN
    nseq = seq // tm
    row = lambda i: (i, 0)
    const = lambda i: (0, 0)
    return pl.pallas_call(
        _proj_in_kernel,
        grid=(t // tm,),
        in_specs=[
            pl.BlockSpec((tm, D_MODEL), row),
            pl.BlockSpec((1, D_MODEL), const),
            pl.BlockSpec((1, D_MODEL), const),
            pl.BlockSpec((D_MODEL, IN_WIDTH), const),
            pl.BlockSpec((1, LANES), const),
            pl.BlockSpec((1, LANES), const),
            pl.BlockSpec((tm, LANES), lambda i: (i % nseq, 0)),
            pl.BlockSpec((tm, LANES), lambda i: (i % nseq, 0)),
            pl.BlockSpec((LANES, LANES), const),
        ],
        out_specs=[
            pl.BlockSpec((tm, D_MODEL), row),
            pl.BlockSpec((ATT_HEADS, tm, HEAD_DIM), lambda i: (0, i, 0)),
            pl.BlockSpec((ATT_KV_HEADS, tm, HEAD_DIM), lambda i: (0, i, 0)),
            pl.BlockSpec((tm, 2 * KV_WIDTH), row),
            pl.BlockSpec((tm, CONV_WIDTH), row),
        ],
        out_shape=[
            jax.ShapeDtypeStruct((t, D_MODEL), F32),
            jax.ShapeDtypeStruct((ATT_HEADS, t, HEAD_DIM), BF16),
            jax.ShapeDtypeStruct((ATT_KV_HEADS, t, HEAD_DIM), BF16),
            jax.ShapeDtypeStruct((t, 2 * KV_WIDTH), BF16),
            jax.ShapeDtypeStruct((t, CONV_WIDTH), BF16),
        ],
        compiler_params=pltpu.CompilerParams(dimension_semantics=("parallel",), vmem_limit_bytes=VMEM_LIMIT),
        name="proj_in",
    )(x, ln_g, ln_b, w_in, qg, kg, cos, sin, hm)


def _conv_kernel(u_ref, w_ref, cb_ref, g_ref, b_ref, o_ref, win_ref, sh_ref, *, seq):
    i = pl.program_id(1)
    tc = TC_CONV
    t0 = pl.multiple_of(i * tc, tc)
    top_start = pl.multiple_of(jnp.maximum(t0 - HALO, 0), HALO)
    bot_start = pl.multiple_of(jnp.minimum(t0 + tc, seq - HALO), HALO)
    top = u_ref[pl.ds(top_start, HALO), :].astype(F32)
    bot = u_ref[pl.ds(bot_start, HALO), :].astype(F32)
    win_ref[0:HALO, :] = jnp.where(i > 0, top, 0.0)
    win_ref[HALO:HALO + tc, :] = u_ref[pl.ds(t0, tc), :].astype(F32)
    win_ref[HALO + tc:, :] = jnp.where(i < pl.num_programs(1) - 1, bot, 0.0)
    sh_rows = sh_ref.shape[1]
    for s in range(1, SUBLANES):
        sh_ref[s] = win_ref[s:s + sh_rows, :]
    w = w_ref[...]
    cols = []
    for c in range(CONV_WIDTH // LANES):
        cs = slice(c * LANES, (c + 1) * LANES)
        acc = jnp.zeros((tc, LANES), F32)
        for j in range(CONV_KSIZE):
            off = HALO - CONV_PAD + j
            base, s = off - off % SUBLANES, off % SUBLANES
            tap = win_ref[base:base + tc, cs] if s == 0 else sh_ref[s, base:base + tc, cs]
            acc = acc + tap * w[j:j + 1, cs]
        cols.append(acc)
    y = jnp.concatenate(cols, axis=1) + cb_ref[...]
    y = _layer_norm(y, g_ref[...], b_ref[...])
    o_ref[...] = (y * _sigmoid(y)).astype(BF16)


def _conv_branch(u, conv_w, conv_b, cln_g, cln_b):
    b, seq, _ = u.shape
    tc = TC_CONV
    const = lambda bi, i: (0, 0)
    return pl.pallas_call(
        functools.partial(_conv_kernel, seq=seq),
        grid=(b, seq // tc),
        in_specs=[
            pl.BlockSpec((None, seq, CONV_WIDTH), lambda bi, i: (bi, 0, 0)),
            pl.BlockSpec((CONV_KSIZE, CONV_WIDTH), const),
            pl.BlockSpec((1, CONV_WIDTH), const),
            pl.BlockSpec((1, CONV_WIDTH), const),
            pl.BlockSpec((1, CONV_WIDTH), const),
        ],
        out_specs=pl.BlockSpec((None, tc, CONV_WIDTH), lambda bi, i: (bi, i, 0)),
        out_shape=jax.ShapeDtypeStruct((b, seq, CONV_WIDTH), BF16),
        scratch_shapes=[pltpu.VMEM((tc + 2 * HALO, CONV_WIDTH), F32),
                        pltpu.VMEM((SUBLANES, tc + 2 * HALO - SUBLANES, CONV_WIDTH), F32)],
        compiler_params=pltpu.CompilerParams(dimension_semantics=("parallel", "arbitrary"),
                                             vmem_limit_bytes=VMEM_LIMIT),
        name="conv_branch",
    )(u, conv_w, conv_b, cln_g, cln_b)


def _gqa_kernel(q_ref, k_ref, v_ref, o_ref, kmax_ref):
    rep = ATT_HEADS // ATT_KV_HEADS
    tq = q_ref.shape[1]
    seq = k_ref.shape[1]
    n_chunks = seq // KV_CHUNK
    nt_dims = (((1,), (1,)), ((), ()))

    @pl.when(pl.program_id(1) == 0)
    def _():
        for g in range(ATT_KV_HEADS):
            kf = k_ref[g].astype(F32)
            kmax_ref[g] = jnp.sqrt(jnp.max(jnp.sum(kf * kf, -1, keepdims=True), axis=0, keepdims=True))

    qs = [q_ref[g * rep:(g + 1) * rep].reshape(rep * tq, HEAD_DIM) for g in range(ATT_KV_HEADS)]
    bound = []
    for g in range(ATT_KV_HEADS):
        qf = qs[g].astype(F32)
        bound.append(jnp.sqrt(jnp.sum(qf * qf, -1, keepdims=True)) * kmax_ref[g])
    worst = jnp.maximum(jnp.max(bound[0]), jnp.max(bound[1]))

    def finish(acc):
        outs = []
        for g in range(ATT_KV_HEADS):
            o = acc[g] / pltpu.roll(acc[g], HEAD_DIM, axis=1)
            outs.extend(o[r * tq:(r + 1) * tq, :HEAD_DIM] for r in range(rep))
        o_ref[...] = jnp.concatenate(outs, axis=1).astype(BF16)

    @pl.when(worst <= ATT_SAFE_SPAN)
    def _():
        acc = [None] * ATT_KV_HEADS
        for c in range(n_chunks):
            rows = slice(c * KV_CHUNK, (c + 1) * KV_CHUNK)
            for g in range(ATT_KV_HEADS):
                s = lax.dot_general(qs[g], k_ref[g, rows], nt_dims, preferred_element_type=F32)
                p = jnp.exp2(s - bound[g]).astype(BF16)
                d = jnp.dot(p, v_ref[rows, g * LANES:(g + 1) * LANES], preferred_element_type=F32)
                acc[g] = d if c == 0 else acc[g] + d
        finish(acc)

    @pl.when(worst > ATT_SAFE_SPAN)
    def _():
        def step(c, carry):
            rows = pl.ds(pl.multiple_of(c * KV_CHUNK, KV_CHUNK), KV_CHUNK)
            new = []
            for g in range(ATT_KV_HEADS):
                m, acc = carry[g]
                s = lax.dot_general(qs[g], k_ref[g, rows], nt_dims, preferred_element_type=F32)
                m_new = jnp.maximum(m, jnp.max(s, -1, keepdims=True))
                p = jnp.exp2(s - m_new).astype(BF16)
                d = jnp.dot(p, v_ref[rows, g * LANES:(g + 1) * LANES], preferred_element_type=F32)
                new.append((m_new, jnp.exp2(m - m_new) * acc + d))
            return tuple(new)

        init = tuple((jnp.full((rep * tq, 1), -jnp.inf, F32), jnp.zeros((rep * tq, LANES), F32))
                     for _ in range(ATT_KV_HEADS))
        out = lax.fori_loop(0, n_chunks, step, init)
        finish([out[g][1] for g in range(ATT_KV_HEADS)])


def _gqa(q, k, v):
    _, b, seq, _ = q.shape
    tq = TQ_ATT
    return pl.pallas_call(
        _gqa_kernel,
        grid=(b, seq // tq),
        in_specs=[
            pl.BlockSpec((ATT_HEADS, None, tq, HEAD_DIM), lambda bi, i: (0, bi, i, 0)),
            pl.BlockSpec((ATT_KV_HEADS, None, seq, HEAD_DIM), lambda bi, i: (0, bi, 0, 0)),
            pl.BlockSpec((None, seq, ATT_KV_HEADS * LANES), lambda bi, i: (bi, 0, 0)),
        ],
        out_specs=pl.BlockSpec((None, tq, ATT_WIDTH), lambda bi, i: (bi, i, 0)),
        out_shape=jax.ShapeDtypeStruct((b, seq, ATT_WIDTH), BF16),
        scratch_shapes=[pltpu.VMEM((ATT_KV_HEADS, 1, 1), F32)],
        compiler_params=pltpu.CompilerParams(dimension_semantics=("arbitrary", "arbitrary"),
                                             vmem_limit_bytes=VMEM_LIMIT),
        name="gqa",
    )(q, k, v)


def _mem_kv_kernel(m_ref, g_ref, b_ref, wk_ref, wv_ref, k_ref, v_ref):
    m = _layer_norm(m_ref[...], g_ref[...], b_ref[...]).astype(BF16)
    k_ref[...] = jnp.dot(m, wk_ref[...], preferred_element_type=F32).astype(BF16)
    v_ref[...] = jnp.dot(m, wv_ref[...], preferred_element_type=F32).astype(BF16)


def _mem_kv(mem, ln_g, ln_b, w_ck, w_cv):
    b = mem.shape[0]
    const = lambda bi: (0, 0)
    blk = pl.BlockSpec((None, N_MEM, D_MODEL), lambda bi: (bi, 0, 0))
    return pl.pallas_call(
        _mem_kv_kernel,
        grid=(b,),
        in_specs=[blk, pl.BlockSpec((1, D_MODEL), const), pl.BlockSpec((1, D_MODEL), const),
                  pl.BlockSpec((D_MODEL, D_MODEL), const), pl.BlockSpec((D_MODEL, D_MODEL), const)],
        out_specs=[blk, blk],
        out_shape=[jax.ShapeDtypeStruct((b, N_MEM, D_MODEL), BF16)] * 2,
        compiler_params=pltpu.CompilerParams(dimension_semantics=("parallel",), vmem_limit_bytes=VMEM_LIMIT),
        name="mem_kv",
    )(mem, ln_g, ln_b, w_ck, w_cv)


def _post_kernel(att_ref, c_ref, h0_ref, km_ref, vm_ref, wo_ref, g1_ref, b1_ref, wq_ref, wco_ref,
                 g2_ref, b2_ref, wr_hi_ref, wr_lo_ref, br_ref, tri_ref, h2_ref, t_ref, comb_ref, cnt_ref):
    mixed = jnp.concatenate([att_ref[...], c_ref[...]], axis=1)
    mix = jnp.dot(mixed, wo_ref[...], preferred_element_type=F32)
    h1 = _layer_norm(DEEPNORM_ALPHA * h0_ref[...] + mix, g1_ref[...], b1_ref[...])
    qx = (jnp.dot(h1.astype(BF16), wq_ref[...], preferred_element_type=F32) * (MEM_HEAD_DIM ** -0.5)).astype(BF16)
    heads = []
    for hd in range(MEM_HEADS):
        hs = slice(hd * MEM_HEAD_DIM, (hd + 1) * MEM_HEAD_DIM)
        s = lax.dot_general(qx[:, hs], km_ref[:, hs], (((1,), (1,)), ((), ())), preferred_element_type=F32)
        p = jnp.exp(s - jnp.max(s, -1, keepdims=True))
        l = jnp.sum(p, -1, keepdims=True)
        heads.append(jnp.dot(p.astype(BF16), vm_ref[:, hs], preferred_element_type=F32) / l)
    o = jnp.concatenate(heads, axis=1).astype(BF16)
    xa = jnp.dot(o, wco_ref[...], preferred_element_type=F32)
    h2 = _layer_norm(DEEPNORM_ALPHA * h1 + xa, g2_ref[...], b2_ref[...])
    h2_ref[...] = h2
    t_hi = h2.astype(BF16)
    t_ref[...] = t_hi
    t_lo = (h2 - t_hi.astype(F32)).astype(BF16)
    logits = (jnp.dot(t_hi, wr_hi_ref[...], preferred_element_type=F32)
              + jnp.dot(t_lo, wr_hi_ref[...], preferred_element_type=F32)
              + jnp.dot(t_hi, wr_lo_ref[...], preferred_element_type=F32)) + br_ref[...]
    lane = lax.broadcasted_iota(jnp.int32, logits.shape, 1)
    neg = -jnp.inf
    big = jnp.int32(LANES)
    gl = jnp.where(lane < N_GROUPS, logits, neg)
    gmax = jnp.max(gl, -1, keepdims=True)
    p_grp = 1.0 / jnp.sum(jnp.exp(gl - gmax), -1, keepdims=True)
    grp = jnp.min(jnp.where(gl == gmax, lane, big), -1, keepdims=True)
    lo = N_GROUPS + EXPERTS_PER_GROUP * grp
    el = jnp.where((lane >= lo) & (lane < lo + EXPERTS_PER_GROUP), logits, neg)
    v1 = jnp.max(el, -1, keepdims=True)
    i1 = jnp.min(jnp.where(el == v1, lane, big), -1, keepdims=True)
    el2 = jnp.where(lane == i1, neg, el)
    v2 = jnp.max(el2, -1, keepdims=True)
    i2 = jnp.min(jnp.where(el2 == v2, lane, big), -1, keepdims=True)
    e21 = jnp.exp(v2 - v1)
    w1 = p_grp / (1.0 + e21)
    w2 = p_grp * e21 / (1.0 + e21)
    @pl.when(pl.program_id(1) % (TM_MOE // TM_POST) == 0)
    def _():
        cnt_ref[...] = jnp.zeros_like(cnt_ref)

    onehot = jnp.where(lane == grp, 1.0, 0.0)
    before = jnp.dot(tri_ref[...], onehot.astype(BF16), preferred_element_type=F32) + cnt_ref[...]
    rank = jnp.sum(onehot * before, -1, keepdims=True)
    cnt_ref[...] += jnp.sum(onehot, 0, keepdims=True)
    code = grp.astype(F32) * CODE_STRIDE + rank
    comb_ref[...] = (jnp.where(lane == i1 - N_GROUPS, w1, 0.0) + jnp.where(lane == i2 - N_GROUPS, w2, 0.0)
                     + jnp.where(lane == CODE_LANE, code, 0.0))


def _post(att, c, h0, km, vm, w_out, g1, b1, w_cq, w_co, g2, b2, wr_hi, wr_lo, br):
    b, seq, _ = att.shape
    tm = TM_POST
    tok = lambda width: pl.BlockSpec((None, tm, width), lambda bi, i: (bi, i, 0))
    memb = pl.BlockSpec((None, N_MEM, D_MODEL), lambda bi, i: (bi, 0, 0))
    const = lambda shape: pl.BlockSpec(shape, lambda bi, i: (0, 0))
    vec = const((1, D_MODEL))
    sq = const((D_MODEL, D_MODEL))
    assert seq % TM_MOE == 0 and TM_MOE % tm == 0
    idx = jnp.arange(tm, dtype=jnp.int32)
    tri = jnp.where(idx[:, None] > idx[None, :], 1.0, 0.0).astype(BF16)
    return pl.pallas_call(
        _post_kernel,
        grid=(b, seq // tm),
        in_specs=[tok(ATT_WIDTH), tok(CONV_WIDTH), tok(D_MODEL), memb, memb, sq, vec, vec, sq, sq, vec, vec,
                  const((D_MODEL, LANES)), const((D_MODEL, LANES)), const((1, LANES)), const((tm, tm))],
        out_specs=[tok(D_MODEL), tok(D_MODEL), tok(LANES)],
        out_shape=[jax.ShapeDtypeStruct((b, seq, D_MODEL), F32),
                   jax.ShapeDtypeStruct((b, seq, D_MODEL), BF16),
                   jax.ShapeDtypeStruct((b, seq, LANES), F32)],
        scratch_shapes=[pltpu.VMEM((1, LANES), F32)],
        compiler_params=pltpu.CompilerParams(dimension_semantics=("arbitrary", "arbitrary"),
                                             vmem_limit_bytes=VMEM_LIMIT),
        name="post",
    )(att, c, h0, km, vm, w_out, g1, b1, w_cq, w_co, g2, b2, wr_hi, wr_lo, br, tri)


def _moe_kernel(cnt_ref, t_ref, crow_ref, ccol_ref, comb_ref, h2_ref, wg_ref, wu_ref, wd_ref, g3_ref, b3_ref,
                o_ref):
    i = pl.program_id(0)
    g = pl.program_id(1)
    tm = t_ref.shape[0]

    @pl.when(g == 0)
    def _():
        o_ref[...] = jnp.zeros_like(o_ref)

    comb = comb_ref[...]
    comb_hi = comb.astype(BF16)
    comb_lo = (comb - comb_hi.astype(F32)).astype(BF16)
    n_chunks = (cnt_ref[i * N_GROUPS + g] + MOE_ROWS - 1) // MOE_ROWS

    def chunk(c, carry):
        base = g * CODE_STRIDE + c * MOE_ROWS
        sub = lax.broadcasted_iota(jnp.int32, (MOE_ROWS, tm), 0)
        pick = jnp.where(crow_ref[...] == base + sub, 1.0, 0.0).astype(BF16)
        xg = jnp.dot(pick, t_ref[...], preferred_element_type=F32).astype(BF16)
        cw = (jnp.dot(pick, comb_hi, preferred_element_type=F32)
              + jnp.dot(pick, comb_lo, preferred_element_type=F32))
        lane = lax.broadcasted_iota(jnp.int32, cw.shape, 1)
        yg = jnp.zeros((MOE_ROWS, D_MODEL), F32)
        for e in range(EXPERTS_PER_GROUP):
            col = jnp.sum(jnp.where(lane == g * EXPERTS_PER_GROUP + e, cw, 0.0), -1, keepdims=True)
            gate = jnp.dot(xg, wg_ref[e], preferred_element_type=F32)
            up = jnp.dot(xg, wu_ref[e], preferred_element_type=F32)
            he = (gate * _sigmoid(gate) * up * col).astype(BF16)
            yg = yg + jnp.dot(he, wd_ref[e], preferred_element_type=F32)
        ln = lax.broadcasted_iota(jnp.int32, (tm, MOE_ROWS_PAD), 1)
        place = jnp.where((ccol_ref[...] == base + ln) & (ln < MOE_ROWS), 1.0, 0.0).astype(BF16)
        yg_pad = jnp.concatenate([yg.astype(BF16), jnp.zeros((MOE_ROWS_PAD - MOE_ROWS, D_MODEL), BF16)], axis=0)
        o_ref[...] += jnp.dot(place, yg_pad, preferred_element_type=F32)
        return carry

    lax.fori_loop(0, n_chunks, chunk, 0)

    @pl.when(g == pl.num_programs(1) - 1)
    def _():
        o_ref[...] = _layer_norm(DEEPNORM_ALPHA * h2_ref[...] + o_ref[...], g3_ref[...], b3_ref[...])


def _moe(t, comb, h2, w_gate, w_up, w_down, g3, b3):
    n = t.shape[0]
    tm = TM_MOE
    nt = n // tm
    code = comb[:, CODE_LANE].astype(jnp.int32).reshape(nt, tm)
    counts = jnp.sum((code // CODE_STRIDE)[..., None] == jnp.arange(N_GROUPS, dtype=jnp.int32), axis=1,
                     dtype=jnp.int32).reshape(nt * N_GROUPS)

    row = lambda width, **kw: pl.BlockSpec((tm, width), lambda i, g, cnt: (i, 0), **kw)
    vec = pl.BlockSpec((1, D_MODEL), lambda i, g, cnt: (0, 0))
    wspec = lambda shape: pl.BlockSpec((EXPERTS_PER_GROUP,) + shape, lambda i, g, cnt: (g, 0, 0))
    return pl.pallas_call(
        _moe_kernel,
        grid_spec=pltpu.PrefetchScalarGridSpec(
            num_scalar_prefetch=1,
            grid=(nt, N_GROUPS),
            in_specs=[row(D_MODEL),
                      pl.BlockSpec((None, 1, tm), lambda i, g, cnt: (i, 0, 0)),
                      row(1), row(LANES),
                      row(D_MODEL, pipeline_mode=pl.Buffered(1)),
                      wspec((D_MODEL, D_EXPERT)), wspec((D_MODEL, D_EXPERT)), wspec((D_EXPERT, D_MODEL)),
                      vec, vec],
            out_specs=row(D_MODEL),
        ),
        out_shape=jax.ShapeDtypeStruct((n, D_MODEL), F32),
        compiler_params=pltpu.CompilerParams(dimension_semantics=("parallel", "arbitrary"),
                                             vmem_limit_bytes=VMEM_LIMIT),
        name="moe",
    )(counts, t, code.reshape(nt, 1, tm), code.reshape(n, 1), comb, h2, w_gate, w_up, w_down, g3, b3)


def _rope_tables(seq):
    rows = seq // GRID_W
    row = jnp.repeat(jnp.arange(rows, dtype=F32), GRID_W)
    col = jnp.tile(jnp.arange(GRID_W, dtype=F32), rows)
    inv_freq = ROPE_THETA ** (-jnp.arange(0, ROPE_AXIS_DIM, 2, dtype=F32) / ROPE_AXIS_DIM)
    ang = jnp.concatenate([row[:, None] * inv_freq, col[:, None] * inv_freq], -1)
    cos = jnp.repeat(jnp.cos(ang), 2, axis=-1)
    sin = jnp.repeat(jnp.sin(ang), 2, axis=-1)
    sign = jnp.tile(jnp.array([-1.0, 1.0], F32), HEAD_DIM // 2)
    sin = sin * sign
    return jnp.tile(cos, (1, LANES // HEAD_DIM)), jnp.tile(sin, (1, LANES // HEAD_DIM))


def _encoder(x, mem, p):
    b, seq, d = x.shape
    cos, sin = _rope_tables(seq)
    h0, q, k, v, u = _proj_in(x.reshape(b * seq, d), p["ln_in_g"], p["ln_in_b"], p["w_in"], p["qg"], p["kg"],
                              cos, sin, p["head_mean"], seq)
    c = _conv_branch(u.reshape(b, seq, CONV_WIDTH), p["conv_w"], p["conv_b"], p["cln_g"], p["cln_b"])
    att = _gqa(q.reshape(ATT_HEADS, b, seq, HEAD_DIM), k.reshape(ATT_KV_HEADS, b, seq, HEAD_DIM),
               v.reshape(b, seq, 2 * KV_WIDTH))
    km, vm = _mem_kv(mem, p["ln_mem_g"], p["ln_mem_b"], p["w_ck"], p["w_cv"])
    h2, t, comb = _post(att, c, h0.reshape(b, seq, d), km, vm, p["w_out"], p["ln1_g"], p["ln1_b"],
                        p["w_cq"], p["w_co"], p["ln2_g"], p["ln2_b"], p["wr_hi"], p["wr_lo"], p["br"])
    y = _moe(t.reshape(b * seq, d), comb.reshape(b * seq, LANES), h2.reshape(b * seq, d),
             p["w_gate"], p["w_up"], p["w_down"], p["ln3_g"], p["ln3_b"])
    return y.reshape(b, seq, d)


def kernel(x_prompt, x_sample, mem_prompt, mem_sample, ln_in_g, ln_in_b, ln_mem_g, ln_mem_b, w_in, q_norm_g, k_norm_g, conv_w, conv_b, conv_ln_g, conv_ln_b, w_mix_out, ln1_g, ln1_b, w_cq, w_ck, w_cv, w_co, ln2_g, ln2_b, w_router_g, b_router_g, w_router_e, b_router_e, w_e_gate, w_e_up, w_e_down, ln3_g, ln3_b):
    l = 0
    vec = lambda a: a.reshape(1, -1).astype(F32)
    w_r = jnp.concatenate([w_router_g[l], jnp.transpose(w_router_e[l], (1, 0, 2)).reshape(D_MODEL, N_EXPERTS)], axis=1)
    w_r = jnp.pad(w_r, ((0, 0), (0, LANES - w_r.shape[1])))
    wr_hi = w_r.astype(BF16)
    wr_lo = (w_r - wr_hi.astype(F32)).astype(BF16)
    b_r = jnp.concatenate([b_router_g[l], b_router_e[l].reshape(-1)])
    b_r = jnp.pad(b_r, (0, LANES - b_r.shape[0])).reshape(1, LANES)
    head = jnp.arange(LANES) // HEAD_DIM
    head_mean = jnp.where(head[:, None] == head[None, :], 1.0 / HEAD_DIM, 0.0).astype(BF16)
    p = {
        "ln_in_g": vec(ln_in_g), "ln_in_b": vec(ln_in_b), "ln_mem_g": vec(ln_mem_g), "ln_mem_b": vec(ln_mem_b),
        "w_in": w_in[l].astype(BF16),
        "qg": jnp.tile(q_norm_g[l], LANES // HEAD_DIM).reshape(1, LANES),
        "kg": jnp.tile(k_norm_g[l], LANES // HEAD_DIM).reshape(1, LANES),
        "head_mean": head_mean,
        "conv_w": conv_w[l], "conv_b": vec(conv_b[l]), "cln_g": vec(conv_ln_g[l]), "cln_b": vec(conv_ln_b[l]),
        "w_out": w_mix_out[l].astype(BF16), "ln1_g": vec(ln1_g[l]), "ln1_b": vec(ln1_b[l]),
        "w_cq": w_cq[l].astype(BF16), "w_ck": w_ck[l].astype(BF16), "w_cv": w_cv[l].astype(BF16),
        "w_co": w_co[l].astype(BF16), "ln2_g": vec(ln2_g[l]), "ln2_b": vec(ln2_b[l]),
        "wr_hi": wr_hi, "wr_lo": wr_lo, "br": b_r,
        "w_gate": w_e_gate[l].astype(BF16), "w_up": w_e_up[l].astype(BF16), "w_down": w_e_down[l].astype(BF16),
        "ln3_g": vec(ln3_g[l]), "ln3_b": vec(ln3_b[l]),
    }
    return (_encoder(x_prompt, mem_prompt, p), _encoder(x_sample, mem_sample, p))
```

```python
import functools

import jax
import jax.numpy as jnp
from jax import lax
from jax.experimental import pallas as pl
from jax.experimental.pallas import tpu as pltpu
from jax.experimental.pallas import tpu_sc as plsc

F32 = jnp.float32
BF16 = jnp.bfloat16

D_MODEL = 1024
DEPTH = 1
GRID_W = 64
N_MEM = 256
ATT_HEADS = 8
ATT_KV_HEADS = 2
HEAD_DIM = 64
ATT_WIDTH = ATT_HEADS * HEAD_DIM
KV_WIDTH = ATT_KV_HEADS * HEAD_DIM
ROPE_THETA = 10000.0
ROPE_AXIS_DIM = HEAD_DIM // 2
CONV_WIDTH = D_MODEL - ATT_WIDTH
CONV_KSIZE = 31
CONV_PAD = CONV_KSIZE // 2
IN_WIDTH = ATT_WIDTH + 2 * KV_WIDTH + 2 * CONV_WIDTH
MEM_HEADS = 4
MEM_HEAD_DIM = D_MODEL // MEM_HEADS
N_GROUPS = 4
EXPERTS_PER_GROUP = 8
N_EXPERTS = N_GROUPS * EXPERTS_PER_GROUP
D_EXPERT = 256
LN_EPS = 1e-5
RMS_EPS = 1e-6
DEEPNORM_ALPHA = (2.0 * DEPTH) ** 0.25
LOG2_E = 1.4426950408889634
Q_SCALE = HEAD_DIM ** -0.5 * LOG2_E

LANES = 128
SUBLANES = 8
HALO = 16
VMEM_LIMIT = 56 * 1024 * 1024

TM_IN = 512
TC_CONV = 128
TQ_ATT = 256
KV_CHUNK = 512
ATT_SAFE_SPAN = 50.0
TM_POST = 512
TR_EXPERT = 256
TM_COMBINE = 512
ROW_WORDS = 256
SC_WINDOW = 128
ROUTE_E, ROUTE_W, ROUTE_RANK = 0, 2, 4


def _layer_norm(x, g, b):
    mu = jnp.mean(x, -1, keepdims=True)
    xc = x - mu
    var = jnp.mean(xc * xc, -1, keepdims=True)
    return xc * lax.rsqrt(var + LN_EPS) * g + b


def _sigmoid(x):
    return 1.0 / (1.0 + jnp.exp(-x))


def _rope(x, cos, sin_signed):
    lane = lax.broadcasted_iota(jnp.int32, x.shape, 1)
    nxt = pltpu.roll(x, LANES - 1, axis=1)
    prv = pltpu.roll(x, 1, axis=1)
    return x * cos + jnp.where((lane & 1) == 0, nxt, prv) * sin_signed


def _proj_in_kernel(x_ref, g_ref, b_ref, w_ref, qg_ref, kg_ref, cos_ref, sin_ref, hm_ref,
                    h_ref, q_ref, k_ref, v_ref, u_ref):
    h = _layer_norm(x_ref[...], g_ref[...], b_ref[...])
    h_ref[...] = h
    z = jnp.dot(h.astype(BF16), w_ref[...], preferred_element_type=F32)
    i1 = ATT_WIDTH
    i2 = i1 + KV_WIDTH
    i3 = i2 + KV_WIDTH
    i4 = i3 + CONV_WIDTH
    cos = cos_ref[...]
    sin = sin_ref[...]
    hm = hm_ref[...]
    heads_per_tile = LANES // HEAD_DIM
    for j in range(ATT_WIDTH // LANES):
        qj = z[:, j * LANES:(j + 1) * LANES]
        ms = jnp.dot((qj * qj).astype(BF16), hm, preferred_element_type=F32)
        qn = qj * lax.rsqrt(ms + RMS_EPS) * qg_ref[...]
        qr = (_rope(qn, cos, sin) * Q_SCALE).astype(BF16)
        for r in range(heads_per_tile):
            q_ref[j * heads_per_tile + r] = qr[:, r * HEAD_DIM:(r + 1) * HEAD_DIM]
    kj = z[:, i1:i2]
    ms = jnp.dot((kj * kj).astype(BF16), hm, preferred_element_type=F32)
    kn = kj * lax.rsqrt(ms + RMS_EPS) * kg_ref[...]
    kr = _rope(kn, cos, sin).astype(BF16)
    vj = z[:, i2:i3].astype(BF16)
    ones = jnp.ones((vj.shape[0], HEAD_DIM), BF16)
    for g in range(ATT_KV_HEADS):
        k_ref[g] = kr[:, g * HEAD_DIM:(g + 1) * HEAD_DIM]
    v_ref[...] = jnp.concatenate([vj[:, :HEAD_DIM], ones, vj[:, HEAD_DIM:], ones], axis=1)
    u_ref[...] = (z[:, i3:i4] * _sigmoid(z[:, i4:])).astype(BF16)


def _proj_in(x, ln_g, ln_b, w_in, qg, kg, cos, sin, hm, seq):
    t = x.shape[0]
    tm = TM_IN
    nseq = seq // tm
    row = lambda i: (i, 0)
    const = lambda i: (0, 0)
    return pl.pallas_call(
        _proj_in_kernel,
        grid=(t // tm,),
        in_specs=[
            pl.BlockSpec((tm, D_MODEL), row),
            pl.BlockSpec((1, D_MODEL), const),
            pl.BlockSpec((1, D_MODEL), const),
            pl.BlockSpec((D_MODEL, IN_WIDTH), const),
            pl.BlockSpec((1, LANES), const),
            pl.BlockSpec((1, LANES), const),
            pl.BlockSpec((tm, LANES), lambda i: (i % nseq, 0)),
            pl.BlockSpec((tm, LANES), lambda i: (i % nseq, 0)),
            pl.BlockSpec((LANES, LANES), const),
        ],
        out_specs=[
            pl.BlockSpec((tm, D_MODEL), row),
            pl.BlockSpec((ATT_HEADS, tm, HEAD_DIM), lambda i: (0, i, 0)),
            pl.BlockSpec((ATT_KV_HEADS, tm, HEAD_DIM), lambda i: (0, i, 0)),
            pl.BlockSpec((tm, 2 * KV_WIDTH), row),
            pl.BlockSpec((tm, CONV_WIDTH), row),
        ],
        out_shape=[
            jax.ShapeDtypeStruct((t, D_MODEL), F32),
            jax.ShapeDtypeStruct((ATT_HEADS, t, HEAD_DIM), BF16),
            jax.ShapeDtypeStruct((ATT_KV_HEADS, t, HEAD_DIM), BF16),
            jax.ShapeDtypeStruct((t, 2 * KV_WIDTH), BF16),
            jax.ShapeDtypeStruct((t, CONV_WIDTH), BF16),
        ],
        compiler_params=pltpu.CompilerParams(dimension_semantics=("parallel",), vmem_limit_bytes=VMEM_LIMIT),
        name="proj_in",
    )(x, ln_g, ln_b, w_in, qg, kg, cos, sin, hm)


def _conv_kernel(u_ref, w_ref, cb_ref, g_ref, b_ref, o_ref, win_ref, sh_ref, *, seq):
    i = pl.program_id(1)
    tc = TC_CONV
    t0 = pl.multiple_of(i * tc, tc)
    top_start = pl.multiple_of(jnp.maximum(t0 - HALO, 0), HALO)
    bot_start = pl.multiple_of(jnp.minimum(t0 + tc, seq - HALO), HALO)
    top = u_ref[pl.ds(top_start, HALO), :].astype(F32)
    bot = u_ref[pl.ds(bot_start, HALO), :].astype(F32)
    win_ref[0:HALO, :] = jnp.where(i > 0, top, 0.0)
    win_ref[HALO:HALO + tc, :] = u_ref[pl.ds(t0, tc), :].astype(F32)
    win_ref[HALO + tc:, :] = jnp.where(i < pl.num_programs(1) - 1, bot, 0.0)
    sh_rows = sh_ref.shape[1]
    for s in range(1, SUBLANES):
        sh_ref[s] = win_ref[s:s + sh_rows, :]
    w = w_ref[...]
    cols = []
    for c in range(CONV_WIDTH // LANES):
        cs = slice(c * LANES, (c + 1) * LANES)
        acc = jnp.zeros((tc, LANES), F32)
        for j in range(CONV_KSIZE):
            off = HALO - CONV_PAD + j
            base, s = off - off % SUBLANES, off % SUBLANES
            tap = win_ref[base:base + tc, cs] if s == 0 else sh_ref[s, base:base + tc, cs]
            acc = acc + tap * w[j:j + 1, cs]
        cols.append(acc)
    y = jnp.concatenate(cols, axis=1) + cb_ref[...]
    y = _layer_norm(y, g_ref[...], b_ref[...])
    o_ref[...] = (y * _sigmoid(y)).astype(BF16)


def _conv_branch(u, conv_w, conv_b, cln_g, cln_b):
    b, seq, _ = u.shape
    tc = TC_CONV
    const = lambda bi, i: (0, 0)
    return pl.pallas_call(
        functools.partial(_conv_kernel, seq=seq),
        grid=(b, seq // tc),
        in_specs=[
            pl.BlockSpec((None, seq, CONV_WIDTH), lambda bi, i: (bi, 0, 0)),
            pl.BlockSpec((CONV_KSIZE, CONV_WIDTH), const),
            pl.BlockSpec((1, CONV_WIDTH), const),
            pl.BlockSpec((1, CONV_WIDTH), const),
            pl.BlockSpec((1, CONV_WIDTH), const),
        ],
        out_specs=pl.BlockSpec((None, tc, CONV_WIDTH), lambda bi, i: (bi, i, 0)),
        out_shape=jax.ShapeDtypeStruct((b, seq, CONV_WIDTH), BF16),
        scratch_shapes=[pltpu.VMEM((tc + 2 * HALO, CONV_WIDTH), F32),
                        pltpu.VMEM((SUBLANES, tc + 2 * HALO - SUBLANES, CONV_WIDTH), F32)],
        compiler_params=pltpu.CompilerParams(dimension_semantics=("parallel", "arbitrary"),
                                             vmem_limit_bytes=VMEM_LIMIT),
        name="conv_branch",
    )(u, conv_w, conv_b, cln_g, cln_b)


def _gqa_kernel(q_ref, k_ref, v_ref, o_ref, kmax_ref):
    rep = ATT_HEADS // ATT_KV_HEADS
    tq = q_ref.shape[1]
    seq = k_ref.shape[1]
    n_chunks = seq // KV_CHUNK
    nt_dims = (((1,), (1,)), ((), ()))

    @pl.when(pl.program_id(1) == 0)
    def _():
        for g in range(ATT_KV_HEADS):
            kf = k_ref[g].astype(F32)
            kmax_ref[g] = jnp.sqrt(jnp.max(jnp.sum(kf * kf, -1, keepdims=True), axis=0, keepdims=True))

    qs = [q_ref[g * rep:(g + 1) * rep].reshape(rep * tq, HEAD_DIM) for g in range(ATT_KV_HEADS)]
    bound = []
    for g in range(ATT_KV_HEADS):
        qf = qs[g].astype(F32)
        bound.append(jnp.sqrt(jnp.sum(qf * qf, -1, keepdims=True)) * kmax_ref[g])
    worst = jnp.maximum(jnp.max(bound[0]), jnp.max(bound[1]))

    def finish(acc):
        outs = []
        for g in range(ATT_KV_HEADS):
            o = acc[g] / pltpu.roll(acc[g], HEAD_DIM, axis=1)
            outs.extend(o[r * tq:(r + 1) * tq, :HEAD_DIM] for r in range(rep))
        o_ref[...] = jnp.concatenate(outs, axis=1).astype(BF16)

    @pl.when(worst <= ATT_SAFE_SPAN)
    def _():
        acc = [None] * ATT_KV_HEADS
        for c in range(n_chunks):
            rows = slice(c * KV_CHUNK, (c + 1) * KV_CHUNK)
            for g in range(ATT_KV_HEADS):
                s = lax.dot_general(qs[g], k_ref[g, rows], nt_dims, preferred_element_type=F32)
                p = jnp.exp2(s - bound[g]).astype(BF16)
                d = jnp.dot(p, v_ref[rows, g * LANES:(g + 1) * LANES], preferred_element_type=F32)
                acc[g] = d if c == 0 else acc[g] + d
        finish(acc)

    @pl.when(worst > ATT_SAFE_SPAN)
    def _():
        def step(c, carry):
            rows = pl.ds(pl.multiple_of(c * KV_CHUNK, KV_CHUNK), KV_CHUNK)
            new = []
            for g in range(ATT_KV_HEADS):
                m, acc = carry[g]
                s = lax.dot_general(qs[g], k_ref[g, rows], nt_dims, preferred_element_type=F32)
                m_new = jnp.maximum(m, jnp.max(s, -1, keepdims=True))
                p = jnp.exp2(s - m_new).astype(BF16)
                d = jnp.dot(p, v_ref[rows, g * LANES:(g + 1) * LANES], preferred_element_type=F32)
                new.append((m_new, jnp.exp2(m - m_new) * acc + d))
            return tuple(new)

        init = tuple((jnp.full((rep * tq, 1), -jnp.inf, F32), jnp.zeros((rep * tq, LANES), F32))
                     for _ in range(ATT_KV_HEADS))
        out = lax.fori_loop(0, n_chunks, step, init)
        finish([out[g][1] for g in range(ATT_KV_HEADS)])


def _gqa(q, k, v):
    _, b, seq, _ = q.shape
    tq = TQ_ATT
    return pl.pallas_call(
        _gqa_kernel,
        grid=(b, seq // tq),
        in_specs=[
            pl.BlockSpec((ATT_HEADS, None, tq, HEAD_DIM), lambda bi, i: (0, bi, i, 0)),
            pl.BlockSpec((ATT_KV_HEADS, None, seq, HEAD_DIM), lambda bi, i: (0, bi, 0, 0)),
            pl.BlockSpec((None, seq, ATT_KV_HEADS * LANES), lambda bi, i: (bi, 0, 0)),
        ],
        out_specs=pl.BlockSpec((None, tq, ATT_WIDTH), lambda bi, i: (bi, i, 0)),
        out_shape=jax.ShapeDtypeStruct((b, seq, ATT_WIDTH), BF16),
        scratch_shapes=[pltpu.VMEM((ATT_KV_HEADS, 1, 1), F32)],
        compiler_params=pltpu.CompilerParams(dimension_semantics=("arbitrary", "arbitrary"),
                                             vmem_limit_bytes=VMEM_LIMIT),
        name="gqa",
    )(q, k, v)


def _mem_kv_kernel(m_ref, g_ref, b_ref, wk_ref, wv_ref, k_ref, v_ref):
    m = _layer_norm(m_ref[...], g_ref[...], b_ref[...]).astype(BF16)
    k_ref[...] = jnp.dot(m, wk_ref[...], preferred_element_type=F32).astype(BF16)
    v_ref[...] = jnp.dot(m, wv_ref[...], preferred_element_type=F32).astype(BF16)


def _mem_kv(mem, ln_g, ln_b, w_ck, w_cv):
    b = mem.shape[0]
    const = lambda bi: (0, 0)
    blk = pl.BlockSpec((None, N_MEM, D_MODEL), lambda bi: (bi, 0, 0))
    return pl.pallas_call(
        _mem_kv_kernel,
        grid=(b,),
        in_specs=[blk, pl.BlockSpec((1, D_MODEL), const), pl.BlockSpec((1, D_MODEL), const),
                  pl.BlockSpec((D_MODEL, D_MODEL), const), pl.BlockSpec((D_MODEL, D_MODEL), const)],
        out_specs=[blk, blk],
        out_shape=[jax.ShapeDtypeStruct((b, N_MEM, D_MODEL), BF16)] * 2,
        compiler_params=pltpu.CompilerParams(dimension_semantics=("parallel",), vmem_limit_bytes=VMEM_LIMIT),
        name="mem_kv",
    )(mem, ln_g, ln_b, w_ck, w_cv)


def _post_kernel(att_ref, c_ref, h0_ref, km_ref, vm_ref, wo_ref, g1_ref, b1_ref, wq_ref, wco_ref,
                 g2_ref, b2_ref, wr_hi_ref, wr_lo_ref, br_ref, tri_ref, h2_ref, t_ref, route_ref, cnt_out_ref,
                 cnt_ref):
    mixed = jnp.concatenate([att_ref[...], c_ref[...]], axis=1)
    mix = jnp.dot(mixed, wo_ref[...], preferred_element_type=F32)
    h1 = _layer_norm(DEEPNORM_ALPHA * h0_ref[...] + mix, g1_ref[...], b1_ref[...])
    qx = (jnp.dot(h1.astype(BF16), wq_ref[...], preferred_element_type=F32) * (MEM_HEAD_DIM ** -0.5)).astype(BF16)
    heads = []
    for hd in range(MEM_HEADS):
        hs = slice(hd * MEM_HEAD_DIM, (hd + 1) * MEM_HEAD_DIM)
        s = lax.dot_general(qx[:, hs], km_ref[:, hs], (((1,), (1,)), ((), ())), preferred_element_type=F32)
        p = jnp.exp(s - jnp.max(s, -1, keepdims=True))
        l = jnp.sum(p, -1, keepdims=True)
        heads.append(jnp.dot(p.astype(BF16), vm_ref[:, hs], preferred_element_type=F32) / l)
    o = jnp.concatenate(heads, axis=1).astype(BF16)
    xa = jnp.dot(o, wco_ref[...], preferred_element_type=F32)
    h2 = _layer_norm(DEEPNORM_ALPHA * h1 + xa, g2_ref[...], b2_ref[...])
    h2_ref[...] = h2
    t_hi = h2.astype(BF16)
    t_rounded = t_hi.astype(F32)
    tw = _pack_bf16_pairs(t_rounded)
    for half in range(2):
        t_ref[half] = tw[:, half * ROW_WORDS:(half + 1) * ROW_WORDS]
    t_lo = (h2 - t_rounded).astype(BF16)
    logits = (jnp.dot(t_hi, wr_hi_ref[...], preferred_element_type=F32)
              + jnp.dot(t_lo, wr_hi_ref[...], preferred_element_type=F32)
              + jnp.dot(t_hi, wr_lo_ref[...], preferred_element_type=F32)) + br_ref[...]
    lane = lax.broadcasted_iota(jnp.int32, logits.shape, 1)
    neg = -jnp.inf
    big = jnp.int32(LANES)
    gl = jnp.where(lane < N_GROUPS, logits, neg)
    gmax = jnp.max(gl, -1, keepdims=True)
    p_grp = 1.0 / jnp.sum(jnp.exp(gl - gmax), -1, keepdims=True)
    grp = jnp.min(jnp.where(gl == gmax, lane, big), -1, keepdims=True)
    lo = N_GROUPS + EXPERTS_PER_GROUP * grp
    el = jnp.where((lane >= lo) & (lane < lo + EXPERTS_PER_GROUP), logits, neg)
    v1 = jnp.max(el, -1, keepdims=True)
    i1 = jnp.min(jnp.where(el == v1, lane, big), -1, keepdims=True)
    el2 = jnp.where(lane == i1, neg, el)
    v2 = jnp.max(el2, -1, keepdims=True)
    i2 = jnp.min(jnp.where(el2 == v2, lane, big), -1, keepdims=True)
    e21 = jnp.exp(v2 - v1)
    w1 = p_grp / (1.0 + e21)
    w2 = p_grp * e21 / (1.0 + e21)
    @pl.when((pl.program_id(0) == 0) & (pl.program_id(1) == 0))
    def _():
        cnt_ref[...] = jnp.zeros_like(cnt_ref)

    e1 = i1 - N_GROUPS
    e2 = i2 - N_GROUPS
    hot1 = jnp.where(lane == e1, 1.0, 0.0)
    hot2 = jnp.where(lane == e2, 1.0, 0.0)
    both = hot1 + hot2
    before = jnp.dot(tri_ref[...], both.astype(BF16), preferred_element_type=F32) + cnt_ref[...]
    rank1 = jnp.sum(hot1 * before, -1, keepdims=True)
    rank2 = jnp.sum(hot2 * before, -1, keepdims=True)
    cnt_ref[...] += jnp.sum(both, 0, keepdims=True)
    cnt_out_ref[...] = cnt_ref[...]
    route = jnp.zeros(logits.shape, F32)
    for k, val in enumerate((e1.astype(F32), e2.astype(F32), w1, w2, rank1, rank2)):
        route = jnp.where(lane == k, val, route)
    route_ref[...] = route


def _post(att, c, h0, km, vm, w_out, g1, b1, w_cq, w_co, g2, b2, wr_hi, wr_lo, br):
    b, seq, _ = att.shape
    tm = TM_POST
    tok = lambda width: pl.BlockSpec((None, tm, width), lambda bi, i: (bi, i, 0))
    memb = pl.BlockSpec((None, N_MEM, D_MODEL), lambda bi, i: (bi, 0, 0))
    const = lambda shape: pl.BlockSpec(shape, lambda bi, i: (0, 0))
    vec = const((1, D_MODEL))
    sq = const((D_MODEL, D_MODEL))
    idx = jnp.arange(tm, dtype=jnp.int32)
    tri = jnp.where(idx[:, None] > idx[None, :], 1.0, 0.0).astype(BF16)
    return pl.pallas_call(
        _post_kernel,
        grid=(b, seq // tm),
        in_specs=[tok(ATT_WIDTH), tok(CONV_WIDTH), tok(D_MODEL), memb, memb, sq, vec, vec, sq, sq, vec, vec,
                  const((D_MODEL, LANES)), const((D_MODEL, LANES)), const((1, LANES)), const((tm, tm))],
        out_specs=[tok(D_MODEL),
                   pl.BlockSpec((2, None, tm, ROW_WORDS), lambda bi, i: (0, bi, i, 0)),
                   tok(LANES), const((1, LANES))],
        out_shape=[jax.ShapeDtypeStruct((b, seq, D_MODEL), F32),
                   jax.ShapeDtypeStruct((2, b, seq, ROW_WORDS), jnp.uint32),
                   jax.ShapeDtypeStruct((b, seq, LANES), F32),
                   jax.ShapeDtypeStruct((1, LANES), F32)],
        scratch_shapes=[pltpu.VMEM((1, LANES), F32)],
        compiler_params=pltpu.CompilerParams(dimension_semantics=("arbitrary", "arbitrary"),
                                             vmem_limit_bytes=VMEM_LIMIT),
        name="post",
    )(att, c, h0, km, vm, w_out, g1, b1, w_cq, w_co, g2, b2, wr_hi, wr_lo, br, tri)


def _pack_bf16_pairs(x):
    k = x.shape[1] // 2
    bits = lax.bitcast_convert_type(x, jnp.uint32)
    return (bits[:, :k] >> 16) | (bits[:, k:] & jnp.uint32(0xFFFF0000))


def _unpack_bf16_pairs(w):
    lo = lax.bitcast_convert_type(w << 16, F32)
    hi = lax.bitcast_convert_type(w & jnp.uint32(0xFFFF0000), F32)
    return jnp.concatenate([lo, hi], axis=1)


def _sc_mesh():
    return plsc.VectorSubcoreMesh(core_axis_name="c", subcore_axis_name="s")


def _sc_scatter2(x, idx_a, idx_b, n_out):
    m = x.shape[0]

    @pl.kernel(out_type=jax.ShapeDtypeStruct((n_out, ROW_WORDS), x.dtype), mesh=_sc_mesh(), scratch_types=[])
    def scatter(x_hbm, ia_hbm, ib_hbm, o_hbm):
        def body(x_vmem, ia_vmem, ib_vmem):
            pltpu.sync_copy(x_vmem, o_hbm.at[ia_vmem.at[0]])
            pltpu.sync_copy(x_vmem, o_hbm.at[ib_vmem.at[0]])

        pltpu.emit_pipeline(
            body, grid=(m // SC_WINDOW,),
            in_specs=[pl.BlockSpec((SC_WINDOW, ROW_WORDS), lambda i: (i, 0)),
                      pl.BlockSpec((1, SC_WINDOW), lambda i: (0, i)),
                      pl.BlockSpec((1, SC_WINDOW), lambda i: (0, i))],
            out_specs=[],
            core_axis_name=("c", "s"), dimension_semantics=(pltpu.PARALLEL,),
        )(x_hbm, ia_hbm, ib_hbm)

    return scatter(x, idx_a.reshape(1, m), idx_b.reshape(1, m))


def _sc_gather(x, idx):
    m = idx.shape[0]

    @pl.kernel(out_type=jax.ShapeDtypeStruct((m, ROW_WORDS), x.dtype), mesh=_sc_mesh(), scratch_types=[])
    def gather(x_hbm, i_hbm, o_hbm):
        def body(i_vmem, o_vmem):
            pltpu.sync_copy(x_hbm.at[i_vmem.at[0]], o_vmem)

        pltpu.emit_pipeline(
            body, grid=(m // SC_WINDOW,),
            in_specs=[pl.BlockSpec((1, SC_WINDOW), lambda i: (0, i))],
            out_specs=[pl.BlockSpec((SC_WINDOW, ROW_WORDS), lambda i: (i, 0))],
            core_axis_name=("c", "s"), dimension_semantics=(pltpu.PARALLEL,),
        )(i_hbm, o_hbm)

    return gather(x, idx.reshape(1, m))


def _experts_kernel(tile_expert_ref, n_used_ref, x_ref, wg_ref, wu_ref, wd_ref, y_ref):
    @pl.when(pl.program_id(0) < n_used_ref[0])
    def _():
        u0 = _unpack_bf16_pairs(x_ref[0])
        u1 = _unpack_bf16_pairs(x_ref[1])
        x = jnp.concatenate([u0[:, :ROW_WORDS], u1[:, :ROW_WORDS], u0[:, ROW_WORDS:], u1[:, ROW_WORDS:]],
                            axis=1).astype(BF16)
        gate = jnp.dot(x, wg_ref[...], preferred_element_type=F32)
        up = jnp.dot(x, wu_ref[...], preferred_element_type=F32)
        he = (gate * _sigmoid(gate) * up).astype(BF16)
        y = jnp.dot(he, wd_ref[...], preferred_element_type=F32)
        yw = _pack_bf16_pairs(y.astype(BF16).astype(F32))
        for half in range(2):
            y_ref[half] = yw[:, half * ROW_WORDS:(half + 1) * ROW_WORDS]


def _experts(xs, tile_expert, n_used, w_gate, w_up, w_down):
    _, rows, _ = xs.shape
    tr = TR_EXPERT
    halves = pl.BlockSpec((2, tr, ROW_WORDS), lambda j, te, nu: (0, j, 0))
    return pl.pallas_call(
        _experts_kernel,
        grid_spec=pltpu.PrefetchScalarGridSpec(
            num_scalar_prefetch=2,
            grid=(rows // tr,),
            in_specs=[halves,
                      pl.BlockSpec((None, D_MODEL, D_EXPERT), lambda j, te, nu: (te[j], 0, 0)),
                      pl.BlockSpec((None, D_MODEL, D_EXPERT), lambda j, te, nu: (te[j], 0, 0)),
                      pl.BlockSpec((None, D_EXPERT, D_MODEL), lambda j, te, nu: (te[j], 0, 0))],
            out_specs=halves,
        ),
        out_shape=jax.ShapeDtypeStruct(xs.shape, jnp.uint32),
        compiler_params=pltpu.CompilerParams(dimension_semantics=("arbitrary",), vmem_limit_bytes=VMEM_LIMIT),
        name="experts",
    )(tile_expert, n_used, xs, w_gate, w_up, w_down)


def _combine_kernel(h2_ref, y1_ref, y2_ref, route_ref, g3_ref, b3_ref, o_ref):
    route = route_ref[...]
    w1 = route[:, ROUTE_W:ROUTE_W + 1]
    w2 = route[:, ROUTE_W + 1:ROUTE_W + 2]
    halves = []
    for half in range(2):
        halves.append(w1 * _unpack_bf16_pairs(y1_ref[half]) + w2 * _unpack_bf16_pairs(y2_ref[half]))
    ff = jnp.concatenate([halves[0][:, :ROW_WORDS], halves[1][:, :ROW_WORDS],
                          halves[0][:, ROW_WORDS:], halves[1][:, ROW_WORDS:]], axis=1)
    o_ref[...] = _layer_norm(DEEPNORM_ALPHA * h2_ref[...] + ff, g3_ref[...], b3_ref[...])


def _combine(h2, y1, y2, route, g3, b3):
    n = h2.shape[0]
    tm = TM_COMBINE
    row = lambda width: pl.BlockSpec((tm, width), lambda i: (i, 0))
    halves = pl.BlockSpec((2, tm, ROW_WORDS), lambda i: (0, i, 0))
    vec = pl.BlockSpec((1, D_MODEL), lambda i: (0, 0))
    return pl.pallas_call(
        _combine_kernel,
        grid=(n // tm,),
        in_specs=[row(D_MODEL), halves, halves, row(LANES), vec, vec],
        out_specs=row(D_MODEL),
        out_shape=jax.ShapeDtypeStruct((n, D_MODEL), F32),
        compiler_params=pltpu.CompilerParams(dimension_semantics=("parallel",), vmem_limit_bytes=VMEM_LIMIT),
        name="combine",
    )(h2, y1, y2, route, g3, b3)


def _moe(tw, route, counts, h2, w_gate, w_up, w_down, g3, b3):
    n = h2.shape[0]
    tr = TR_EXPERT
    cap = 2 * n + N_EXPERTS * tr
    cnt = counts[0, :N_EXPERTS].astype(jnp.int32)
    seg = (cnt + tr - 1) // tr * tr
    ends = jnp.cumsum(seg)
    starts = ends - seg
    e1 = route[:, ROUTE_E].astype(jnp.int32)
    e2 = route[:, ROUTE_E + 1].astype(jnp.int32)
    pos1 = starts[e1] + route[:, ROUTE_RANK].astype(jnp.int32)
    pos2 = starts[e2] + route[:, ROUTE_RANK + 1].astype(jnp.int32)
    tile_start = jnp.arange(cap // tr, dtype=jnp.int32) * tr
    tile_expert = jnp.minimum(jnp.sum(tile_start[:, None] >= ends[None, :], axis=1, dtype=jnp.int32), N_EXPERTS - 1)
    n_used = (ends[-1] // tr).reshape(1)
    idx1 = jnp.concatenate([pos1, cap + pos1])
    idx2 = jnp.concatenate([pos2, cap + pos2])
    xs = _sc_scatter2(tw.reshape(2 * n, ROW_WORDS), idx1, idx2, 2 * cap)
    ys = _experts(xs.reshape(2, cap, ROW_WORDS), tile_expert, n_used, w_gate, w_up, w_down)
    ys = ys.reshape(2 * cap, ROW_WORDS)
    y1 = _sc_gather(ys, idx1).reshape(2, n, ROW_WORDS)
    y2 = _sc_gather(ys, idx2).reshape(2, n, ROW_WORDS)
    return _combine(h2, y1, y2, route, g3, b3)


def _rope_tables(seq):
    rows = seq // GRID_W
    row = jnp.repeat(jnp.arange(rows, dtype=F32), GRID_W)
    col = jnp.tile(jnp.arange(GRID_W, dtype=F32), rows)
    inv_freq = ROPE_THETA ** (-jnp.arange(0, ROPE_AXIS_DIM, 2, dtype=F32) / ROPE_AXIS_DIM)
    ang = jnp.concatenate([row[:, None] * inv_freq, col[:, None] * inv_freq], -1)
    cos = jnp.repeat(jnp.cos(ang), 2, axis=-1)
    sin = jnp.repeat(jnp.sin(ang), 2, axis=-1)
    sign = jnp.tile(jnp.array([-1.0, 1.0], F32), HEAD_DIM // 2)
    sin = sin * sign
    return jnp.tile(cos, (1, LANES // HEAD_DIM)), jnp.tile(sin, (1, LANES // HEAD_DIM))


def _encoder(x, mem, p):
    b, seq, d = x.shape
    cos, sin = _rope_tables(seq)
    h0, q, k, v, u = _proj_in(x.reshape(b * seq, d), p["ln_in_g"], p["ln_in_b"], p["w_in"], p["qg"], p["kg"],
                              cos, sin, p["head_mean"], seq)
    c = _conv_branch(u.reshape(b, seq, CONV_WIDTH), p["conv_w"], p["conv_b"], p["cln_g"], p["cln_b"])
    att = _gqa(q.reshape(ATT_HEADS, b, seq, HEAD_DIM), k.reshape(ATT_KV_HEADS, b, seq, HEAD_DIM),
               v.reshape(b, seq, 2 * KV_WIDTH))
    km, vm = _mem_kv(mem, p["ln_mem_g"], p["ln_mem_b"], p["w_ck"], p["w_cv"])
    h2, tw, route, counts = _post(att, c, h0.reshape(b, seq, d), km, vm, p["w_out"], p["ln1_g"], p["ln1_b"],
                                  p["w_cq"], p["w_co"], p["ln2_g"], p["ln2_b"], p["wr_hi"], p["wr_lo"], p["br"])
    y = _moe(tw.reshape(2, b * seq, ROW_WORDS), route.reshape(b * seq, LANES), counts, h2.reshape(b * seq, d),
             p["w_gate"], p["w_up"], p["w_down"], p["ln3_g"], p["ln3_b"])
    return y.reshape(b, seq, d)


def kernel(x_prompt, x_sample, mem_prompt, mem_sample, ln_in_g, ln_in_b, ln_mem_g, ln_mem_b, w_in, q_norm_g, k_norm_g, conv_w, conv_b, conv_ln_g, conv_ln_b, w_mix_out, ln1_g, ln1_b, w_cq, w_ck, w_cv, w_co, ln2_g, ln2_b, w_router_g, b_router_g, w_router_e, b_router_e, w_e_gate, w_e_up, w_e_down, ln3_g, ln3_b):
    l = 0
    vec = lambda a: a.reshape(1, -1).astype(F32)
    w_r = jnp.concatenate([w_router_g[l], jnp.transpose(w_router_e[l], (1, 0, 2)).reshape(D_MODEL, N_EXPERTS)], axis=1)
    w_r = jnp.pad(w_r, ((0, 0), (0, LANES - w_r.shape[1])))
    wr_hi = w_r.astype(BF16)
    wr_lo = (w_r - wr_hi.astype(F32)).astype(BF16)
    b_r = jnp.concatenate([b_router_g[l], b_router_e[l].reshape(-1)])
    b_r = jnp.pad(b_r, (0, LANES - b_r.shape[0])).reshape(1, LANES)
    head = jnp.arange(LANES) // HEAD_DIM
    head_mean = jnp.where(head[:, None] == head[None, :], 1.0 / HEAD_DIM, 0.0).astype(BF16)
    p = {
        "ln_in_g": vec(ln_in_g), "ln_in_b": vec(ln_in_b), "ln_mem_g": vec(ln_mem_g), "ln_mem_b": vec(ln_mem_b),
        "w_in": w_in[l].astype(BF16),
        "qg": jnp.tile(q_norm_g[l], LANES // HEAD_DIM).reshape(1, LANES),
        "kg": jnp.tile(k_norm_g[l], LANES // HEAD_DIM).reshape(1, LANES),
        "head_mean": head_mean,
        "conv_w": conv_w[l], "conv_b": vec(conv_b[l]), "cln_g": vec(conv_ln_g[l]), "cln_b": vec(conv_ln_b[l]),
        "w_out": w_mix_out[l].astype(BF16), "ln1_g": vec(ln1_g[l]), "ln1_b": vec(ln1_b[l]),
        "w_cq": w_cq[l].astype(BF16), "w_ck": w_ck[l].astype(BF16), "w_cv": w_cv[l].astype(BF16),
        "w_co": w_co[l].astype(BF16), "ln2_g": vec(ln2_g[l]), "ln2_b": vec(ln2_b[l]),
        "wr_hi": wr_hi, "wr_lo": wr_lo, "br": b_r,
        "w_gate": w_e_gate[l].astype(BF16), "w_up": w_e_up[l].astype(BF16), "w_down": w_e_down[l].astype(BF16),
        "ln3_g": vec(ln3_g[l]), "ln3_b": vec(ln3_b[l]),
    }
    return (_encoder(x_prompt, mem_prompt, p), _encoder(x_sample, mem_sample, p))
```

```python
import functools

import jax
import jax.numpy as jnp
from jax import lax
from jax.experimental import pallas as pl
from jax.experimental.pallas import tpu as pltpu
from jax.experimental.pallas import tpu_sc as plsc

F32 = jnp.float32
BF16 = jnp.bfloat16

D_MODEL = 1024
DEPTH = 1
GRID_W = 64
N_MEM = 256
ATT_HEADS = 8
ATT_KV_HEADS = 2
HEAD_DIM = 64
ATT_WIDTH = ATT_HEADS * HEAD_DIM
KV_WIDTH = ATT_KV_HEADS * HEAD_DIM
ROPE_THETA = 10000.0
ROPE_AXIS_DIM = HEAD_DIM // 2
CONV_WIDTH = D_MODEL - ATT_WIDTH
CONV_KSIZE = 31
CONV_PAD = CONV_KSIZE // 2
IN_WIDTH = ATT_WIDTH + 2 * KV_WIDTH + 2 * CONV_WIDTH
MEM_HEADS = 4
MEM_HEAD_DIM = D_MODEL // MEM_HEADS
N_GROUPS = 4
EXPERTS_PER_GROUP = 8
N_EXPERTS = N_GROUPS * EXPERTS_PER_GROUP
D_EXPERT = 256
LN_EPS = 1e-5
RMS_EPS = 1e-6
DEEPNORM_ALPHA = (2.0 * DEPTH) ** 0.25
LOG2_E = 1.4426950408889634
Q_SCALE = HEAD_DIM ** -0.5 * LOG2_E

LANES = 128
SUBLANES = 8
HALO = 16
VMEM_LIMIT = 56 * 1024 * 1024

TM_IN = 512
TC_CONV = 128
TQ_ATT = 256
KV_CHUNK = 512
ATT_SAFE_SPAN = 50.0
TM_POST = 512
TR_EXPERT = 512
TM_COMBINE = 512
ROW_WORDS = 256
SC_WINDOW = 128
ROUTE_E, ROUTE_W, ROUTE_RANK = 0, 2, 4


def _layer_norm(x, g, b):
    mu = jnp.mean(x, -1, keepdims=True)
    xc = x - mu
    var = jnp.mean(xc * xc, -1, keepdims=True)
    return xc * lax.rsqrt(var + LN_EPS) * g + b


def _sigmoid(x):
    return 1.0 / (1.0 + jnp.exp(-x))


def _rope(x, cos, sin_signed):
    lane = lax.broadcasted_iota(jnp.int32, x.shape, 1)
    nxt = pltpu.roll(x, LANES - 1, axis=1)
    prv = pltpu.roll(x, 1, axis=1)
    return x * cos + jnp.where((lane & 1) == 0, nxt, prv) * sin_signed


def _proj_in_kernel(x_ref, g_ref, b_ref, w_ref, qg_ref, kg_ref, cos_ref, sin_ref, hm_ref,
                    h_ref, q_ref, k_ref, v_ref, u_ref):
    h = _layer_norm(x_ref[...], g_ref[...], b_ref[...])
    h_ref[...] = h
    z = jnp.dot(h.astype(BF16), w_ref[...], preferred_element_type=F32)
    i1 = ATT_WIDTH
    i2 = i1 + KV_WIDTH
    i3 = i2 + KV_WIDTH
    i4 = i3 + CONV_WIDTH
    cos = cos_ref[...]
    sin = sin_ref[...]
    hm = hm_ref[...]
    heads_per_tile = LANES // HEAD_DIM
    for j in range(ATT_WIDTH // LANES):
        qj = z[:, j * LANES:(j + 1) * LANES]
        ms = jnp.dot((qj * qj).astype(BF16), hm, preferred_element_type=F32)
        qn = qj * lax.rsqrt(ms + RMS_EPS) * qg_ref[...]
        qr = (_rope(qn, cos, sin) * Q_SCALE).astype(BF16)
        for r in range(heads_per_tile):
            q_ref[j * heads_per_tile + r] = qr[:, r * HEAD_DIM:(r + 1) * HEAD_DIM]
    kj = z[:, i1:i2]
    ms = jnp.dot((kj * kj).astype(BF16), hm, preferred_element_type=F32)
    kn = kj * lax.rsqrt(ms + RMS_EPS) * kg_ref[...]
    kr = _rope(kn, cos, sin).astype(BF16)
    vj = z[:, i2:i3].astype(BF16)
    ones = jnp.ones((vj.shape[0], HEAD_DIM), BF16)
    for g in range(ATT_KV_HEADS):
        k_ref[g] = kr[:, g * HEAD_DIM:(g + 1) * HEAD_DIM]
    v_ref[...] = jnp.concatenate([vj[:, :HEAD_DIM], ones, vj[:, HEAD_DIM:], ones], axis=1)
    u_ref[...] = (z[:, i3:i4] * _sigmoid(z[:, i4:])).astype(BF16)


def _proj_in(x, ln_g, ln_b, w_in, qg, kg, cos, sin, hm, seq):
    t = x.shape[0]
    tm = TM_IN
    nseq = seq // tm
    row = lambda i: (i, 0)
    const = lambda i: (0, 0)
    return pl.pallas_call(
        _proj_in_kernel,
        grid=(t // tm,),
        in_specs=[
            pl.BlockSpec((tm, D_MODEL), row),
            pl.BlockSpec((1, D_MODEL), const),
            pl.BlockSpec((1, D_MODEL), const),
            pl.BlockSpec((D_MODEL, IN_WIDTH), const),
            pl.BlockSpec((1, LANES), const),
            pl.BlockSpec((1, LANES), const),
            pl.BlockSpec((tm, LANES), lambda i: (i % nseq, 0)),
            pl.BlockSpec((tm, LANES), lambda i: (i % nseq, 0)),
            pl.BlockSpec((LANES, LANES), const),
        ],
        out_specs=[
            pl.BlockSpec((tm, D_MODEL), row),
            pl.BlockSpec((ATT_HEADS, tm, HEAD_DIM), lambda i: (0, i, 0)),
            pl.BlockSpec((ATT_KV_HEADS, tm, HEAD_DIM), lambda i: (0, i, 0)),
            pl.BlockSpec((tm, 2 * KV_WIDTH), row),
            pl.BlockSpec((tm, CONV_WIDTH), row),
        ],
        out_shape=[
            jax.ShapeDtypeStruct((t, D_MODEL), F32),
            jax.ShapeDtypeStruct((ATT_HEADS, t, HEAD_DIM), BF16),
            jax.ShapeDtypeStruct((ATT_KV_HEADS, t, HEAD_DIM), BF16),
            jax.ShapeDtypeStruct((t, 2 * KV_WIDTH), BF16),
            jax.ShapeDtypeStruct((t, CONV_WIDTH), BF16),
        ],
        compiler_params=pltpu.CompilerParams(dimension_semantics=("parallel",), vmem_limit_bytes=VMEM_LIMIT),
        name="proj_in",
    )(x, ln_g, ln_b, w_in, qg, kg, cos, sin, hm)


def _conv_kernel(u_ref, w_ref, cb_ref, g_ref, b_ref, o_ref, win_ref, sh_ref, *, seq):
    i = pl.program_id(1)
    tc = TC_CONV
    t0 = pl.multiple_of(i * tc, tc)
    top_start = pl.multiple_of(jnp.maximum(t0 - HALO, 0), HALO)
    bot_start = pl.multiple_of(jnp.minimum(t0 + tc, seq - HALO), HALO)
    top = u_ref[pl.ds(top_start, HALO), :].astype(F32)
    bot = u_ref[pl.ds(bot_start, HALO), :].astype(F32)
    win_ref[0:HALO, :] = jnp.where(i > 0, top, 0.0)
    win_ref[HALO:HALO + tc, :] = u_ref[pl.ds(t0, tc), :].astype(F32)
    win_ref[HALO + tc:, :] = jnp.where(i < pl.num_programs(1) - 1, bot, 0.0)
    sh_rows = sh_ref.shape[1]
    for s in range(1, SUBLANES):
        sh_ref[s] = win_ref[s:s + sh_rows, :]
    w = w_ref[...]
    cols = []
    for c in range(CONV_WIDTH // LANES):
        cs = slice(c * LANES, (c + 1) * LANES)
        acc = jnp.zeros((tc, LANES), F32)
        for j in range(CONV_KSIZE):
            off = HALO - CONV_PAD + j
            base, s = off - off % SUBLANES, off % SUBLANES
            tap = win_ref[base:base + tc, cs] if s == 0 else sh_ref[s, base:base + tc, cs]
            acc = acc + tap * w[j:j + 1, cs]
        cols.append(acc)
    y = jnp.concatenate(cols, axis=1) + cb_ref[...]
    y = _layer_norm(y, g_ref[...], b_ref[...])
    o_ref[...] = (y * _sigmoid(y)).astype(BF16)


def _conv_branch(u, conv_w, conv_b, cln_g, cln_b):
    b, seq, _ = u.shape
    tc = TC_CONV
    const = lambda bi, i: (0, 0)
    return pl.pallas_call(
        functools.partial(_conv_kernel, seq=seq),
        grid=(b, seq // tc),
        in_specs=[
            pl.BlockSpec((None, seq, CONV_WIDTH), lambda bi, i: (bi, 0, 0)),
            pl.BlockSpec((CONV_KSIZE, CONV_WIDTH), const),
            pl.BlockSpec((1, CONV_WIDTH), const),
            pl.BlockSpec((1, CONV_WIDTH), const),
            pl.BlockSpec((1, CONV_WIDTH), const),
        ],
        out_specs=pl.BlockSpec((None, tc, CONV_WIDTH), lambda bi, i: (bi, i, 0)),
        out_shape=jax.ShapeDtypeStruct((b, seq, CONV_WIDTH), BF16),
        scratch_shapes=[pltpu.VMEM((tc + 2 * HALO, CONV_WIDTH), F32),
                        pltpu.VMEM((SUBLANES, tc + 2 * HALO - SUBLANES, CONV_WIDTH), F32)],
        compiler_params=pltpu.CompilerParams(dimension_semantics=("parallel", "arbitrary"),
                                             vmem_limit_bytes=VMEM_LIMIT),
        name="conv_branch",
    )(u, conv_w, conv_b, cln_g, cln_b)


def _gqa_kernel(q_ref, k_ref, v_ref, o_ref, kmax_ref):
    rep = ATT_HEADS // ATT_KV_HEADS
    tq = q_ref.shape[1]
    seq = k_ref.shape[1]
    n_chunks = seq // KV_CHUNK
    nt_dims = (((1,), (1,)), ((), ()))

    @pl.when(pl.program_id(1) == 0)
    def _():
        for g in range(ATT_KV_HEADS):
            kf = k_ref[g].astype(F32)
            kmax_ref[g] = jnp.sqrt(jnp.max(jnp.sum(kf * kf, -1, keepdims=True), axis=0, keepdims=True))

    qs = [q_ref[g * rep:(g + 1) * rep].reshape(rep * tq, HEAD_DIM) for g in range(ATT_KV_HEADS)]
    bound = []
    for g in range(ATT_KV_HEADS):
        qf = qs[g].astype(F32)
        bound.append(jnp.sqrt(jnp.sum(qf * qf, -1, keepdims=True)) * kmax_ref[g])
    worst = jnp.maximum(jnp.max(bound[0]), jnp.max(bound[1]))

    def finish(acc):
        outs = []
        for g in range(ATT_KV_HEADS):
            o = acc[g] / pltpu.roll(acc[g], HEAD_DIM, axis=1)
            outs.extend(o[r * tq:(r + 1) * tq, :HEAD_DIM] for r in range(rep))
        o_ref[...] = jnp.concatenate(outs, axis=1).astype(BF16)

    @pl.when(worst <= ATT_SAFE_SPAN)
    def _():
        acc = [None] * ATT_KV_HEADS
        for c in range(n_chunks):
            rows = slice(c * KV_CHUNK, (c + 1) * KV_CHUNK)
            for g in range(ATT_KV_HEADS):
                s = lax.dot_general(qs[g], k_ref[g, rows], nt_dims, preferred_element_type=F32)
                p = jnp.exp2(s - bound[g]).astype(BF16)
                d = jnp.dot(p, v_ref[rows, g * LANES:(g + 1) * LANES], preferred_element_type=F32)
                acc[g] = d if c == 0 else acc[g] + d
        finish(acc)

    @pl.when(worst > ATT_SAFE_SPAN)
    def _():
        def step(c, carry):
            rows = pl.ds(pl.multiple_of(c * KV_CHUNK, KV_CHUNK), KV_CHUNK)
            new = []
            for g in range(ATT_KV_HEADS):
                m, acc = carry[g]
                s = lax.dot_general(qs[g], k_ref[g, rows], nt_dims, preferred_element_type=F32)
                m_new = jnp.maximum(m, jnp.max(s, -1, keepdims=True))
                p = jnp.exp2(s - m_new).astype(BF16)
                d = jnp.dot(p, v_ref[rows, g * LANES:(g + 1) * LANES], preferred_element_type=F32)
                new.append((m_new, jnp.exp2(m - m_new) * acc + d))
            return tuple(new)

        init = tuple((jnp.full((rep * tq, 1), -jnp.inf, F32), jnp.zeros((rep * tq, LANES), F32))
                     for _ in range(ATT_KV_HEADS))
        out = lax.fori_loop(0, n_chunks, step, init)
        finish([out[g][1] for g in range(ATT_KV_HEADS)])


def _gqa(q, k, v):
    _, b, seq, _ = q.shape
    tq = TQ_ATT
    return pl.pallas_call(
        _gqa_kernel,
        grid=(b, seq // tq),
        in_specs=[
            pl.BlockSpec((ATT_HEADS, None, tq, HEAD_DIM), lambda bi, i: (0, bi, i, 0)),
            pl.BlockSpec((ATT_KV_HEADS, None, seq, HEAD_DIM), lambda bi, i: (0, bi, 0, 0)),
            pl.BlockSpec((None, seq, ATT_KV_HEADS * LANES), lambda bi, i: (bi, 0, 0)),
        ],
        out_specs=pl.BlockSpec((None, tq, ATT_WIDTH), lambda bi, i: (bi, i, 0)),
        out_shape=jax.ShapeDtypeStruct((b, seq, ATT_WIDTH), BF16),
        scratch_shapes=[pltpu.VMEM((ATT_KV_HEADS, 1, 1), F32)],
        compiler_params=pltpu.CompilerParams(dimension_semantics=("arbitrary", "arbitrary"),
                                             vmem_limit_bytes=VMEM_LIMIT),
        name="gqa",
    )(q, k, v)


def _mem_kv_kernel(m_ref, g_ref, b_ref, wk_ref, wv_ref, k_ref, v_ref):
    m = _layer_norm(m_ref[...], g_ref[...], b_ref[...]).astype(BF16)
    k_ref[...] = jnp.dot(m, wk_ref[...], preferred_element_type=F32).astype(BF16)
    v_ref[...] = jnp.dot(m, wv_ref[...], preferred_element_type=F32).astype(BF16)


def _mem_kv(mem, ln_g, ln_b, w_ck, w_cv):
    b = mem.shape[0]
    const = lambda bi: (0, 0)
    blk = pl.BlockSpec((None, N_MEM, D_MODEL), lambda bi: (bi, 0, 0))
    return pl.pallas_call(
        _mem_kv_kernel,
        grid=(b,),
        in_specs=[blk, pl.BlockSpec((1, D_MODEL), const), pl.BlockSpec((1, D_MODEL), const),
                  pl.BlockSpec((D_MODEL, D_MODEL), const), pl.BlockSpec((D_MODEL, D_MODEL), const)],
        out_specs=[blk, blk],
        out_shape=[jax.ShapeDtypeStruct((b, N_MEM, D_MODEL), BF16)] * 2,
        compiler_params=pltpu.CompilerParams(dimension_semantics=("parallel",), vmem_limit_bytes=VMEM_LIMIT),
        name="mem_kv",
    )(mem, ln_g, ln_b, w_ck, w_cv)


def _post_kernel(att_ref, c_ref, h0_ref, km_ref, vm_ref, wo_ref, g1_ref, b1_ref, wq_ref, wco_ref,
                 g2_ref, b2_ref, wr_hi_ref, wr_lo_ref, br_ref, tri_ref, h2_ref, t_ref, route_ref, cnt_out_ref,
                 cnt_ref):
    mixed = jnp.concatenate([att_ref[...], c_ref[...]], axis=1)
    mix = jnp.dot(mixed, wo_ref[...], preferred_element_type=F32)
    h1 = _layer_norm(DEEPNORM_ALPHA * h0_ref[...] + mix, g1_ref[...], b1_ref[...])
    qx = (jnp.dot(h1.astype(BF16), wq_ref[...], preferred_element_type=F32) * (MEM_HEAD_DIM ** -0.5)).astype(BF16)
    heads = []
    for hd in range(MEM_HEADS):
        hs = slice(hd * MEM_HEAD_DIM, (hd + 1) * MEM_HEAD_DIM)
        s = lax.dot_general(qx[:, hs], km_ref[:, hs], (((1,), (1,)), ((), ())), preferred_element_type=F32)
        p = jnp.exp(s - jnp.max(s, -1, keepdims=True))
        l = jnp.sum(p, -1, keepdims=True)
        heads.append(jnp.dot(p.astype(BF16), vm_ref[:, hs], preferred_element_type=F32) / l)
    o = jnp.concatenate(heads, axis=1).astype(BF16)
    xa = jnp.dot(o, wco_ref[...], preferred_element_type=F32)
    h2 = _layer_norm(DEEPNORM_ALPHA * h1 + xa, g2_ref[...], b2_ref[...])
    h2_ref[...] = h2
    t_hi = h2.astype(BF16)
    t_rounded = t_hi.astype(F32)
    tw = _pack_bf16_pairs(t_rounded)
    for half in range(2):
        t_ref[half] = tw[:, half * ROW_WORDS:(half + 1) * ROW_WORDS]
    t_lo = (h2 - t_rounded).astype(BF16)
    logits = (jnp.dot(t_hi, wr_hi_ref[...], preferred_element_type=F32)
              + jnp.dot(t_lo, wr_hi_ref[...], preferred_element_type=F32)
              + jnp.dot(t_hi, wr_lo_ref[...], preferred_element_type=F32)) + br_ref[...]
    lane = lax.broadcasted_iota(jnp.int32, logits.shape, 1)
    neg = -jnp.inf
    big = jnp.int32(LANES)
    gl = jnp.where(lane < N_GROUPS, logits, neg)
    gmax = jnp.max(gl, -1, keepdims=True)
    p_grp = 1.0 / jnp.sum(jnp.exp(gl - gmax), -1, keepdims=True)
    grp = jnp.min(jnp.where(gl == gmax, lane, big), -1, keepdims=True)
    lo = N_GROUPS + EXPERTS_PER_GROUP * grp
    el = jnp.where((lane >= lo) & (lane < lo + EXPERTS_PER_GROUP), logits, neg)
    v1 = jnp.max(el, -1, keepdims=True)
    i1 = jnp.min(jnp.where(el == v1, lane, big), -1, keepdims=True)
    el2 = jnp.where(lane == i1, neg, el)
    v2 = jnp.max(el2, -1, keepdims=True)
    i2 = jnp.min(jnp.where(el2 == v2, lane, big), -1, keepdims=True)
    e21 = jnp.exp(v2 - v1)
    w1 = p_grp / (1.0 + e21)
    w2 = p_grp * e21 / (1.0 + e21)
    @pl.when((pl.program_id(0) == 0) & (pl.program_id(1) == 0))
    def _():
        cnt_ref[...] = jnp.zeros_like(cnt_ref)

    e1 = i1 - N_GROUPS
    e2 = i2 - N_GROUPS
    hot1 = jnp.where(lane == e1, 1.0, 0.0)
    hot2 = jnp.where(lane == e2, 1.0, 0.0)
    both = hot1 + hot2
    before = jnp.dot(tri_ref[...], both.astype(BF16), preferred_element_type=F32) + cnt_ref[...]
    rank1 = jnp.sum(hot1 * before, -1, keepdims=True)
    rank2 = jnp.sum(hot2 * before, -1, keepdims=True)
    cnt_ref[...] += jnp.sum(both, 0, keepdims=True)
    cnt_out_ref[...] = cnt_ref[...]
    route = jnp.zeros(logits.shape, F32)
    for k, val in enumerate((e1.astype(F32), e2.astype(F32), w1, w2, rank1, rank2)):
        route = jnp.where(lane == k, val, route)
    route_ref[...] = route


def _post(att, c, h0, km, vm, w_out, g1, b1, w_cq, w_co, g2, b2, wr_hi, wr_lo, br):
    b, seq, _ = att.shape
    tm = TM_POST
    tok = lambda width: pl.BlockSpec((None, tm, width), lambda bi, i: (bi, i, 0))
    memb = pl.BlockSpec((None, N_MEM, D_MODEL), lambda bi, i: (bi, 0, 0))
    const = lambda shape: pl.BlockSpec(shape, lambda bi, i: (0, 0))
    vec = const((1, D_MODEL))
    sq = const((D_MODEL, D_MODEL))
    idx = jnp.arange(tm, dtype=jnp.int32)
    tri = jnp.where(idx[:, None] > idx[None, :], 1.0, 0.0).astype(BF16)
    return pl.pallas_call(
        _post_kernel,
        grid=(b, seq // tm),
        in_specs=[tok(ATT_WIDTH), tok(CONV_WIDTH), tok(D_MODEL), memb, memb, sq, vec, vec, sq, sq, vec, vec,
                  const((D_MODEL, LANES)), const((D_MODEL, LANES)), const((1, LANES)), const((tm, tm))],
        out_specs=[tok(D_MODEL),
                   pl.BlockSpec((2, None, tm, ROW_WORDS), lambda bi, i: (0, bi, i, 0)),
                   tok(LANES), const((1, LANES))],
        out_shape=[jax.ShapeDtypeStruct((b, seq, D_MODEL), F32),
                   jax.ShapeDtypeStruct((2, b, seq, ROW_WORDS), jnp.uint32),
                   jax.ShapeDtypeStruct((b, seq, LANES), F32),
                   jax.ShapeDtypeStruct((1, LANES), F32)],
        scratch_shapes=[pltpu.VMEM((1, LANES), F32)],
        compiler_params=pltpu.CompilerParams(dimension_semantics=("arbitrary", "arbitrary"),
                                             vmem_limit_bytes=VMEM_LIMIT),
        name="post",
    )(att, c, h0, km, vm, w_out, g1, b1, w_cq, w_co, g2, b2, wr_hi, wr_lo, br, tri)


def _pack_bf16_pairs(x):
    k = x.shape[1] // 2
    bits = lax.bitcast_convert_type(x, jnp.uint32)
    return (bits[:, :k] >> 16) | (bits[:, k:] & jnp.uint32(0xFFFF0000))


def _unpack_bf16_pairs(w):
    lo = lax.bitcast_convert_type(w << 16, F32)
    hi = lax.bitcast_convert_type(w & jnp.uint32(0xFFFF0000), F32)
    return jnp.concatenate([lo, hi], axis=1)


def _sc_mesh():
    return plsc.VectorSubcoreMesh(core_axis_name="c", subcore_axis_name="s")


def _sc_scatter2(x, idx_a, idx_b, n_out):
    m = x.shape[0]

    @pl.kernel(out_type=jax.ShapeDtypeStruct((n_out, ROW_WORDS), x.dtype), mesh=_sc_mesh(), scratch_types=[])
    def scatter(x_hbm, ia_hbm, ib_hbm, o_hbm):
        def body(x_vmem, ia_vmem, ib_vmem):
            pltpu.sync_copy(x_vmem, o_hbm.at[ia_vmem.at[0]])
            pltpu.sync_copy(x_vmem, o_hbm.at[ib_vmem.at[0]])

        pltpu.emit_pipeline(
            body, grid=(m // SC_WINDOW,),
            in_specs=[pl.BlockSpec((SC_WINDOW, ROW_WORDS), lambda i: (i, 0)),
                      pl.BlockSpec((1, SC_WINDOW), lambda i: (0, i)),
                      pl.BlockSpec((1, SC_WINDOW), lambda i: (0, i))],
            out_specs=[],
            core_axis_name=("c", "s"), dimension_semantics=(pltpu.PARALLEL,),
        )(x_hbm, ia_hbm, ib_hbm)

    return scatter(x, idx_a.reshape(1, m), idx_b.reshape(1, m))


def _sc_gather(x, idx):
    m = idx.shape[0]

    @pl.kernel(out_type=jax.ShapeDtypeStruct((m, ROW_WORDS), x.dtype), mesh=_sc_mesh(), scratch_types=[])
    def gather(x_hbm, i_hbm, o_hbm):
        def body(i_vmem, o_vmem):
            pltpu.sync_copy(x_hbm.at[i_vmem.at[0]], o_vmem)

        pltpu.emit_pipeline(
            body, grid=(m // SC_WINDOW,),
            in_specs=[pl.BlockSpec((1, SC_WINDOW), lambda i: (0, i))],
            out_specs=[pl.BlockSpec((SC_WINDOW, ROW_WORDS), lambda i: (i, 0))],
            core_axis_name=("c", "s"), dimension_semantics=(pltpu.PARALLEL,),
        )(i_hbm, o_hbm)

    return gather(x, idx.reshape(1, m))


def _experts_kernel(tile_expert_ref, n_used_ref, x_ref, wg32_ref, wu32_ref, wd32_ref, y_ref, wg_ref, wu_ref, wd_ref):
    j = pl.program_id(0)
    prev = tile_expert_ref[jnp.maximum(j - 1, 0)]

    @pl.when((j == 0) | (tile_expert_ref[j] != prev))
    def _():
        wg_ref[...] = wg32_ref[...].astype(BF16)
        wu_ref[...] = wu32_ref[...].astype(BF16)
        wd_ref[...] = wd32_ref[...].astype(BF16)

    @pl.when(j < n_used_ref[0])
    def _():
        u0 = _unpack_bf16_pairs(x_ref[0])
        u1 = _unpack_bf16_pairs(x_ref[1])
        x = jnp.concatenate([u0[:, :ROW_WORDS], u1[:, :ROW_WORDS], u0[:, ROW_WORDS:], u1[:, ROW_WORDS:]],
                            axis=1).astype(BF16)
        gate = jnp.dot(x, wg_ref[...], preferred_element_type=F32)
        up = jnp.dot(x, wu_ref[...], preferred_element_type=F32)
        he = (gate * _sigmoid(gate) * up).astype(BF16)
        y = jnp.dot(he, wd_ref[...], preferred_element_type=F32)
        yw = _pack_bf16_pairs(y.astype(BF16).astype(F32))
        for half in range(2):
            y_ref[half] = yw[:, half * ROW_WORDS:(half + 1) * ROW_WORDS]


def _experts(xs, tile_expert, n_used, w_gate, w_up, w_down):
    _, rows, _ = xs.shape
    tr = TR_EXPERT
    halves = pl.BlockSpec((2, tr, ROW_WORDS), lambda j, te, nu: (0, j, 0))
    return pl.pallas_call(
        _experts_kernel,
        grid_spec=pltpu.PrefetchScalarGridSpec(
            num_scalar_prefetch=2,
            grid=(rows // tr,),
            in_specs=[halves,
                      pl.BlockSpec((None, D_MODEL, D_EXPERT), lambda j, te, nu: (te[j], 0, 0)),
                      pl.BlockSpec((None, D_MODEL, D_EXPERT), lambda j, te, nu: (te[j], 0, 0)),
                      pl.BlockSpec((None, D_EXPERT, D_MODEL), lambda j, te, nu: (te[j], 0, 0))],
            out_specs=halves,
            scratch_shapes=[pltpu.VMEM((D_MODEL, D_EXPERT), BF16), pltpu.VMEM((D_MODEL, D_EXPERT), BF16),
                            pltpu.VMEM((D_EXPERT, D_MODEL), BF16)],
        ),
        out_shape=jax.ShapeDtypeStruct(xs.shape, jnp.uint32),
        compiler_params=pltpu.CompilerParams(dimension_semantics=("arbitrary",), vmem_limit_bytes=VMEM_LIMIT),
        name="experts",
    )(tile_expert, n_used, xs, w_gate, w_up, w_down)


def _combine_kernel(h2_ref, y1_ref, y2_ref, route_ref, g3_ref, b3_ref, o_ref):
    route = route_ref[...]
    w1 = route[:, ROUTE_W:ROUTE_W + 1]
    w2 = route[:, ROUTE_W + 1:ROUTE_W + 2]
    halves = []
    for half in range(2):
        halves.append(w1 * _unpack_bf16_pairs(y1_ref[half]) + w2 * _unpack_bf16_pairs(y2_ref[half]))
    ff = jnp.concatenate([halves[0][:, :ROW_WORDS], halves[1][:, :ROW_WORDS],
                          halves[0][:, ROW_WORDS:], halves[1][:, ROW_WORDS:]], axis=1)
    o_ref[...] = _layer_norm(DEEPNORM_ALPHA * h2_ref[...] + ff, g3_ref[...], b3_ref[...])


def _combine(h2, y1, y2, route, g3, b3):
    n = h2.shape[0]
    tm = TM_COMBINE
    row = lambda width: pl.BlockSpec((tm, width), lambda i: (i, 0))
    halves = pl.BlockSpec((2, tm, ROW_WORDS), lambda i: (0, i, 0))
    vec = pl.BlockSpec((1, D_MODEL), lambda i: (0, 0))
    return pl.pallas_call(
        _combine_kernel,
        grid=(n // tm,),
        in_specs=[row(D_MODEL), halves, halves, row(LANES), vec, vec],
        out_specs=row(D_MODEL),
        out_shape=jax.ShapeDtypeStruct((n, D_MODEL), F32),
        compiler_params=pltpu.CompilerParams(dimension_semantics=("parallel",), vmem_limit_bytes=VMEM_LIMIT),
        name="combine",
    )(h2, y1, y2, route, g3, b3)


def _moe(tw, route, counts, h2, w_gate, w_up, w_down, g3, b3):
    n = h2.shape[0]
    tr = TR_EXPERT
    cap = 2 * n + N_EXPERTS * tr
    cnt = counts[0, :N_EXPERTS].astype(jnp.int32)
    seg = (cnt + tr - 1) // tr * tr
    ends = jnp.cumsum(seg)
    starts = ends - seg
    expert = route[:, ROUTE_E:ROUTE_E + 2].astype(jnp.int32)
    rank = route[:, ROUTE_RANK:ROUTE_RANK + 2].astype(jnp.int32)
    hot = expert[..., None] == jnp.arange(N_EXPERTS, dtype=jnp.int32)
    pos = rank + jnp.sum(jnp.where(hot, starts, 0), axis=-1)
    tile_start = jnp.arange(cap // tr, dtype=jnp.int32) * tr
    tile_expert = jnp.minimum(jnp.sum(tile_start[:, None] >= ends[None, :], axis=1, dtype=jnp.int32), N_EXPERTS - 1)
    n_used = (ends[-1] // tr).reshape(1)
    idx = jnp.concatenate([pos, cap + pos], axis=0)
    idx1, idx2 = idx[:, 0], idx[:, 1]
    xs = _sc_scatter2(tw.reshape(2 * n, ROW_WORDS), idx1, idx2, 2 * cap)
    ys = _experts(xs.reshape(2, cap, ROW_WORDS), tile_expert, n_used, w_gate, w_up, w_down)
    ys = ys.reshape(2 * cap, ROW_WORDS)
    y1 = _sc_gather(ys, idx1).reshape(2, n, ROW_WORDS)
    y2 = _sc_gather(ys, idx2).reshape(2, n, ROW_WORDS)
    return _combine(h2, y1, y2, route, g3, b3)


def _rope_tables(seq):
    rows = seq // GRID_W
    row = jnp.repeat(jnp.arange(rows, dtype=F32), GRID_W)
    col = jnp.tile(jnp.arange(GRID_W, dtype=F32), rows)
    inv_freq = ROPE_THETA ** (-jnp.arange(0, ROPE_AXIS_DIM, 2, dtype=F32) / ROPE_AXIS_DIM)
    ang = jnp.concatenate([row[:, None] * inv_freq, col[:, None] * inv_freq], -1)
    cos = jnp.repeat(jnp.cos(ang), 2, axis=-1)
    sin = jnp.repeat(jnp.sin(ang), 2, axis=-1)
    sign = jnp.tile(jnp.array([-1.0, 1.0], F32), HEAD_DIM // 2)
    sin = sin * sign
    return jnp.tile(cos, (1, LANES // HEAD_DIM)), jnp.tile(sin, (1, LANES // HEAD_DIM))


def _encoder(x, mem, p):
    b, seq, d = x.shape
    cos, sin = _rope_tables(seq)
    h0, q, k, v, u = _proj_in(x.reshape(b * seq, d), p["ln_in_g"], p["ln_in_b"], p["w_in"], p["qg"], p["kg"],
                              cos, sin, p["head_mean"], seq)
    c = _conv_branch(u.reshape(b, seq, CONV_WIDTH), p["conv_w"], p["conv_b"], p["cln_g"], p["cln_b"])
    att = _gqa(q.reshape(ATT_HEADS, b, seq, HEAD_DIM), k.reshape(ATT_KV_HEADS, b, seq, HEAD_DIM),
               v.reshape(b, seq, 2 * KV_WIDTH))
    km, vm = _mem_kv(mem, p["ln_mem_g"], p["ln_mem_b"], p["w_ck"], p["w_cv"])
    h2, tw, route, counts = _post(att, c, h0.reshape(b, seq, d), km, vm, p["w_out"], p["ln1_g"], p["ln1_b"],
                                  p["w_cq"], p["w_co"], p["ln2_g"], p["ln2_b"], p["wr_hi"], p["wr_lo"], p["br"])
    y = _moe(tw.reshape(2, b * seq, ROW_WORDS), route.reshape(b * seq, LANES), counts, h2.reshape(b * seq, d),
             p["w_gate"], p["w_up"], p["w_down"], p["ln3_g"], p["ln3_b"])
    return y.reshape(b, seq, d)


def kernel(x_prompt, x_sample, mem_prompt, mem_sample, ln_in_g, ln_in_b, ln_mem_g, ln_mem_b, w_in, q_norm_g, k_norm_g, conv_w, conv_b, conv_ln_g, conv_ln_b, w_mix_out, ln1_g, ln1_b, w_cq, w_ck, w_cv, w_co, ln2_g, ln2_b, w_router_g, b_router_g, w_router_e, b_router_e, w_e_gate, w_e_up, w_e_down, ln3_g, ln3_b):
    l = 0
    vec = lambda a: a.reshape(1, -1).astype(F32)
    w_r = jnp.concatenate([w_router_g[l], jnp.transpose(w_router_e[l], (1, 0, 2)).reshape(D_MODEL, N_EXPERTS)], axis=1)
    w_r = jnp.pad(w_r, ((0, 0), (0, LANES - w_r.shape[1])))
    wr_hi = w_r.astype(BF16)
    wr_lo = (w_r - wr_hi.astype(F32)).astype(BF16)
    b_r = jnp.concatenate([b_router_g[l], b_router_e[l].reshape(-1)])
    b_r = jnp.pad(b_r, (0, LANES - b_r.shape[0])).reshape(1, LANES)
    head = jnp.arange(LANES) // HEAD_DIM
    head_mean = jnp.where(head[:, None] == head[None, :], 1.0 / HEAD_DIM, 0.0).astype(BF16)
    p = {
        "ln_in_g": vec(ln_in_g), "ln_in_b": vec(ln_in_b), "ln_mem_g": vec(ln_mem_g), "ln_mem_b": vec(ln_mem_b),
        "w_in": w_in[l].astype(BF16),
        "qg": jnp.tile(q_norm_g[l], LANES // HEAD_DIM).reshape(1, LANES),
        "kg": jnp.tile(k_norm_g[l], LANES // HEAD_DIM).reshape(1, LANES),
        "head_mean": head_mean,
        "conv_w": conv_w[l], "conv_b": vec(conv_b[l]), "cln_g": vec(conv_ln_g[l]), "cln_b": vec(conv_ln_b[l]),
        "w_out": w_mix_out[l].astype(BF16), "ln1_g": vec(ln1_g[l]), "ln1_b": vec(ln1_b[l]),
        "w_cq": w_cq[l].astype(BF16), "w_ck": w_ck[l].astype(BF16), "w_cv": w_cv[l].astype(BF16),
        "w_co": w_co[l].astype(BF16), "ln2_g": vec(ln2_g[l]), "ln2_b": vec(ln2_b[l]),
        "wr_hi": wr_hi, "wr_lo": wr_lo, "br": b_r,
        "w_gate": w_e_gate[l], "w_up": w_e_up[l], "w_down": w_e_down[l],
        "ln3_g": vec(ln3_g[l]), "ln3_b": vec(ln3_b[l]),
    }
    return (_encoder(x_prompt, mem_prompt, p), _encoder(x_sample, mem_sample, p))
```

```python
import functools

import jax
import jax.numpy as jnp
from jax import lax
from jax.experimental import pallas as pl
from jax.experimental.pallas import tpu as pltpu
from jax.experimental.pallas import tpu_sc as plsc

F32 = jnp.float32
BF16 = jnp.bfloat16

D_MODEL = 1024
DEPTH = 1
GRID_W = 64
N_MEM = 256
ATT_HEADS = 8
ATT_KV_HEADS = 2
HEAD_DIM = 64
ATT_WIDTH = ATT_HEADS * HEAD_DIM
KV_WIDTH = ATT_KV_HEADS * HEAD_DIM
ROPE_THETA = 10000.0
ROPE_AXIS_DIM = HEAD_DIM // 2
CONV_WIDTH = D_MODEL - ATT_WIDTH
CONV_KSIZE = 31
CONV_PAD = CONV_KSIZE // 2
IN_WIDTH = ATT_WIDTH + 2 * KV_WIDTH + 2 * CONV_WIDTH
MEM_HEADS = 4
MEM_HEAD_DIM = D_MODEL // MEM_HEADS
N_GROUPS = 4
EXPERTS_PER_GROUP = 8
N_EXPERTS = N_GROUPS * EXPERTS_PER_GROUP
D_EXPERT = 256
LN_EPS = 1e-5
RMS_EPS = 1e-6
DEEPNORM_ALPHA = (2.0 * DEPTH) ** 0.25
LOG2_E = 1.4426950408889634
Q_SCALE = HEAD_DIM ** -0.5 * LOG2_E

LANES = 128
SUBLANES = 8
HALO = 16
VMEM_LIMIT = 56 * 1024 * 1024

TM_IN = 1024
TC_CONV = 256
TQ_ATT = 256
KV_CHUNK = 512
ATT_SAFE_SPAN = 50.0
TM_POST = 1024
TR_EXPERT = 512
TM_COMBINE = 1024
ROW_WORDS = 256
SC_WINDOW = 128
ROUTE_E, ROUTE_W, ROUTE_RANK = 0, 2, 4


def _layer_norm(x, g, b):
    mu = jnp.mean(x, -1, keepdims=True)
    xc = x - mu
    var = jnp.mean(xc * xc, -1, keepdims=True)
    return xc * lax.rsqrt(var + LN_EPS) * g + b


def _sigmoid(x):
    return 1.0 / (1.0 + jnp.exp(-x))


def _rope(x, cos, sin_signed):
    lane = lax.broadcasted_iota(jnp.int32, x.shape, 1)
    nxt = pltpu.roll(x, LANES - 1, axis=1)
    prv = pltpu.roll(x, 1, axis=1)
    return x * cos + jnp.where((lane & 1) == 0, nxt, prv) * sin_signed


def _proj_in_kernel(x_ref, g_ref, b_ref, w_ref, qg_ref, kg_ref, cos_ref, sin_ref, hm_ref,
                    h_ref, q_ref, k_ref, v_ref, u_ref):
    h = _layer_norm(x_ref[...], g_ref[...], b_ref[...])
    h_ref[...] = h
    z = jnp.dot(h.astype(BF16), w_ref[...], preferred_element_type=F32)
    i1 = ATT_WIDTH
    i2 = i1 + KV_WIDTH
    i3 = i2 + KV_WIDTH
    i4 = i3 + CONV_WIDTH
    cos = cos_ref[...]
    sin = sin_ref[...]
    hm = hm_ref[...]
    heads_per_tile = LANES // HEAD_DIM
    for j in range(ATT_WIDTH // LANES):
        qj = z[:, j * LANES:(j + 1) * LANES]
        ms = jnp.dot((qj * qj).astype(BF16), hm, preferred_element_type=F32)
        qn = qj * lax.rsqrt(ms + RMS_EPS) * qg_ref[...]
        qr = (_rope(qn, cos, sin) * Q_SCALE).astype(BF16)
        for r in range(heads_per_tile):
            q_ref[j * heads_per_tile + r] = qr[:, r * HEAD_DIM:(r + 1) * HEAD_DIM]
    kj = z[:, i1:i2]
    ms = jnp.dot((kj * kj).astype(BF16), hm, preferred_element_type=F32)
    kn = kj * lax.rsqrt(ms + RMS_EPS) * kg_ref[...]
    kr = _rope(kn, cos, sin).astype(BF16)
    vj = z[:, i2:i3].astype(BF16)
    ones = jnp.ones((vj.shape[0], HEAD_DIM), BF16)
    for g in range(ATT_KV_HEADS):
        k_ref[g] = kr[:, g * HEAD_DIM:(g + 1) * HEAD_DIM]
    v_ref[...] = jnp.concatenate([vj[:, :HEAD_DIM], ones, vj[:, HEAD_DIM:], ones], axis=1)
    u_ref[...] = (z[:, i3:i4] * _sigmoid(z[:, i4:])).astype(BF16)


def _proj_in(x, ln_g, ln_b, w_in, qg, kg, cos, sin, hm, seq):
    t = x.shape[0]
    tm = TM_IN
    nseq = seq // tm
    row = lambda i: (i, 0)
    const = lambda i: (0, 0)
    return pl.pallas_call(
        _proj_in_kernel,
        grid=(t // tm,),
        in_specs=[
            pl.BlockSpec((tm, D_MODEL), row),
            pl.BlockSpec((1, D_MODEL), const),
            pl.BlockSpec((1, D_MODEL), const),
            pl.BlockSpec((D_MODEL, IN_WIDTH), const),
            pl.BlockSpec((1, LANES), const),
            pl.BlockSpec((1, LANES), const),
            pl.BlockSpec((tm, LANES), lambda i: (i % nseq, 0)),
            pl.BlockSpec((tm, LANES), lambda i: (i % nseq, 0)),
            pl.BlockSpec((LANES, LANES), const),
        ],
        out_specs=[
            pl.BlockSpec((tm, D_MODEL), row),
            pl.BlockSpec((ATT_HEADS, tm, HEAD_DIM), lambda i: (0, i, 0)),
            pl.BlockSpec((ATT_KV_HEADS, tm, HEAD_DIM), lambda i: (0, i, 0)),
            pl.BlockSpec((tm, 2 * KV_WIDTH), row),
            pl.BlockSpec((tm, CONV_WIDTH), row),
        ],
        out_shape=[
            jax.ShapeDtypeStruct((t, D_MODEL), F32),
            jax.ShapeDtypeStruct((ATT_HEADS, t, HEAD_DIM), BF16),
            jax.ShapeDtypeStruct((ATT_KV_HEADS, t, HEAD_DIM), BF16),
            jax.ShapeDtypeStruct((t, 2 * KV_WIDTH), BF16),
            jax.ShapeDtypeStruct((t, CONV_WIDTH), BF16),
        ],
        compiler_params=pltpu.CompilerParams(dimension_semantics=("parallel",), vmem_limit_bytes=VMEM_LIMIT),
        name="proj_in",
    )(x, ln_g, ln_b, w_in, qg, kg, cos, sin, hm)


def _conv_kernel(u_ref, w_ref, cb_ref, g_ref, b_ref, o_ref, win_ref, sh_ref, *, seq):
    i = pl.program_id(1)
    tc = TC_CONV
    t0 = pl.multiple_of(i * tc, tc)
    top_start = pl.multiple_of(jnp.maximum(t0 - HALO, 0), HALO)
    bot_start = pl.multiple_of(jnp.minimum(t0 + tc, seq - HALO), HALO)
    top = u_ref[pl.ds(top_start, HALO), :].astype(F32)
    bot = u_ref[pl.ds(bot_start, HALO), :].astype(F32)
    win_ref[0:HALO, :] = jnp.where(i > 0, top, 0.0)
    win_ref[HALO:HALO + tc, :] = u_ref[pl.ds(t0, tc), :].astype(F32)
    win_ref[HALO + tc:, :] = jnp.where(i < pl.num_programs(1) - 1, bot, 0.0)
    sh_rows = sh_ref.shape[1]
    for s in range(1, SUBLANES):
        sh_ref[s] = win_ref[s:s + sh_rows, :]
    w = w_ref[...]
    cols = []
    for c in range(CONV_WIDTH // LANES):
        cs = slice(c * LANES, (c + 1) * LANES)
        acc = jnp.zeros((tc, LANES), F32)
        for j in range(CONV_KSIZE):
            off = HALO - CONV_PAD + j
            base, s = off - off % SUBLANES, off % SUBLANES
            tap = win_ref[base:base + tc, cs] if s == 0 else sh_ref[s, base:base + tc, cs]
            acc = acc + tap * w[j:j + 1, cs]
        cols.append(acc)
    y = jnp.concatenate(cols, axis=1) + cb_ref[...]
    y = _layer_norm(y, g_ref[...], b_ref[...])
    o_ref[...] = (y * _sigmoid(y)).astype(BF16)


def _conv_branch(u, conv_w, conv_b, cln_g, cln_b):
    b, seq, _ = u.shape
    tc = TC_CONV
    const = lambda bi, i: (0, 0)
    return pl.pallas_call(
        functools.partial(_conv_kernel, seq=seq),
        grid=(b, seq // tc),
        in_specs=[
            pl.BlockSpec((None, seq, CONV_WIDTH), lambda bi, i: (bi, 0, 0)),
            pl.BlockSpec((CONV_KSIZE, CONV_WIDTH), const),
            pl.BlockSpec((1, CONV_WIDTH), const),
            pl.BlockSpec((1, CONV_WIDTH), const),
            pl.BlockSpec((1, CONV_WIDTH), const),
        ],
        out_specs=pl.BlockSpec((None, tc, CONV_WIDTH), lambda bi, i: (bi, i, 0)),
        out_shape=jax.ShapeDtypeStruct((b, seq, CONV_WIDTH), BF16),
        scratch_shapes=[pltpu.VMEM((tc + 2 * HALO, CONV_WIDTH), F32),
                        pltpu.VMEM((SUBLANES, tc + 2 * HALO - SUBLANES, CONV_WIDTH), F32)],
        compiler_params=pltpu.CompilerParams(dimension_semantics=("parallel", "arbitrary"),
                                             vmem_limit_bytes=VMEM_LIMIT),
        name="conv_branch",
    )(u, conv_w, conv_b, cln_g, cln_b)


def _gqa_kernel(q_ref, k_ref, v_ref, o_ref, kmax_ref):
    rep = ATT_HEADS // ATT_KV_HEADS
    tq = q_ref.shape[1]
    seq = k_ref.shape[1]
    n_chunks = seq // KV_CHUNK
    nt_dims = (((1,), (1,)), ((), ()))

    @pl.when(pl.program_id(1) == 0)
    def _():
        for g in range(ATT_KV_HEADS):
            kf = k_ref[g].astype(F32)
            kmax_ref[g] = jnp.sqrt(jnp.max(jnp.sum(kf * kf, -1, keepdims=True), axis=0, keepdims=True))

    qs = [q_ref[g * rep:(g + 1) * rep].reshape(rep * tq, HEAD_DIM) for g in range(ATT_KV_HEADS)]
    bound = []
    for g in range(ATT_KV_HEADS):
        qf = qs[g].astype(F32)
        bound.append(jnp.sqrt(jnp.sum(qf * qf, -1, keepdims=True)) * kmax_ref[g])
    worst = jnp.maximum(jnp.max(bound[0]), jnp.max(bound[1]))

    def finish(acc):
        outs = []
        for g in range(ATT_KV_HEADS):
            o = acc[g] / pltpu.roll(acc[g], HEAD_DIM, axis=1)
            outs.extend(o[r * tq:(r + 1) * tq, :HEAD_DIM] for r in range(rep))
        o_ref[...] = jnp.concatenate(outs, axis=1).astype(BF16)

    @pl.when(worst <= ATT_SAFE_SPAN)
    def _():
        acc = [None] * ATT_KV_HEADS
        for c in range(n_chunks):
            rows = slice(c * KV_CHUNK, (c + 1) * KV_CHUNK)
            for g in range(ATT_KV_HEADS):
                s = lax.dot_general(qs[g], k_ref[g, rows], nt_dims, preferred_element_type=F32)
                p = jnp.exp2(s - bound[g]).astype(BF16)
                d = jnp.dot(p, v_ref[rows, g * LANES:(g + 1) * LANES], preferred_element_type=F32)
                acc[g] = d if c == 0 else acc[g] + d
        finish(acc)

    @pl.when(worst > ATT_SAFE_SPAN)
    def _():
        def step(c, carry):
            rows = pl.ds(pl.multiple_of(c * KV_CHUNK, KV_CHUNK), KV_CHUNK)
            new = []
            for g in range(ATT_KV_HEADS):
                m, acc = carry[g]
                s = lax.dot_general(qs[g], k_ref[g, rows], nt_dims, preferred_element_type=F32)
                m_new = jnp.maximum(m, jnp.max(s, -1, keepdims=True))
                p = jnp.exp2(s - m_new).astype(BF16)
                d = jnp.dot(p, v_ref[rows, g * LANES:(g + 1) * LANES], preferred_element_type=F32)
                new.append((m_new, jnp.exp2(m - m_new) * acc + d))
            return tuple(new)

        init = tuple((jnp.full((rep * tq, 1), -jnp.inf, F32), jnp.zeros((rep * tq, LANES), F32))
                     for _ in range(ATT_KV_HEADS))
        out = lax.fori_loop(0, n_chunks, step, init)
        finish([out[g][1] for g in range(ATT_KV_HEADS)])


def _gqa(q, k, v):
    _, b, seq, _ = q.shape
    tq = TQ_ATT
    return pl.pallas_call(
        _gqa_kernel,
        grid=(b, seq // tq),
        in_specs=[
            pl.BlockSpec((ATT_HEADS, None, tq, HEAD_DIM), lambda bi, i: (0, bi, i, 0)),
            pl.BlockSpec((ATT_KV_HEADS, None, seq, HEAD_DIM), lambda bi, i: (0, bi, 0, 0)),
            pl.BlockSpec((None, seq, ATT_KV_HEADS * LANES), lambda bi, i: (bi, 0, 0)),
        ],
        out_specs=pl.BlockSpec((None, tq, ATT_WIDTH), lambda bi, i: (bi, i, 0)),
        out_shape=jax.ShapeDtypeStruct((b, seq, ATT_WIDTH), BF16),
        scratch_shapes=[pltpu.VMEM((ATT_KV_HEADS, 1, 1), F32)],
        compiler_params=pltpu.CompilerParams(dimension_semantics=("arbitrary", "arbitrary"),
                                             vmem_limit_bytes=VMEM_LIMIT),
        name="gqa",
    )(q, k, v)


def _mem_kv_kernel(m_ref, g_ref, b_ref, wk_ref, wv_ref, k_ref, v_ref):
    m = _layer_norm(m_ref[...], g_ref[...], b_ref[...]).astype(BF16)
    k_ref[...] = jnp.dot(m, wk_ref[...], preferred_element_type=F32).astype(BF16)
    v_ref[...] = jnp.dot(m, wv_ref[...], preferred_element_type=F32).astype(BF16)


def _mem_kv(mem, ln_g, ln_b, w_ck, w_cv):
    b = mem.shape[0]
    const = lambda bi: (0, 0)
    blk = pl.BlockSpec((None, N_MEM, D_MODEL), lambda bi: (bi, 0, 0))
    return pl.pallas_call(
        _mem_kv_kernel,
        grid=(b,),
        in_specs=[blk, pl.BlockSpec((1, D_MODEL), const), pl.BlockSpec((1, D_MODEL), const),
                  pl.BlockSpec((D_MODEL, D_MODEL), const), pl.BlockSpec((D_MODEL, D_MODEL), const)],
        out_specs=[blk, blk],
        out_shape=[jax.ShapeDtypeStruct((b, N_MEM, D_MODEL), BF16)] * 2,
        compiler_params=pltpu.CompilerParams(dimension_semantics=("parallel",), vmem_limit_bytes=VMEM_LIMIT),
        name="mem_kv",
    )(mem, ln_g, ln_b, w_ck, w_cv)


def _post_kernel(att_ref, c_ref, h0_ref, km_ref, vm_ref, wo_ref, g1_ref, b1_ref, wq_ref, wco_ref,
                 g2_ref, b2_ref, wr_hi_ref, wr_lo_ref, br_ref, tri_ref, h2_ref, t_ref, route_ref, cnt_out_ref,
                 cnt_ref):
    tm = att_ref.shape[0]
    halves = [slice(0, tm // 2), slice(tm // 2, tm)]
    mixed = [jnp.concatenate([att_ref[r, :], c_ref[r, :]], axis=1) for r in halves]
    mix = [jnp.dot(m, wo_ref[...], preferred_element_type=F32) for m in mixed]
    h1 = [_layer_norm(DEEPNORM_ALPHA * h0_ref[r, :] + m, g1_ref[...], b1_ref[...]) for r, m in zip(halves, mix)]
    qx = [(jnp.dot(h.astype(BF16), wq_ref[...], preferred_element_type=F32) * (MEM_HEAD_DIM ** -0.5)).astype(BF16)
          for h in h1]
    heads = [[], []]
    for hd in range(MEM_HEADS):
        hs = slice(hd * MEM_HEAD_DIM, (hd + 1) * MEM_HEAD_DIM)
        for k in range(2):
            s = lax.dot_general(qx[k][:, hs], km_ref[:, hs], (((1,), (1,)), ((), ())), preferred_element_type=F32)
            p = jnp.exp(s - jnp.max(s, -1, keepdims=True))
            l = jnp.sum(p, -1, keepdims=True)
            heads[k].append(jnp.dot(p.astype(BF16), vm_ref[:, hs], preferred_element_type=F32) / l)
    o = [jnp.concatenate(hk, axis=1).astype(BF16) for hk in heads]
    xa = [jnp.dot(ok, wco_ref[...], preferred_element_type=F32) for ok in o]
    h2 = jnp.concatenate([_layer_norm(DEEPNORM_ALPHA * h + x, g2_ref[...], b2_ref[...]) for h, x in zip(h1, xa)],
                         axis=0)
    h2_ref[...] = h2
    t_hi = h2.astype(BF16)
    t_rounded = t_hi.astype(F32)
    tw = _pack_bf16_pairs(t_rounded)
    for half in range(2):
        t_ref[half] = tw[:, half * ROW_WORDS:(half + 1) * ROW_WORDS]
    t_lo = (h2 - t_rounded).astype(BF16)
    logits = (jnp.dot(t_hi, wr_hi_ref[...], preferred_element_type=F32)
              + jnp.dot(t_lo, wr_hi_ref[...], preferred_element_type=F32)
              + jnp.dot(t_hi, wr_lo_ref[...], preferred_element_type=F32)) + br_ref[...]
    lane = lax.broadcasted_iota(jnp.int32, logits.shape, 1)
    neg = -jnp.inf
    big = jnp.int32(LANES)
    gl = jnp.where(lane < N_GROUPS, logits, neg)
    gmax = jnp.max(gl, -1, keepdims=True)
    p_grp = 1.0 / jnp.sum(jnp.exp(gl - gmax), -1, keepdims=True)
    grp = jnp.min(jnp.where(gl == gmax, lane, big), -1, keepdims=True)
    lo = N_GROUPS + EXPERTS_PER_GROUP * grp
    el = jnp.where((lane >= lo) & (lane < lo + EXPERTS_PER_GROUP), logits, neg)
    v1 = jnp.max(el, -1, keepdims=True)
    i1 = jnp.min(jnp.where(el == v1, lane, big), -1, keepdims=True)
    el2 = jnp.where(lane == i1, neg, el)
    v2 = jnp.max(el2, -1, keepdims=True)
    i2 = jnp.min(jnp.where(el2 == v2, lane, big), -1, keepdims=True)
    e21 = jnp.exp(v2 - v1)
    w1 = p_grp / (1.0 + e21)
    w2 = p_grp * e21 / (1.0 + e21)
    @pl.when((pl.program_id(0) == 0) & (pl.program_id(1) == 0))
    def _():
        cnt_ref[...] = jnp.zeros_like(cnt_ref)

    e1 = i1 - N_GROUPS
    e2 = i2 - N_GROUPS
    hot1 = jnp.where(lane == e1, 1.0, 0.0)
    hot2 = jnp.where(lane == e2, 1.0, 0.0)
    both = hot1 + hot2
    before = jnp.dot(tri_ref[...], both.astype(BF16), preferred_element_type=F32) + cnt_ref[...]
    rank1 = jnp.sum(hot1 * before, -1, keepdims=True)
    rank2 = jnp.sum(hot2 * before, -1, keepdims=True)
    cnt_ref[...] += jnp.sum(both, 0, keepdims=True)
    cnt_out_ref[...] = cnt_ref[...]
    route = jnp.zeros(logits.shape, F32)
    for k, val in enumerate((e1.astype(F32), e2.astype(F32), w1, w2, rank1, rank2)):
        route = jnp.where(lane == k, val, route)
    route_ref[...] = route


def _post(att, c, h0, km, vm, w_out, g1, b1, w_cq, w_co, g2, b2, wr_hi, wr_lo, br):
    b, seq, _ = att.shape
    tm = TM_POST
    tok = lambda width: pl.BlockSpec((None, tm, width), lambda bi, i: (bi, i, 0))
    memb = pl.BlockSpec((None, N_MEM, D_MODEL), lambda bi, i: (bi, 0, 0))
    const = lambda shape: pl.BlockSpec(shape, lambda bi, i: (0, 0))
    vec = const((1, D_MODEL))
    sq = const((D_MODEL, D_MODEL))
    idx = jnp.arange(tm, dtype=jnp.int32)
    tri = jnp.where(idx[:, None] > idx[None, :], 1.0, 0.0).astype(BF16)
    return pl.pallas_call(
        _post_kernel,
        grid=(b, seq // tm),
        in_specs=[tok(ATT_WIDTH), tok(CONV_WIDTH), tok(D_MODEL), memb, memb, sq, vec, vec, sq, sq, vec, vec,
                  const((D_MODEL, LANES)), const((D_MODEL, LANES)), const((1, LANES)), const((tm, tm))],
        out_specs=[tok(D_MODEL),
                   pl.BlockSpec((2, None, tm, ROW_WORDS), lambda bi, i: (0, bi, i, 0)),
                   tok(LANES), const((1, LANES))],
        out_shape=[jax.ShapeDtypeStruct((b, seq, D_MODEL), F32),
                   jax.ShapeDtypeStruct((2, b, seq, ROW_WORDS), jnp.uint32),
                   jax.ShapeDtypeStruct((b, seq, LANES), F32),
                   jax.ShapeDtypeStruct((1, LANES), F32)],
        scratch_shapes=[pltpu.VMEM((1, LANES), F32)],
        compiler_params=pltpu.CompilerParams(dimension_semantics=("arbitrary", "arbitrary"),
                                             vmem_limit_bytes=VMEM_LIMIT),
        name="post",
    )(att, c, h0, km, vm, w_out, g1, b1, w_cq, w_co, g2, b2, wr_hi, wr_lo, br, tri)


def _pack_bf16_pairs(x):
    k = x.shape[1] // 2
    bits = lax.bitcast_convert_type(x, jnp.uint32)
    return (bits[:, :k] >> 16) | (bits[:, k:] & jnp.uint32(0xFFFF0000))


def _unpack_bf16_pairs(w):
    lo = lax.bitcast_convert_type(w << 16, F32)
    hi = lax.bitcast_convert_type(w & jnp.uint32(0xFFFF0000), F32)
    return jnp.concatenate([lo, hi], axis=1)


def _sc_mesh():
    return plsc.VectorSubcoreMesh(core_axis_name="c", subcore_axis_name="s")


def _sc_scatter2(x, idx_a, idx_b, n_out):
    m = x.shape[0]

    @pl.kernel(out_type=jax.ShapeDtypeStruct((n_out, ROW_WORDS), x.dtype), mesh=_sc_mesh(), scratch_types=[])
    def scatter(x_hbm, ia_hbm, ib_hbm, o_hbm):
        def body(x_vmem, ia_vmem, ib_vmem):
            pltpu.sync_copy(x_vmem, o_hbm.at[ia_vmem.at[0]])
            pltpu.sync_copy(x_vmem, o_hbm.at[ib_vmem.at[0]])

        pltpu.emit_pipeline(
            body, grid=(m // SC_WINDOW,),
            in_specs=[pl.BlockSpec((SC_WINDOW, ROW_WORDS), lambda i: (i, 0)),
                      pl.BlockSpec((1, SC_WINDOW), lambda i: (0, i)),
                      pl.BlockSpec((1, SC_WINDOW), lambda i: (0, i))],
            out_specs=[],
            core_axis_name=("c", "s"), dimension_semantics=(pltpu.PARALLEL,),
        )(x_hbm, ia_hbm, ib_hbm)

    return scatter(x, idx_a.reshape(1, m), idx_b.reshape(1, m))


def _sc_gather(x, idx):
    m = idx.shape[0]

    @pl.kernel(out_type=jax.ShapeDtypeStruct((m, ROW_WORDS), x.dtype), mesh=_sc_mesh(), scratch_types=[])
    def gather(x_hbm, i_hbm, o_hbm):
        def body(i_vmem, o_vmem):
            pltpu.sync_copy(x_hbm.at[i_vmem.at[0]], o_vmem)

        pltpu.emit_pipeline(
            body, grid=(m // SC_WINDOW,),
            in_specs=[pl.BlockSpec((1, SC_WINDOW), lambda i: (0, i))],
            out_specs=[pl.BlockSpec((SC_WINDOW, ROW_WORDS), lambda i: (i, 0))],
            core_axis_name=("c", "s"), dimension_semantics=(pltpu.PARALLEL,),
        )(i_hbm, o_hbm)

    return gather(x, idx.reshape(1, m))


def _experts_kernel(tile_expert_ref, n_used_ref, x_ref, wg32_ref, wu32_ref, wd32_ref, y_ref, wg_ref, wu_ref, wd_ref):
    j = pl.program_id(0)
    prev = tile_expert_ref[jnp.maximum(j - 1, 0)]

    @pl.when((j == 0) | (tile_expert_ref[j] != prev))
    def _():
        wg_ref[...] = wg32_ref[...].astype(BF16)
        wu_ref[...] = wu32_ref[...].astype(BF16)
        wd_ref[...] = wd32_ref[...].astype(BF16)

    @pl.when(j < n_used_ref[0])
    def _():
        u0 = _unpack_bf16_pairs(x_ref[0])
        u1 = _unpack_bf16_pairs(x_ref[1])
        x = jnp.concatenate([u0[:, :ROW_WORDS], u1[:, :ROW_WORDS], u0[:, ROW_WORDS:], u1[:, ROW_WORDS:]],
                            axis=1).astype(BF16)
        gate = jnp.dot(x, wg_ref[...], preferred_element_type=F32)
        up = jnp.dot(x, wu_ref[...], preferred_element_type=F32)
        he = (gate * _sigmoid(gate) * up).astype(BF16)
        y = jnp.dot(he, wd_ref[...], preferred_element_type=F32)
        yw = _pack_bf16_pairs(y.astype(BF16).astype(F32))
        for half in range(2):
            y_ref[half] = yw[:, half * ROW_WORDS:(half + 1) * ROW_WORDS]


def _experts(xs, tile_expert, n_used, w_gate, w_up, w_down):
    _, rows, _ = xs.shape
    tr = TR_EXPERT
    halves = pl.BlockSpec((2, tr, ROW_WORDS), lambda j, te, nu: (0, jnp.minimum(j, nu[0] - 1), 0))
    return pl.pallas_call(
        _experts_kernel,
        grid_spec=pltpu.PrefetchScalarGridSpec(
            num_scalar_prefetch=2,
            grid=(rows // tr,),
            in_specs=[halves,
                      pl.BlockSpec((None, D_MODEL, D_EXPERT), lambda j, te, nu: (te[j], 0, 0)),
                      pl.BlockSpec((None, D_MODEL, D_EXPERT), lambda j, te, nu: (te[j], 0, 0)),
                      pl.BlockSpec((None, D_EXPERT, D_MODEL), lambda j, te, nu: (te[j], 0, 0))],
            out_specs=halves,
            scratch_shapes=[pltpu.VMEM((D_MODEL, D_EXPERT), BF16), pltpu.VMEM((D_MODEL, D_EXPERT), BF16),
                            pltpu.VMEM((D_EXPERT, D_MODEL), BF16)],
        ),
        out_shape=jax.ShapeDtypeStruct(xs.shape, jnp.uint32),
        compiler_params=pltpu.CompilerParams(dimension_semantics=("arbitrary",), vmem_limit_bytes=VMEM_LIMIT),
        name="experts",
    )(tile_expert, n_used, xs, w_gate, w_up, w_down)


def _combine_kernel(h2_ref, y1_ref, y2_ref, route_ref, g3_ref, b3_ref, o_ref):
    route = route_ref[...]
    w1 = route[:, ROUTE_W:ROUTE_W + 1]
    w2 = route[:, ROUTE_W + 1:ROUTE_W + 2]
    halves = []
    for half in range(2):
        halves.append(w1 * _unpack_bf16_pairs(y1_ref[half]) + w2 * _unpack_bf16_pairs(y2_ref[half]))
    ff = jnp.concatenate([halves[0][:, :ROW_WORDS], halves[1][:, :ROW_WORDS],
                          halves[0][:, ROW_WORDS:], halves[1][:, ROW_WORDS:]], axis=1)
    o_ref[...] = _layer_norm(DEEPNORM_ALPHA * h2_ref[...] + ff, g3_ref[...], b3_ref[...])


def _combine(h2, y1, y2, route, g3, b3):
    n = h2.shape[0]
    tm = TM_COMBINE
    row = lambda width: pl.BlockSpec((tm, width), lambda i: (i, 0))
    halves = pl.BlockSpec((2, tm, ROW_WORDS), lambda i: (0, i, 0))
    vec = pl.BlockSpec((1, D_MODEL), lambda i: (0, 0))
    return pl.pallas_call(
        _combine_kernel,
        grid=(n // tm,),
        in_specs=[row(D_MODEL), halves, halves, row(LANES), vec, vec],
        out_specs=row(D_MODEL),
        out_shape=jax.ShapeDtypeStruct((n, D_MODEL), F32),
        compiler_params=pltpu.CompilerParams(dimension_semantics=("parallel",), vmem_limit_bytes=VMEM_LIMIT),
        name="combine",
    )(h2, y1, y2, route, g3, b3)


def _moe(tw, route, counts, h2, w_gate, w_up, w_down, g3, b3):
    n = h2.shape[0]
    tr = TR_EXPERT
    cap = 2 * n + N_EXPERTS * tr
    cnt = counts[0, :N_EXPERTS].astype(jnp.int32)
    seg = (cnt + tr - 1) // tr * tr
    ends = jnp.cumsum(seg)
    starts = ends - seg
    expert = route[:, ROUTE_E:ROUTE_E + 2].astype(jnp.int32)
    rank = route[:, ROUTE_RANK:ROUTE_RANK + 2].astype(jnp.int32)
    hot = expert[..., None] == jnp.arange(N_EXPERTS, dtype=jnp.int32)
    pos = rank + jnp.sum(jnp.where(hot, starts, 0), axis=-1)
    tile_start = jnp.arange(cap // tr, dtype=jnp.int32) * tr
    tile_expert = jnp.minimum(jnp.sum(tile_start[:, None] >= ends[None, :], axis=1, dtype=jnp.int32), N_EXPERTS - 1)
    n_used = (ends[-1] // tr).reshape(1)
    idx = jnp.concatenate([pos, cap + pos], axis=0)
    idx1, idx2 = idx[:, 0], idx[:, 1]
    xs = _sc_scatter2(tw.reshape(2 * n, ROW_WORDS), idx1, idx2, 2 * cap)
    ys = _experts(xs.reshape(2, cap, ROW_WORDS), tile_expert, n_used, w_gate, w_up, w_down)
    ys = ys.reshape(2 * cap, ROW_WORDS)
    y1 = _sc_gather(ys, idx1).reshape(2, n, ROW_WORDS)
    y2 = _sc_gather(ys, idx2).reshape(2, n, ROW_WORDS)
    return _combine(h2, y1, y2, route, g3, b3)


def _rope_tables(seq):
    rows = seq // GRID_W
    row = jnp.repeat(jnp.arange(rows, dtype=F32), GRID_W)
    col = jnp.tile(jnp.arange(GRID_W, dtype=F32), rows)
    inv_freq = ROPE_THETA ** (-jnp.arange(0, ROPE_AXIS_DIM, 2, dtype=F32) / ROPE_AXIS_DIM)
    ang = jnp.concatenate([row[:, None] * inv_freq, col[:, None] * inv_freq], -1)
    cos = jnp.repeat(jnp.cos(ang), 2, axis=-1)
    sin = jnp.repeat(jnp.sin(ang), 2, axis=-1)
    sign = jnp.tile(jnp.array([-1.0, 1.0], F32), HEAD_DIM // 2)
    sin = sin * sign
    return jnp.tile(cos, (1, LANES // HEAD_DIM)), jnp.tile(sin, (1, LANES // HEAD_DIM))


def _encoder(x, mem, p):
    b, seq, d = x.shape
    cos, sin = _rope_tables(seq)
    h0, q, k, v, u = _proj_in(x.reshape(b * seq, d), p["ln_in_g"], p["ln_in_b"], p["w_in"], p["qg"], p["kg"],
                              cos, sin, p["head_mean"], seq)
    c = _conv_branch(u.reshape(b, seq, CONV_WIDTH), p["conv_w"], p["conv_b"], p["cln_g"], p["cln_b"])
    att = _gqa(q.reshape(ATT_HEADS, b, seq, HEAD_DIM), k.reshape(ATT_KV_HEADS, b, seq, HEAD_DIM),
               v.reshape(b, seq, 2 * KV_WIDTH))
    km, vm = _mem_kv(mem, p["ln_mem_g"], p["ln_mem_b"], p["w_ck"], p["w_cv"])
    h2, tw, route, counts = _post(att, c, h0.reshape(b, seq, d), km, vm, p["w_out"], p["ln1_g"], p["ln1_b"],
                                  p["w_cq"], p["w_co"], p["ln2_g"], p["ln2_b"], p["wr_hi"], p["wr_lo"], p["br"])
    y = _moe(tw.reshape(2, b * seq, ROW_WORDS), route.reshape(b * seq, LANES), counts, h2.reshape(b * seq, d),
             p["w_gate"], p["w_up"], p["w_down"], p["ln3_g"], p["ln3_b"])
    return y.reshape(b, seq, d)


def kernel(x_prompt, x_sample, mem_prompt, mem_sample, ln_in_g, ln_in_b, ln_mem_g, ln_mem_b, w_in, q_norm_g, k_norm_g, conv_w, conv_b, conv_ln_g, conv_ln_b, w_mix_out, ln1_g, ln1_b, w_cq, w_ck, w_cv, w_co, ln2_g, ln2_b, w_router_g, b_router_g, w_router_e, b_router_e, w_e_gate, w_e_up, w_e_down, ln3_g, ln3_b):
    l = 0
    vec = lambda a: a.reshape(1, -1).astype(F32)
    w_r = jnp.concatenate([w_router_g[l], jnp.transpose(w_router_e[l], (1, 0, 2)).reshape(D_MODEL, N_EXPERTS)], axis=1)
    w_r = jnp.pad(w_r, ((0, 0), (0, LANES - w_r.shape[1])))
    wr_hi = w_r.astype(BF16)
    wr_lo = (w_r - wr_hi.astype(F32)).astype(BF16)
    b_r = jnp.concatenate([b_router_g[l], b_router_e[l].reshape(-1)])
    b_r = jnp.pad(b_r, (0, LANES - b_r.shape[0])).reshape(1, LANES)
    head = jnp.arange(LANES) // HEAD_DIM
    head_mean = jnp.where(head[:, None] == head[None, :], 1.0 / HEAD_DIM, 0.0).astype(BF16)
    p = {
        "ln_in_g": vec(ln_in_g), "ln_in_b": vec(ln_in_b), "ln_mem_g": vec(ln_mem_g), "ln_mem_b": vec(ln_mem_b),
        "w_in": w_in[l].astype(BF16),
        "qg": jnp.tile(q_norm_g[l], LANES // HEAD_DIM).reshape(1, LANES),
        "kg": jnp.tile(k_norm_g[l], LANES // HEAD_DIM).reshape(1, LANES),
        "head_mean": head_mean,
        "conv_w": conv_w[l], "conv_b": vec(conv_b[l]), "cln_g": vec(conv_ln_g[l]), "cln_b": vec(conv_ln_b[l]),
        "w_out": w_mix_out[l].astype(BF16), "ln1_g": vec(ln1_g[l]), "ln1_b": vec(ln1_b[l]),
        "w_cq": w_cq[l].astype(BF16), "w_ck": w_ck[l].astype(BF16), "w_cv": w_cv[l].astype(BF16),
        "w_co": w_co[l].astype(BF16), "ln2_g": vec(ln2_g[l]), "ln2_b": vec(ln2_b[l]),
        "wr_hi": wr_hi, "wr_lo": wr_lo, "br": b_r,
        "w_gate": w_e_gate[l], "w_up": w_e_up[l], "w_down": w_e_down[l],
        "ln3_g": vec(ln3_g[l]), "ln3_b": vec(ln3_b[l]),
    }
    return (_encoder(x_prompt, mem_prompt, p), _encoder(x_sample, mem_sample, p))
```

```python
import functools

import jax
import jax.numpy as jnp
from jax import lax
from jax.experimental import pallas as pl
from jax.experimental.pallas import tpu as pltpu
from jax.experimental.pallas import tpu_sc as plsc

F32 = jnp.float32
BF16 = jnp.bfloat16

D_MODEL = 1024
DEPTH = 1
GRID_W = 64
N_MEM = 256
ATT_HEADS = 8
ATT_KV_HEADS = 2
HEAD_DIM = 64
ATT_WIDTH = ATT_HEADS * HEAD_DIM
KV_WIDTH = ATT_KV_HEADS * HEAD_DIM
ROPE_THETA = 10000.0
ROPE_AXIS_DIM = HEAD_DIM // 2
CONV_WIDTH = D_MODEL - ATT_WIDTH
CONV_KSIZE = 31
CONV_PAD = CONV_KSIZE // 2
IN_WIDTH = ATT_WIDTH + 2 * KV_WIDTH + 2 * CONV_WIDTH
MEM_HEADS = 4
MEM_HEAD_DIM = D_MODEL // MEM_HEADS
N_GROUPS = 4
EXPERTS_PER_GROUP = 8
N_EXPERTS = N_GROUPS * EXPERTS_PER_GROUP
D_EXPERT = 256
LN_EPS = 1e-5
RMS_EPS = 1e-6
DEEPNORM_ALPHA = (2.0 * DEPTH) ** 0.25
LOG2_E = 1.4426950408889634
Q_SCALE = HEAD_DIM ** -0.5 * LOG2_E

LANES = 128
SUBLANES = 8
HALO = 16
VMEM_LIMIT = 56 * 1024 * 1024

TM_IN = 1024
IN_SPLIT = 4
TC_CONV = 256
TQ_ATT = 256
KV_CHUNK = 512
ATT_SAFE_SPAN = 50.0
TM_POST = 1024
POST_SPLIT = 4
TR_EXPERT = 512
TM_COMBINE = 1024
ROW_WORDS = 256
SC_WINDOW = 128
ROUTE_E, ROUTE_W, ROUTE_RANK = 0, 2, 4
ROUTE_LANES = 8


def _layer_norm(x, g, b):
    mu = jnp.mean(x, -1, keepdims=True)
    xc = x - mu
    var = jnp.mean(xc * xc, -1, keepdims=True)
    return xc * lax.rsqrt(var + LN_EPS) * g + b


def _sigmoid(x):
    return 1.0 / (1.0 + jnp.exp(-x))


def _rope(x, cos, sin_signed):
    lane = lax.broadcasted_iota(jnp.int32, x.shape, 1)
    nxt = pltpu.roll(x, LANES - 1, axis=1)
    prv = pltpu.roll(x, 1, axis=1)
    return x * cos + jnp.where((lane & 1) == 0, nxt, prv) * sin_signed


def _proj_in_kernel(x_ref, g_ref, b_ref, w_ref, qg_ref, kg_ref, cos_ref, sin_ref, hm_ref,
                    h_ref, q_ref, k_ref, v_ref, u_ref):
    i1 = ATT_WIDTH
    i2 = i1 + KV_WIDTH
    i3 = i2 + KV_WIDTH
    i4 = i3 + CONV_WIDTH
    hm = hm_ref[...]
    heads_per_tile = LANES // HEAD_DIM
    tm = x_ref.shape[0]
    for rows in [slice(r0, r0 + tm // IN_SPLIT) for r0 in range(0, tm, tm // IN_SPLIT)]:
        h = _layer_norm(x_ref[rows, :], g_ref[...], b_ref[...])
        h_ref[rows, :] = h
        z = jnp.dot(h.astype(BF16), w_ref[...], preferred_element_type=F32)
        cos = cos_ref[rows, :]
        sin = sin_ref[rows, :]
        for j in range(ATT_WIDTH // LANES):
            qj = z[:, j * LANES:(j + 1) * LANES]
            ms = jnp.dot((qj * qj).astype(BF16), hm, preferred_element_type=F32)
            qn = qj * lax.rsqrt(ms + RMS_EPS) * qg_ref[...]
            qr = (_rope(qn, cos, sin) * Q_SCALE).astype(BF16)
            for r in range(heads_per_tile):
                q_ref[j * heads_per_tile + r, rows, :] = qr[:, r * HEAD_DIM:(r + 1) * HEAD_DIM]
        kj = z[:, i1:i2]
        ms = jnp.dot((kj * kj).astype(BF16), hm, preferred_element_type=F32)
        kn = kj * lax.rsqrt(ms + RMS_EPS) * kg_ref[...]
        kr = _rope(kn, cos, sin).astype(BF16)
        vj = z[:, i2:i3].astype(BF16)
        ones = jnp.ones((vj.shape[0], HEAD_DIM), BF16)
        for g in range(ATT_KV_HEADS):
            k_ref[g, rows, :] = kr[:, g * HEAD_DIM:(g + 1) * HEAD_DIM]
        v_ref[rows, :] = jnp.concatenate([vj[:, :HEAD_DIM], ones, vj[:, HEAD_DIM:], ones], axis=1)
        u_ref[rows, :] = (z[:, i3:i4] * _sigmoid(z[:, i4:])).astype(BF16)


def _proj_in(x, ln_g, ln_b, w_in, qg, kg, cos, sin, hm, seq):
    t = x.shape[0]
    tm = TM_IN
    nseq = seq // tm
    row = lambda i: (i, 0)
    const = lambda i: (0, 0)
    return pl.pallas_call(
        _proj_in_kernel,
        grid=(t // tm,),
        in_specs=[
            pl.BlockSpec((tm, D_MODEL), row),
            pl.BlockSpec((1, D_MODEL), const),
            pl.BlockSpec((1, D_MODEL), const),
            pl.BlockSpec((D_MODEL, IN_WIDTH), const),
            pl.BlockSpec((1, LANES), const),
            pl.BlockSpec((1, LANES), const),
            pl.BlockSpec((tm, LANES), lambda i: (i % nseq, 0)),
            pl.BlockSpec((tm, LANES), lambda i: (i % nseq, 0)),
            pl.BlockSpec((LANES, LANES), const),
        ],
        out_specs=[
            pl.BlockSpec((tm, D_MODEL), row),
            pl.BlockSpec((ATT_HEADS, tm, HEAD_DIM), lambda i: (0, i, 0)),
            pl.BlockSpec((ATT_KV_HEADS, tm, HEAD_DIM), lambda i: (0, i, 0)),
            pl.BlockSpec((tm, 2 * KV_WIDTH), row),
            pl.BlockSpec((tm, CONV_WIDTH), row),
        ],
        out_shape=[
            jax.ShapeDtypeStruct((t, D_MODEL), F32),
            jax.ShapeDtypeStruct((ATT_HEADS, t, HEAD_DIM), BF16),
            jax.ShapeDtypeStruct((ATT_KV_HEADS, t, HEAD_DIM), BF16),
            jax.ShapeDtypeStruct((t, 2 * KV_WIDTH), BF16),
            jax.ShapeDtypeStruct((t, CONV_WIDTH), BF16),
        ],
        compiler_params=pltpu.CompilerParams(dimension_semantics=("parallel",), vmem_limit_bytes=VMEM_LIMIT),
        name="proj_in",
    )(x, ln_g, ln_b, w_in, qg, kg, cos, sin, hm)


def _conv_kernel(u_ref, w_ref, cb_ref, g_ref, b_ref, o_ref, win_ref, sh_ref, *, seq):
    i = pl.program_id(1)
    tc = TC_CONV
    t0 = pl.multiple_of(i * tc, tc)
    top_start = pl.multiple_of(jnp.maximum(t0 - HALO, 0), HALO)
    bot_start = pl.multiple_of(jnp.minimum(t0 + tc, seq - HALO), HALO)
    top = u_ref[pl.ds(top_start, HALO), :].astype(F32)
    bot = u_ref[pl.ds(bot_start, HALO), :].astype(F32)
    win_ref[0:HALO, :] = jnp.where(i > 0, top, 0.0)
    win_ref[HALO:HALO + tc, :] = u_ref[pl.ds(t0, tc), :].astype(F32)
    win_ref[HALO + tc:, :] = jnp.where(i < pl.num_programs(1) - 1, bot, 0.0)
    sh_rows = sh_ref.shape[1]
    for s in range(1, SUBLANES):
        sh_ref[s] = win_ref[s:s + sh_rows, :]
    w = w_ref[...]
    cols = []
    for c in range(CONV_WIDTH // LANES):
        cs = slice(c * LANES, (c + 1) * LANES)
        acc = jnp.zeros((tc, LANES), F32)
        for j in range(CONV_KSIZE):
            off = HALO - CONV_PAD + j
            base, s = off - off % SUBLANES, off % SUBLANES
            tap = win_ref[base:base + tc, cs] if s == 0 else sh_ref[s, base:base + tc, cs]
            acc = acc + tap * w[j:j + 1, cs]
        cols.append(acc)
    y = jnp.concatenate(cols, axis=1) + cb_ref[...]
    y = _layer_norm(y, g_ref[...], b_ref[...])
    o_ref[...] = (y * _sigmoid(y)).astype(BF16)


def _conv_branch(u, conv_w, conv_b, cln_g, cln_b):
    b, seq, _ = u.shape
    tc = TC_CONV
    const = lambda bi, i: (0, 0)
    return pl.pallas_call(
        functools.partial(_conv_kernel, seq=seq),
        grid=(b, seq // tc),
        in_specs=[
            pl.BlockSpec((None, seq, CONV_WIDTH), lambda bi, i: (bi, 0, 0)),
            pl.BlockSpec((CONV_KSIZE, CONV_WIDTH), const),
            pl.BlockSpec((1, CONV_WIDTH), const),
            pl.BlockSpec((1, CONV_WIDTH), const),
            pl.BlockSpec((1, CONV_WIDTH), const),
        ],
        out_specs=pl.BlockSpec((None, tc, CONV_WIDTH), lambda bi, i: (bi, i, 0)),
        out_shape=jax.ShapeDtypeStruct((b, seq, CONV_WIDTH), BF16),
        scratch_shapes=[pltpu.VMEM((tc + 2 * HALO, CONV_WIDTH), F32),
                        pltpu.VMEM((SUBLANES, tc + 2 * HALO - SUBLANES, CONV_WIDTH), F32)],
        compiler_params=pltpu.CompilerParams(dimension_semantics=("parallel", "arbitrary"),
                                             vmem_limit_bytes=VMEM_LIMIT),
        name="conv_branch",
    )(u, conv_w, conv_b, cln_g, cln_b)


def _gqa_kernel(q_ref, k_ref, v_ref, o_ref, kmax_ref):
    rep = ATT_HEADS // ATT_KV_HEADS
    tq = q_ref.shape[1]
    seq = k_ref.shape[1]
    n_chunks = seq // KV_CHUNK
    nt_dims = (((1,), (1,)), ((), ()))

    @pl.when(pl.program_id(1) == 0)
    def _():
        for g in range(ATT_KV_HEADS):
            kf = k_ref[g].astype(F32)
            kmax_ref[g] = jnp.sqrt(jnp.max(jnp.sum(kf * kf, -1, keepdims=True), axis=0, keepdims=True))

    qs = [q_ref[g * rep:(g + 1) * rep].reshape(rep * tq, HEAD_DIM) for g in range(ATT_KV_HEADS)]
    bound = []
    for g in range(ATT_KV_HEADS):
        qf = qs[g].astype(F32)
        bound.append(jnp.sqrt(jnp.sum(qf * qf, -1, keepdims=True)) * kmax_ref[g])
    worst = jnp.maximum(jnp.max(bound[0]), jnp.max(bound[1]))

    def finish(acc):
        outs = []
        for g in range(ATT_KV_HEADS):
            o = acc[g] / pltpu.roll(acc[g], HEAD_DIM, axis=1)
            outs.extend(o[r * tq:(r + 1) * tq, :HEAD_DIM] for r in range(rep))
        o_ref[...] = jnp.concatenate(outs, axis=1).astype(BF16)

    @pl.when(worst <= ATT_SAFE_SPAN)
    def _():
        acc = [None] * ATT_KV_HEADS
        for c in range(n_chunks):
            rows = slice(c * KV_CHUNK, (c + 1) * KV_CHUNK)
            for g in range(ATT_KV_HEADS):
                s = lax.dot_general(qs[g], k_ref[g, rows], nt_dims, preferred_element_type=F32)
                p = jnp.exp2(s - bound[g]).astype(BF16)
                d = jnp.dot(p, v_ref[rows, g * LANES:(g + 1) * LANES], preferred_element_type=F32)
                acc[g] = d if c == 0 else acc[g] + d
        finish(acc)

    @pl.when(worst > ATT_SAFE_SPAN)
    def _():
        def step(c, carry):
            rows = pl.ds(pl.multiple_of(c * KV_CHUNK, KV_CHUNK), KV_CHUNK)
            new = []
            for g in range(ATT_KV_HEADS):
                m, acc = carry[g]
                s = lax.dot_general(qs[g], k_ref[g, rows], nt_dims, preferred_element_type=F32)
                m_new = jnp.maximum(m, jnp.max(s, -1, keepdims=True))
                p = jnp.exp2(s - m_new).astype(BF16)
                d = jnp.dot(p, v_ref[rows, g * LANES:(g + 1) * LANES], preferred_element_type=F32)
                new.append((m_new, jnp.exp2(m - m_new) * acc + d))
            return tuple(new)

        init = tuple((jnp.full((rep * tq, 1), -jnp.inf, F32), jnp.zeros((rep * tq, LANES), F32))
                     for _ in range(ATT_KV_HEADS))
        out = lax.fori_loop(0, n_chunks, step, init)
        finish([out[g][1] for g in range(ATT_KV_HEADS)])


def _gqa(q, k, v):
    _, b, seq, _ = q.shape
    tq = TQ_ATT
    return pl.pallas_call(
        _gqa_kernel,
        grid=(b, seq // tq),
        in_specs=[
            pl.BlockSpec((ATT_HEADS, None, tq, HEAD_DIM), lambda bi, i: (0, bi, i, 0)),
            pl.BlockSpec((ATT_KV_HEADS, None, seq, HEAD_DIM), lambda bi, i: (0, bi, 0, 0)),
            pl.BlockSpec((None, seq, ATT_KV_HEADS * LANES), lambda bi, i: (bi, 0, 0)),
        ],
        out_specs=pl.BlockSpec((None, tq, ATT_WIDTH), lambda bi, i: (bi, i, 0)),
        out_shape=jax.ShapeDtypeStruct((b, seq, ATT_WIDTH), BF16),
        scratch_shapes=[pltpu.VMEM((ATT_KV_HEADS, 1, 1), F32)],
        compiler_params=pltpu.CompilerParams(dimension_semantics=("arbitrary", "arbitrary"),
                                             vmem_limit_bytes=VMEM_LIMIT),
        name="gqa",
    )(q, k, v)


def _mem_kv_kernel(m_ref, g_ref, b_ref, wk_ref, wv_ref, k_ref, v_ref):
    m = _layer_norm(m_ref[...], g_ref[...], b_ref[...]).astype(BF16)
    k_ref[...] = jnp.dot(m, wk_ref[...], preferred_element_type=F32).astype(BF16)
    v_ref[...] = jnp.dot(m, wv_ref[...], preferred_element_type=F32).astype(BF16)


def _mem_kv(mem, ln_g, ln_b, w_ck, w_cv):
    b = mem.shape[0]
    const = lambda bi: (0, 0)
    blk = pl.BlockSpec((None, N_MEM, D_MODEL), lambda bi: (bi, 0, 0))
    return pl.pallas_call(
        _mem_kv_kernel,
        grid=(b,),
        in_specs=[blk, pl.BlockSpec((1, D_MODEL), const), pl.BlockSpec((1, D_MODEL), const),
                  pl.BlockSpec((D_MODEL, D_MODEL), const), pl.BlockSpec((D_MODEL, D_MODEL), const)],
        out_specs=[blk, blk],
        out_shape=[jax.ShapeDtypeStruct((b, N_MEM, D_MODEL), BF16)] * 2,
        compiler_params=pltpu.CompilerParams(dimension_semantics=("parallel",), vmem_limit_bytes=VMEM_LIMIT),
        name="mem_kv",
    )(mem, ln_g, ln_b, w_ck, w_cv)


def _post_kernel(att_ref, c_ref, h0_ref, km_ref, vm_ref, wo_ref, g1_ref, b1_ref, wq_ref, wco_ref,
                 g2_ref, b2_ref, wr_hi_ref, wr_lo_ref, br_ref, tri_ref, h2_ref, t_ref, route_ref, route_t_ref,
                 cnt_out_ref, cnt_ref):
    tm = att_ref.shape[0]
    halves = [slice(r0, r0 + tm // POST_SPLIT) for r0 in range(0, tm, tm // POST_SPLIT)]
    mixed = [jnp.concatenate([att_ref[r, :], c_ref[r, :]], axis=1) for r in halves]
    mix = [jnp.dot(m, wo_ref[...], preferred_element_type=F32) for m in mixed]
    h1 = [_layer_norm(DEEPNORM_ALPHA * h0_ref[r, :] + m, g1_ref[...], b1_ref[...]) for r, m in zip(halves, mix)]
    qx = [(jnp.dot(h.astype(BF16), wq_ref[...], preferred_element_type=F32) * (MEM_HEAD_DIM ** -0.5)).astype(BF16)
          for h in h1]
    heads = [[] for _ in halves]
    for hd in range(MEM_HEADS):
        hs = slice(hd * MEM_HEAD_DIM, (hd + 1) * MEM_HEAD_DIM)
        for k in range(POST_SPLIT):
            s = lax.dot_general(qx[k][:, hs], km_ref[:, hs], (((1,), (1,)), ((), ())), preferred_element_type=F32)
            p = jnp.exp(s - jnp.max(s, -1, keepdims=True))
            l = jnp.sum(p, -1, keepdims=True)
            heads[k].append(jnp.dot(p.astype(BF16), vm_ref[:, hs], preferred_element_type=F32) / l)
    o = [jnp.concatenate(hk, axis=1).astype(BF16) for hk in heads]
    xa = [jnp.dot(ok, wco_ref[...], preferred_element_type=F32) for ok in o]
    h2 = jnp.concatenate([_layer_norm(DEEPNORM_ALPHA * h + x, g2_ref[...], b2_ref[...]) for h, x in zip(h1, xa)],
                         axis=0)
    h2_ref[...] = h2
    t_hi = h2.astype(BF16)
    t_rounded = t_hi.astype(F32)
    tw = _pack_bf16_pairs(t_rounded)
    for half in range(2):
        t_ref[half] = tw[:, half * ROW_WORDS:(half + 1) * ROW_WORDS]
    t_lo = (h2 - t_rounded).astype(BF16)
    logits = (jnp.dot(t_hi, wr_hi_ref[...], preferred_element_type=F32)
              + jnp.dot(t_lo, wr_hi_ref[...], preferred_element_type=F32)
              + jnp.dot(t_hi, wr_lo_ref[...], preferred_element_type=F32)) + br_ref[...]
    lane = lax.broadcasted_iota(jnp.int32, logits.shape, 1)
    neg = -jnp.inf
    big = jnp.int32(LANES)
    gl = jnp.where(lane < N_GROUPS, logits, neg)
    gmax = jnp.max(gl, -1, keepdims=True)
    p_grp = 1.0 / jnp.sum(jnp.exp(gl - gmax), -1, keepdims=True)
    grp = jnp.min(jnp.where(gl == gmax, lane, big), -1, keepdims=True)
    lo = N_GROUPS + EXPERTS_PER_GROUP * grp
    el = jnp.where((lane >= lo) & (lane < lo + EXPERTS_PER_GROUP), logits, neg)
    v1 = jnp.max(el, -1, keepdims=True)
    i1 = jnp.min(jnp.where(el == v1, lane, big), -1, keepdims=True)
    el2 = jnp.where(lane == i1, neg, el)
    v2 = jnp.max(el2, -1, keepdims=True)
    i2 = jnp.min(jnp.where(el2 == v2, lane, big), -1, keepdims=True)
    e21 = jnp.exp(v2 - v1)
    w1 = p_grp / (1.0 + e21)
    w2 = p_grp * e21 / (1.0 + e21)
    @pl.when((pl.program_id(0) == 0) & (pl.program_id(1) == 0))
    def _():
        cnt_ref[...] = jnp.zeros_like(cnt_ref)

    e1 = i1 - N_GROUPS
    e2 = i2 - N_GROUPS
    hot1 = jnp.where(lane == e1, 1.0, 0.0)
    hot2 = jnp.where(lane == e2, 1.0, 0.0)
    both = hot1 + hot2
    before = jnp.dot(tri_ref[...], both.astype(BF16), preferred_element_type=F32) + cnt_ref[...]
    rank1 = jnp.sum(hot1 * before, -1, keepdims=True)
    rank2 = jnp.sum(hot2 * before, -1, keepdims=True)
    cnt_ref[...] += jnp.sum(both, 0, keepdims=True)
    cnt_out_ref[...] = cnt_ref[...]
    route = jnp.zeros(logits.shape, F32)
    for k, val in enumerate((e1.astype(F32), e2.astype(F32), w1, w2, rank1, rank2)):
        route = jnp.where(lane == k, val, route)
    route_ref[...] = route
    route_t_ref[...] = jnp.transpose(route)[:ROUTE_LANES, :]


def _post(att, c, h0, km, vm, w_out, g1, b1, w_cq, w_co, g2, b2, wr_hi, wr_lo, br):
    b, seq, _ = att.shape
    tm = TM_POST
    tok = lambda width: pl.BlockSpec((None, tm, width), lambda bi, i: (bi, i, 0))
    memb = pl.BlockSpec((None, N_MEM, D_MODEL), lambda bi, i: (bi, 0, 0))
    const = lambda shape: pl.BlockSpec(shape, lambda bi, i: (0, 0))
    vec = const((1, D_MODEL))
    sq = const((D_MODEL, D_MODEL))
    idx = jnp.arange(tm, dtype=jnp.int32)
    tri = jnp.where(idx[:, None] > idx[None, :], 1.0, 0.0).astype(BF16)
    return pl.pallas_call(
        _post_kernel,
        grid=(b, seq // tm),
        in_specs=[tok(ATT_WIDTH), tok(CONV_WIDTH), tok(D_MODEL), memb, memb, sq, vec, vec, sq, sq, vec, vec,
                  const((D_MODEL, LANES)), const((D_MODEL, LANES)), const((1, LANES)), const((tm, tm))],
        out_specs=[tok(D_MODEL),
                   pl.BlockSpec((2, None, tm, ROW_WORDS), lambda bi, i: (0, bi, i, 0)),
                   tok(LANES),
                   pl.BlockSpec((ROUTE_LANES, tm), lambda bi, i: (0, bi * (seq // tm) + i)),
                   const((1, LANES))],
        out_shape=[jax.ShapeDtypeStruct((b, seq, D_MODEL), F32),
                   jax.ShapeDtypeStruct((2, b, seq, ROW_WORDS), jnp.uint32),
                   jax.ShapeDtypeStruct((b, seq, LANES), F32),
                   jax.ShapeDtypeStruct((ROUTE_LANES, b * seq), F32),
                   jax.ShapeDtypeStruct((1, LANES), F32)],
        scratch_shapes=[pltpu.VMEM((1, LANES), F32)],
        compiler_params=pltpu.CompilerParams(dimension_semantics=("arbitrary", "arbitrary"),
                                             vmem_limit_bytes=VMEM_LIMIT),
        name="post",
    )(att, c, h0, km, vm, w_out, g1, b1, w_cq, w_co, g2, b2, wr_hi, wr_lo, br, tri)


def _pack_bf16_pairs(x):
    k = x.shape[1] // 2
    bits = lax.bitcast_convert_type(x, jnp.uint32)
    return (bits[:, :k] >> 16) | (bits[:, k:] & jnp.uint32(0xFFFF0000))


def _unpack_bf16_pairs(w):
    lo = lax.bitcast_convert_type(w << 16, F32)
    hi = lax.bitcast_convert_type(w & jnp.uint32(0xFFFF0000), F32)
    return jnp.concatenate([lo, hi], axis=1)


def _sc_mesh():
    return plsc.VectorSubcoreMesh(core_axis_name="c", subcore_axis_name="s")


def _sc_scatter2(x, idx_a, idx_b, n_out):
    m = x.shape[0]

    @pl.kernel(out_type=jax.ShapeDtypeStruct((n_out, ROW_WORDS), x.dtype), mesh=_sc_mesh(), scratch_types=[])
    def scatter(x_hbm, ia_hbm, ib_hbm, o_hbm):
        def body(x_vmem, ia_vmem, ib_vmem):
            pltpu.sync_copy(x_vmem, o_hbm.at[ia_vmem.at[0]])
            pltpu.sync_copy(x_vmem, o_hbm.at[ib_vmem.at[0]])

        pltpu.emit_pipeline(
            body, grid=(m // SC_WINDOW,),
            in_specs=[pl.BlockSpec((SC_WINDOW, ROW_WORDS), lambda i: (i, 0)),
                      pl.BlockSpec((1, SC_WINDOW), lambda i: (0, i)),
                      pl.BlockSpec((1, SC_WINDOW), lambda i: (0, i))],
            out_specs=[],
            core_axis_name=("c", "s"), dimension_semantics=(pltpu.PARALLEL,),
        )(x_hbm, ia_hbm, ib_hbm)

    return scatter(x, idx_a.reshape(1, m), idx_b.reshape(1, m))


def _sc_gather(x, idx):
    m = idx.shape[0]

    @pl.kernel(out_type=jax.ShapeDtypeStruct((m, ROW_WORDS), x.dtype), mesh=_sc_mesh(), scratch_types=[])
    def gather(x_hbm, i_hbm, o_hbm):
        def body(i_vmem, o_vmem):
            pltpu.sync_copy(x_hbm.at[i_vmem.at[0]], o_vmem)

        pltpu.emit_pipeline(
            body, grid=(m // SC_WINDOW,),
            in_specs=[pl.BlockSpec((1, SC_WINDOW), lambda i: (0, i))],
            out_specs=[pl.BlockSpec((SC_WINDOW, ROW_WORDS), lambda i: (i, 0))],
            core_axis_name=("c", "s"), dimension_semantics=(pltpu.PARALLEL,),
        )(i_hbm, o_hbm)

    return gather(x, idx.reshape(1, m))


def _experts_kernel(tile_expert_ref, n_used_ref, x_ref, wg32_ref, wu32_ref, wd32_ref, y_ref, wg_ref, wu_ref, wd_ref):
    j = pl.program_id(0)
    prev = tile_expert_ref[jnp.maximum(j - 1, 0)]

    @pl.when((j == 0) | (tile_expert_ref[j] != prev))
    def _():
        wg_ref[...] = wg32_ref[...].astype(BF16)
        wu_ref[...] = wu32_ref[...].astype(BF16)
        wd_ref[...] = wd32_ref[...].astype(BF16)

    @pl.when(j < n_used_ref[0])
    def _():
        u0 = _unpack_bf16_pairs(x_ref[0])
        u1 = _unpack_bf16_pairs(x_ref[1])
        x = jnp.concatenate([u0[:, :ROW_WORDS], u1[:, :ROW_WORDS], u0[:, ROW_WORDS:], u1[:, ROW_WORDS:]],
                            axis=1).astype(BF16)
        gate = jnp.dot(x, wg_ref[...], preferred_element_type=F32)
        up = jnp.dot(x, wu_ref[...], preferred_element_type=F32)
        he = (gate * _sigmoid(gate) * up).astype(BF16)
        y = jnp.dot(he, wd_ref[...], preferred_element_type=F32)
        yw = _pack_bf16_pairs(y.astype(BF16).astype(F32))
        for half in range(2):
            y_ref[half] = yw[:, half * ROW_WORDS:(half + 1) * ROW_WORDS]


def _experts(xs, tile_expert, n_used, w_gate, w_up, w_down):
    _, rows, _ = xs.shape
    tr = TR_EXPERT
    halves = pl.BlockSpec((2, tr, ROW_WORDS), lambda j, te, nu: (0, jnp.minimum(j, nu[0] - 1), 0))
    return pl.pallas_call(
        _experts_kernel,
        grid_spec=pltpu.PrefetchScalarGridSpec(
            num_scalar_prefetch=2,
            grid=(rows // tr,),
            in_specs=[halves,
                      pl.BlockSpec((None, D_MODEL, D_EXPERT), lambda j, te, nu: (te[j], 0, 0)),
                      pl.BlockSpec((None, D_MODEL, D_EXPERT), lambda j, te, nu: (te[j], 0, 0)),
                      pl.BlockSpec((None, D_EXPERT, D_MODEL), lambda j, te, nu: (te[j], 0, 0))],
            out_specs=halves,
            scratch_shapes=[pltpu.VMEM((D_MODEL, D_EXPERT), BF16), pltpu.VMEM((D_MODEL, D_EXPERT), BF16),
                            pltpu.VMEM((D_EXPERT, D_MODEL), BF16)],
        ),
        out_shape=jax.ShapeDtypeStruct(xs.shape, jnp.uint32),
        compiler_params=pltpu.CompilerParams(dimension_semantics=("arbitrary",), vmem_limit_bytes=VMEM_LIMIT),
        name="experts",
    )(tile_expert, n_used, xs, w_gate, w_up, w_down)


def _combine_kernel(h2_ref, y1_ref, y2_ref, route_ref, g3_ref, b3_ref, o_ref):
    route = route_ref[...]
    w1 = route[:, ROUTE_W:ROUTE_W + 1]
    w2 = route[:, ROUTE_W + 1:ROUTE_W + 2]
    halves = []
    for half in range(2):
        halves.append(w1 * _unpack_bf16_pairs(y1_ref[half]) + w2 * _unpack_bf16_pairs(y2_ref[half]))
    ff = jnp.concatenate([halves[0][:, :ROW_WORDS], halves[1][:, :ROW_WORDS],
                          halves[0][:, ROW_WORDS:], halves[1][:, ROW_WORDS:]], axis=1)
    o_ref[...] = _layer_norm(DEEPNORM_ALPHA * h2_ref[...] + ff, g3_ref[...], b3_ref[...])


def _combine(h2, y1, y2, route, g3, b3):
    n = h2.shape[0]
    tm = TM_COMBINE
    row = lambda width: pl.BlockSpec((tm, width), lambda i: (i, 0))
    halves = pl.BlockSpec((2, tm, ROW_WORDS), lambda i: (0, i, 0))
    vec = pl.BlockSpec((1, D_MODEL), lambda i: (0, 0))
    return pl.pallas_call(
        _combine_kernel,
        grid=(n // tm,),
        in_specs=[row(D_MODEL), halves, halves, row(LANES), vec, vec],
        out_specs=row(D_MODEL),
        out_shape=jax.ShapeDtypeStruct((n, D_MODEL), F32),
        compiler_params=pltpu.CompilerParams(dimension_semantics=("parallel",), vmem_limit_bytes=VMEM_LIMIT),
        name="combine",
    )(h2, y1, y2, route, g3, b3)


def _moe(tw, route, route_t, counts, h2, w_gate, w_up, w_down, g3, b3):
    n = h2.shape[0]
    tr = TR_EXPERT
    cap = 2 * n + N_EXPERTS * tr
    cnt = counts[0, :N_EXPERTS].astype(jnp.int32)
    seg = (cnt + tr - 1) // tr * tr
    ends = jnp.cumsum(seg)
    starts = ends - seg
    tile_start = jnp.arange(cap // tr, dtype=jnp.int32) * tr
    tile_expert = jnp.minimum(jnp.sum(tile_start[:, None] >= ends[None, :], axis=1, dtype=jnp.int32), N_EXPERTS - 1)
    n_used = (ends[-1] // tr).reshape(1)

    def half_row_index(k):
        expert = route_t[ROUTE_E + k].astype(jnp.int32)
        hot = expert[None, :] == jnp.arange(N_EXPERTS, dtype=jnp.int32)[:, None]
        pos = route_t[ROUTE_RANK + k].astype(jnp.int32) + jnp.sum(jnp.where(hot, starts[:, None], 0), axis=0)
        return jnp.concatenate([pos, cap + pos])

    idx1, idx2 = half_row_index(0), half_row_index(1)
    xs = _sc_scatter2(tw.reshape(2 * n, ROW_WORDS), idx1, idx2, 2 * cap)
    ys = _experts(xs.reshape(2, cap, ROW_WORDS), tile_expert, n_used, w_gate, w_up, w_down)
    ys = ys.reshape(2 * cap, ROW_WORDS)
    y1 = _sc_gather(ys, idx1).reshape(2, n, ROW_WORDS)
    y2 = _sc_gather(ys, idx2).reshape(2, n, ROW_WORDS)
    return _combine(h2, y1, y2, route, g3, b3)


def _rope_tables(seq):
    rows = seq // GRID_W
    row = jnp.repeat(jnp.arange(rows, dtype=F32), GRID_W)
    col = jnp.tile(jnp.arange(GRID_W, dtype=F32), rows)
    inv_freq = ROPE_THETA ** (-jnp.arange(0, ROPE_AXIS_DIM, 2, dtype=F32) / ROPE_AXIS_DIM)
    ang = jnp.concatenate([row[:, None] * inv_freq, col[:, None] * inv_freq], -1)
    cos = jnp.repeat(jnp.cos(ang), 2, axis=-1)
    sin = jnp.repeat(jnp.sin(ang), 2, axis=-1)
    sign = jnp.tile(jnp.array([-1.0, 1.0], F32), HEAD_DIM // 2)
    sin = sin * sign
    return jnp.tile(cos, (1, LANES // HEAD_DIM)), jnp.tile(sin, (1, LANES // HEAD_DIM))


def _encoder(x, mem, p):
    b, seq, d = x.shape
    cos, sin = _rope_tables(seq)
    h0, q, k, v, u = _proj_in(x.reshape(b * seq, d), p["ln_in_g"], p["ln_in_b"], p["w_in"], p["qg"], p["kg"],
                              cos, sin, p["head_mean"], seq)
    c = _conv_branch(u.reshape(b, seq, CONV_WIDTH), p["conv_w"], p["conv_b"], p["cln_g"], p["cln_b"])
    att = _gqa(q.reshape(ATT_HEADS, b, seq, HEAD_DIM), k.reshape(ATT_KV_HEADS, b, seq, HEAD_DIM),
               v.reshape(b, seq, 2 * KV_WIDTH))
    km, vm = _mem_kv(mem, p["ln_mem_g"], p["ln_mem_b"], p["w_ck"], p["w_cv"])
    h2, tw, route, route_t, counts = _post(att, c, h0.reshape(b, seq, d), km, vm, p["w_out"], p["ln1_g"],
                                           p["ln1_b"], p["w_cq"], p["w_co"], p["ln2_g"], p["ln2_b"],
                                           p["wr_hi"], p["wr_lo"], p["br"])
    y = _moe(tw.reshape(2, b * seq, ROW_WORDS), route.reshape(b * seq, LANES), route_t, counts,
             h2.reshape(b * seq, d), p["w_gate"], p["w_up"], p["w_down"], p["ln3_g"], p["ln3_b"])
    return y.reshape(b, seq, d)


def kernel(x_prompt, x_sample, mem_prompt, mem_sample, ln_in_g, ln_in_b, ln_mem_g, ln_mem_b, w_in, q_norm_g, k_norm_g, conv_w, conv_b, conv_ln_g, conv_ln_b, w_mix_out, ln1_g, ln1_b, w_cq, w_ck, w_cv, w_co, ln2_g, ln2_b, w_router_g, b_router_g, w_router_e, b_router_e, w_e_gate, w_e_up, w_e_down, ln3_g, ln3_b):
    l = 0
    vec = lambda a: a.reshape(1, -1).astype(F32)
    w_r = jnp.concatenate([w_router_g[l], jnp.transpose(w_router_e[l], (1, 0, 2)).reshape(D_MODEL, N_EXPERTS)], axis=1)
    w_r = jnp.pad(w_r, ((0, 0), (0, LANES - w_r.shape[1])))
    wr_hi = w_r.astype(BF16)
    wr_lo = (w_r - wr_hi.astype(F32)).astype(BF16)
    b_r = jnp.concatenate([b_router_g[l], b_router_e[l].reshape(-1)])
    b_r = jnp.pad(b_r, (0, LANES - b_r.shape[0])).reshape(1, LANES)
    head = jnp.arange(LANES) // HEAD_DIM
    head_mean = jnp.where(head[:, None] == head[None, :], 1.0 / HEAD_DIM, 0.0).astype(BF16)
    p = {
        "ln_in_g": vec(ln_in_g), "ln_in_b": vec(ln_in_b), "ln_mem_g": vec(ln_mem_g), "ln_mem_b": vec(ln_mem_b),
        "w_in": w_in[l].astype(BF16),
        "qg": jnp.tile(q_norm_g[l], LANES // HEAD_DIM).reshape(1, LANES),
        "kg": jnp.tile(k_norm_g[l], LANES // HEAD_DIM).reshape(1, LANES),
        "head_mean": head_mean,
        "conv_w": conv_w[l], "conv_b": vec(conv_b[l]), "cln_g": vec(conv_ln_g[l]), "cln_b": vec(conv_ln_b[l]),
        "w_out": w_mix_out[l].astype(BF16), "ln1_g": vec(ln1_g[l]), "ln1_b": vec(ln1_b[l]),
        "w_cq": w_cq[l].astype(BF16), "w_ck": w_ck[l].astype(BF16), "w_cv": w_cv[l].astype(BF16),
        "w_co": w_co[l].astype(BF16), "ln2_g": vec(ln2_g[l]), "ln2_b": vec(ln2_b[l]),
        "wr_hi": wr_hi, "wr_lo": wr_lo, "br": b_r,
        "w_gate": w_e_gate[l], "w_up": w_e_up[l], "w_down": w_e_down[l],
        "ln3_g": vec(ln3_g[l]), "ln3_b": vec(ln3_b[l]),
    }
    return (_encoder(x_prompt, mem_prompt, p), _encoder(x_sample, mem_sample, p))
```

```python
import functools

import jax
import jax.numpy as jnp
from jax import lax
from jax.experimental import pallas as pl
from jax.experimental.pallas import tpu as pltpu
from jax.experimental.pallas import tpu_sc as plsc

F32 = jnp.float32
BF16 = jnp.bfloat16

D_MODEL = 1024
DEPTH = 1
GRID_W = 64
N_MEM = 256
ATT_HEADS = 8
ATT_KV_HEADS = 2
HEAD_DIM = 64
ATT_WIDTH = ATT_HEADS * HEAD_DIM
KV_WIDTH = ATT_KV_HEADS * HEAD_DIM
ROPE_THETA = 10000.0
ROPE_AXIS_DIM = HEAD_DIM // 2
CONV_WIDTH = D_MODEL - ATT_WIDTH
CONV_KSIZE = 31
CONV_PAD = CONV_KSIZE // 2
IN_WIDTH = ATT_WIDTH + 2 * KV_WIDTH + 2 * CONV_WIDTH
MEM_HEADS = 4
MEM_HEAD_DIM = D_MODEL // MEM_HEADS
N_GROUPS = 4
EXPERTS_PER_GROUP = 8
N_EXPERTS = N_GROUPS * EXPERTS_PER_GROUP
D_EXPERT = 256
LN_EPS = 1e-5
RMS_EPS = 1e-6
DEEPNORM_ALPHA = (2.0 * DEPTH) ** 0.25
LOG2_E = 1.4426950408889634
Q_SCALE = HEAD_DIM ** -0.5 * LOG2_E

LANES = 128
SUBLANES = 8
HALO = 16
VMEM_LIMIT = 56 * 1024 * 1024

TM_IN = 1024
IN_SPLIT = 4
TC_CONV = 256
TQ_ATT = 512
KV_CHUNK = 512
ATT_SAFE_SPAN = 50.0
TM_POST = 1024
POST_SPLIT = 4
TRI_BLOCKS = 4
TR_EXPERT = 512
TM_COMBINE = 1024
ROW_WORDS = 256
SC_WINDOW = 128
ROUTE_E, ROUTE_W, ROUTE_RANK = 0, 2, 4
ROUTE_LANES = 8


def _layer_norm(x, g, b):
    mu = jnp.mean(x, -1, keepdims=True)
    xc = x - mu
    var = jnp.mean(xc * xc, -1, keepdims=True)
    return xc * lax.rsqrt(var + LN_EPS) * g + b


def _sigmoid(x):
    return 1.0 / (1.0 + jnp.exp(-x))


def _rope(x, cos, sin_signed):
    lane = lax.broadcasted_iota(jnp.int32, x.shape, 1)
    nxt = pltpu.roll(x, LANES - 1, axis=1)
    prv = pltpu.roll(x, 1, axis=1)
    return x * cos + jnp.where((lane & 1) == 0, nxt, prv) * sin_signed


def _proj_in_kernel(x_ref, g_ref, b_ref, w_ref, qg_ref, kg_ref, cos_ref, sin_ref, hm_ref,
                    h_ref, q_ref, k_ref, v_ref, u_ref):
    i1 = ATT_WIDTH
    i2 = i1 + KV_WIDTH
    i3 = i2 + KV_WIDTH
    i4 = i3 + CONV_WIDTH
    hm = hm_ref[...]
    heads_per_tile = LANES // HEAD_DIM
    tm = x_ref.shape[0]
    for rows in [slice(r0, r0 + tm // IN_SPLIT) for r0 in range(0, tm, tm // IN_SPLIT)]:
        h = _layer_norm(x_ref[rows, :], g_ref[...], b_ref[...])
        h_ref[rows, :] = h
        z = jnp.dot(h.astype(BF16), w_ref[...], preferred_element_type=F32)
        cos = cos_ref[rows, :]
        sin = sin_ref[rows, :]
        for j in range(ATT_WIDTH // LANES):
            qj = z[:, j * LANES:(j + 1) * LANES]
            ms = jnp.dot((qj * qj).astype(BF16), hm, preferred_element_type=F32)
            qn = qj * lax.rsqrt(ms + RMS_EPS) * qg_ref[...]
            qr = (_rope(qn, cos, sin) * Q_SCALE).astype(BF16)
            for r in range(heads_per_tile):
                q_ref[j * heads_per_tile + r, rows, :] = qr[:, r * HEAD_DIM:(r + 1) * HEAD_DIM]
        kj = z[:, i1:i2]
        ms = jnp.dot((kj * kj).astype(BF16), hm, preferred_element_type=F32)
        kn = kj * lax.rsqrt(ms + RMS_EPS) * kg_ref[...]
        kr = _rope(kn, cos, sin).astype(BF16)
        vj = z[:, i2:i3].astype(BF16)
        ones = jnp.ones((vj.shape[0], HEAD_DIM), BF16)
        for g in range(ATT_KV_HEADS):
            k_ref[g, rows, :] = kr[:, g * HEAD_DIM:(g + 1) * HEAD_DIM]
        v_ref[rows, :] = jnp.concatenate([vj[:, :HEAD_DIM], ones, vj[:, HEAD_DIM:], ones], axis=1)
        u_ref[rows, :] = (z[:, i3:i4] * _sigmoid(z[:, i4:])).astype(BF16)


def _proj_in(x, ln_g, ln_b, w_in, qg, kg, cos, sin, hm, seq):
    t = x.shape[0]
    tm = TM_IN
    nseq = seq // tm
    row = lambda i: (i, 0)
    const = lambda i: (0, 0)
    return pl.pallas_call(
        _proj_in_kernel,
        grid=(t // tm,),
        in_specs=[
            pl.BlockSpec((tm, D_MODEL), row),
            pl.BlockSpec((1, D_MODEL), const),
            pl.BlockSpec((1, D_MODEL), const),
            pl.BlockSpec((D_MODEL, IN_WIDTH), const),
            pl.BlockSpec((1, LANES), const),
            pl.BlockSpec((1, LANES), const),
            pl.BlockSpec((tm, LANES), lambda i: (i % nseq, 0)),
            pl.BlockSpec((tm, LANES), lambda i: (i % nseq, 0)),
            pl.BlockSpec((LANES, LANES), const),
        ],
        out_specs=[
            pl.BlockSpec((tm, D_MODEL), row),
            pl.BlockSpec((ATT_HEADS, tm, HEAD_DIM), lambda i: (0, i, 0)),
            pl.BlockSpec((ATT_KV_HEADS, tm, HEAD_DIM), lambda i: (0, i, 0)),
            pl.BlockSpec((tm, 2 * KV_WIDTH), row),
            pl.BlockSpec((tm, CONV_WIDTH), row),
        ],
        out_shape=[
            jax.ShapeDtypeStruct((t, D_MODEL), F32),
            jax.ShapeDtypeStruct((ATT_HEADS, t, HEAD_DIM), BF16),
            jax.ShapeDtypeStruct((ATT_KV_HEADS, t, HEAD_DIM), BF16),
            jax.ShapeDtypeStruct((t, 2 * KV_WIDTH), BF16),
            jax.ShapeDtypeStruct((t, CONV_WIDTH), BF16),
        ],
        compiler_params=pltpu.CompilerParams(dimension_semantics=("parallel",), vmem_limit_bytes=VMEM_LIMIT),
        name="proj_in",
    )(x, ln_g, ln_b, w_in, qg, kg, cos, sin, hm)


def _conv_kernel(u_ref, w_ref, cb_ref, g_ref, b_ref, o_ref, win_ref, sh_ref, *, seq):
    i = pl.program_id(1)
    tc = TC_CONV
    t0 = pl.multiple_of(i * tc, tc)
    top_start = pl.multiple_of(jnp.maximum(t0 - HALO, 0), HALO)
    bot_start = pl.multiple_of(jnp.minimum(t0 + tc, seq - HALO), HALO)
    top = u_ref[pl.ds(top_start, HALO), :].astype(F32)
    bot = u_ref[pl.ds(bot_start, HALO), :].astype(F32)
    win_ref[0:HALO, :] = jnp.where(i > 0, top, 0.0)
    win_ref[HALO:HALO + tc, :] = u_ref[pl.ds(t0, tc), :].astype(F32)
    win_ref[HALO + tc:, :] = jnp.where(i < pl.num_programs(1) - 1, bot, 0.0)
    sh_rows = sh_ref.shape[1]
    for s in range(1, SUBLANES):
        sh_ref[s] = win_ref[s:s + sh_rows, :]
    w = w_ref[...]
    cols = []
    for c in range(CONV_WIDTH // LANES):
        cs = slice(c * LANES, (c + 1) * LANES)
        acc = jnp.zeros((tc, LANES), F32)
        for j in range(CONV_KSIZE):
            off = HALO - CONV_PAD + j
            base, s = off - off % SUBLANES, off % SUBLANES
            tap = win_ref[base:base + tc, cs] if s == 0 else sh_ref[s, base:base + tc, cs]
            acc = acc + tap * w[j:j + 1, cs]
        cols.append(acc)
    y = jnp.concatenate(cols, axis=1) + cb_ref[...]
    y = _layer_norm(y, g_ref[...], b_ref[...])
    o_ref[...] = (y * _sigmoid(y)).astype(BF16)


def _conv_branch(u, conv_w, conv_b, cln_g, cln_b):
    b, seq, _ = u.shape
    tc = TC_CONV
    const = lambda bi, i: (0, 0)
    return pl.pallas_call(
        functools.partial(_conv_kernel, seq=seq),
        grid=(b, seq // tc),
        in_specs=[
            pl.BlockSpec((None, seq, CONV_WIDTH), lambda bi, i: (bi, 0, 0)),
            pl.BlockSpec((CONV_KSIZE, CONV_WIDTH), const),
            pl.BlockSpec((1, CONV_WIDTH), const),
            pl.BlockSpec((1, CONV_WIDTH), const),
            pl.BlockSpec((1, CONV_WIDTH), const),
        ],
        out_specs=pl.BlockSpec((None, tc, CONV_WIDTH), lambda bi, i: (bi, i, 0)),
        out_shape=jax.ShapeDtypeStruct((b, seq, CONV_WIDTH), BF16),
        scratch_shapes=[pltpu.VMEM((tc + 2 * HALO, CONV_WIDTH), F32),
                        pltpu.VMEM((SUBLANES, tc + 2 * HALO - SUBLANES, CONV_WIDTH), F32)],
        compiler_params=pltpu.CompilerParams(dimension_semantics=("parallel", "arbitrary"),
                                             vmem_limit_bytes=VMEM_LIMIT),
        name="conv_branch",
    )(u, conv_w, conv_b, cln_g, cln_b)


def _gqa_kernel(q_ref, k_ref, v_ref, o_ref, kmax_ref):
    rep = ATT_HEADS // ATT_KV_HEADS
    tq = q_ref.shape[1]
    seq = k_ref.shape[1]
    n_chunks = seq // KV_CHUNK
    nt_dims = (((1,), (1,)), ((), ()))

    @pl.when(pl.program_id(1) == 0)
    def _():
        for g in range(ATT_KV_HEADS):
            kf = k_ref[g].astype(F32)
            kmax_ref[g] = jnp.sqrt(jnp.max(jnp.sum(kf * kf, -1, keepdims=True), axis=0, keepdims=True))

    qs = [q_ref[g * rep:(g + 1) * rep].reshape(rep * tq, HEAD_DIM) for g in range(ATT_KV_HEADS)]
    bound = []
    for g in range(ATT_KV_HEADS):
        qf = qs[g].astype(F32)
        bound.append(jnp.sqrt(jnp.sum(qf * qf, -1, keepdims=True)) * kmax_ref[g])
    worst = jnp.maximum(jnp.max(bound[0]), jnp.max(bound[1]))

    def finish(acc):
        outs = []
        for g in range(ATT_KV_HEADS):
            o = acc[g] / pltpu.roll(acc[g], HEAD_DIM, axis=1)
            outs.extend(o[r * tq:(r + 1) * tq, :HEAD_DIM] for r in range(rep))
        o_ref[...] = jnp.concatenate(outs, axis=1).astype(BF16)

    @pl.when(worst <= ATT_SAFE_SPAN)
    def _():
        acc = [None] * ATT_KV_HEADS
        for c in range(n_chunks):
            rows = slice(c * KV_CHUNK, (c + 1) * KV_CHUNK)
            for g in range(ATT_KV_HEADS):
                s = lax.dot_general(qs[g], k_ref[g, rows], nt_dims, preferred_element_type=F32)
                p = jnp.exp2(s - bound[g]).astype(BF16)
                d = jnp.dot(p, v_ref[rows, g * LANES:(g + 1) * LANES], preferred_element_type=F32)
                acc[g] = d if c == 0 else acc[g] + d
        finish(acc)

    @pl.when(worst > ATT_SAFE_SPAN)
    def _():
        def step(c, carry):
            rows = pl.ds(pl.multiple_of(c * KV_CHUNK, KV_CHUNK), KV_CHUNK)
            new = []
            for g in range(ATT_KV_HEADS):
                m, acc = carry[g]
                s = lax.dot_general(qs[g], k_ref[g, rows], nt_dims, preferred_element_type=F32)
                m_new = jnp.maximum(m, jnp.max(s, -1, keepdims=True))
                p = jnp.exp2(s - m_new).astype(BF16)
                d = jnp.dot(p, v_ref[rows, g * LANES:(g + 1) * LANES], preferred_element_type=F32)
                new.append((m_new, jnp.exp2(m - m_new) * acc + d))
            return tuple(new)

        init = tuple((jnp.full((rep * tq, 1), -jnp.inf, F32), jnp.zeros((rep * tq, LANES), F32))
                     for _ in range(ATT_KV_HEADS))
        out = lax.fori_loop(0, n_chunks, step, init)
        finish([out[g][1] for g in range(ATT_KV_HEADS)])


def _gqa(q, k, v):
    _, b, seq, _ = q.shape
    tq = TQ_ATT
    return pl.pallas_call(
        _gqa_kernel,
        grid=(b, seq // tq),
        in_specs=[
            pl.BlockSpec((ATT_HEADS, None, tq, HEAD_DIM), lambda bi, i: (0, bi, i, 0)),
            pl.BlockSpec((ATT_KV_HEADS, None, seq, HEAD_DIM), lambda bi, i: (0, bi, 0, 0)),
            pl.BlockSpec((None, seq, ATT_KV_HEADS * LANES), lambda bi, i: (bi, 0, 0)),
        ],
        out_specs=pl.BlockSpec((None, tq, ATT_WIDTH), lambda bi, i: (bi, i, 0)),
        out_shape=jax.ShapeDtypeStruct((b, seq, ATT_WIDTH), BF16),
        scratch_shapes=[pltpu.VMEM((ATT_KV_HEADS, 1, 1), F32)],
        compiler_params=pltpu.CompilerParams(dimension_semantics=("arbitrary", "arbitrary"),
                                             vmem_limit_bytes=VMEM_LIMIT),
        name="gqa",
    )(q, k, v)


def _mem_kv_kernel(m_ref, g_ref, b_ref, wk_ref, wv_ref, k_ref, v_ref):
    m = _layer_norm(m_ref[...], g_ref[...], b_ref[...]).astype(BF16)
    k_ref[...] = jnp.dot(m, wk_ref[...], preferred_element_type=F32).astype(BF16)
    v_ref[...] = jnp.dot(m, wv_ref[...], preferred_element_type=F32).astype(BF16)


def _mem_kv(mem, ln_g, ln_b, w_ck, w_cv):
    b = mem.shape[0]
    const = lambda bi: (0, 0)
    blk = pl.BlockSpec((None, N_MEM, D_MODEL), lambda bi: (bi, 0, 0))
    return pl.pallas_call(
        _mem_kv_kernel,
        grid=(b,),
        in_specs=[blk, pl.BlockSpec((1, D_MODEL), const), pl.BlockSpec((1, D_MODEL), const),
                  pl.BlockSpec((D_MODEL, D_MODEL), const), pl.BlockSpec((D_MODEL, D_MODEL), const)],
        out_specs=[blk, blk],
        out_shape=[jax.ShapeDtypeStruct((b, N_MEM, D_MODEL), BF16)] * 2,
        compiler_params=pltpu.CompilerParams(dimension_semantics=("parallel",), vmem_limit_bytes=VMEM_LIMIT),
        name="mem_kv",
    )(mem, ln_g, ln_b, w_ck, w_cv)


def _post_kernel(att_ref, c_ref, h0_ref, km_ref, vm_ref, wo_ref, g1_ref, b1_ref, wq_ref, wco_ref,
                 g2_ref, b2_ref, wr_ref, br_ref, tri_ref, h2_ref, t_ref, route_ref, route_t_ref,
                 cnt_out_ref, cnt_ref):
    tm = att_ref.shape[0]
    halves = [slice(r0, r0 + tm // POST_SPLIT) for r0 in range(0, tm, tm // POST_SPLIT)]
    mixed = [jnp.concatenate([att_ref[r, :], c_ref[r, :]], axis=1) for r in halves]
    mix = [jnp.dot(m, wo_ref[...], preferred_element_type=F32) for m in mixed]
    h1 = [_layer_norm(DEEPNORM_ALPHA * h0_ref[r, :] + m, g1_ref[...], b1_ref[...]) for r, m in zip(halves, mix)]
    qx = [(jnp.dot(h.astype(BF16), wq_ref[...], preferred_element_type=F32) * (MEM_HEAD_DIM ** -0.5)).astype(BF16)
          for h in h1]
    heads = [[] for _ in halves]
    for hd in range(MEM_HEADS):
        hs = slice(hd * MEM_HEAD_DIM, (hd + 1) * MEM_HEAD_DIM)
        for k in range(POST_SPLIT):
            s = lax.dot_general(qx[k][:, hs], km_ref[:, hs], (((1,), (1,)), ((), ())), preferred_element_type=F32)
            p = jnp.exp(s - jnp.max(s, -1, keepdims=True))
            l = jnp.sum(p, -1, keepdims=True)
            heads[k].append(jnp.dot(p.astype(BF16), vm_ref[:, hs], preferred_element_type=F32) / l)
    o = [jnp.concatenate(hk, axis=1).astype(BF16) for hk in heads]
    xa = [jnp.dot(ok, wco_ref[...], preferred_element_type=F32) for ok in o]
    h2 = jnp.concatenate([_layer_norm(DEEPNORM_ALPHA * h + x, g2_ref[...], b2_ref[...]) for h, x in zip(h1, xa)],
                         axis=0)
    h2_ref[...] = h2
    t_hi = h2.astype(BF16)
    t_rounded = t_hi.astype(F32)
    tw = _pack_bf16_pairs(t_rounded)
    for half in range(2):
        t_ref[half] = tw[:, half * ROW_WORDS:(half + 1) * ROW_WORDS]
    t_lo = (h2 - t_rounded).astype(BF16)
    both_parts = jnp.dot(t_hi, wr_ref[...], preferred_element_type=F32)
    logits = (both_parts[:, :LANES] + both_parts[:, LANES:]
              + jnp.dot(t_lo, wr_ref[:, :LANES], preferred_element_type=F32)) + br_ref[...]
    lane = lax.broadcasted_iota(jnp.int32, logits.shape, 1)
    neg = -jnp.inf
    big = jnp.int32(LANES)
    gl = jnp.where(lane < N_GROUPS, logits, neg)
    gmax = jnp.max(gl, -1, keepdims=True)
    p_grp = 1.0 / jnp.sum(jnp.exp(gl - gmax), -1, keepdims=True)
    grp = jnp.min(jnp.where(gl == gmax, lane, big), -1, keepdims=True)
    lo = N_GROUPS + EXPERTS_PER_GROUP * grp
    el = jnp.where((lane >= lo) & (lane < lo + EXPERTS_PER_GROUP), logits, neg)
    v1 = jnp.max(el, -1, keepdims=True)
    i1 = jnp.min(jnp.where(el == v1, lane, big), -1, keepdims=True)
    el2 = jnp.where(lane == i1, neg, el)
    v2 = jnp.max(el2, -1, keepdims=True)
    i2 = jnp.min(jnp.where(el2 == v2, lane, big), -1, keepdims=True)
    e21 = jnp.exp(v2 - v1)
    w1 = p_grp / (1.0 + e21)
    w2 = p_grp * e21 / (1.0 + e21)
    @pl.when((pl.program_id(0) == 0) & (pl.program_id(1) == 0))
    def _():
        cnt_ref[...] = jnp.zeros_like(cnt_ref)

    e1 = i1 - N_GROUPS
    e2 = i2 - N_GROUPS
    hot1 = jnp.where(lane == e1, 1.0, 0.0)
    hot2 = jnp.where(lane == e2, 1.0, 0.0)
    both = hot1 + hot2
    both16 = both.astype(BF16)
    blk = tri_ref.shape[0] // TRI_BLOCKS
    before = jnp.concatenate(
        [jnp.dot(tri_ref[r * blk:(r + 1) * blk, :(r + 1) * blk], both16[:(r + 1) * blk], preferred_element_type=F32)
         for r in range(TRI_BLOCKS)], axis=0) + cnt_ref[...]
    rank1 = jnp.sum(hot1 * before, -1, keepdims=True)
    rank2 = jnp.sum(hot2 * before, -1, keepdims=True)
    cnt_ref[...] += jnp.sum(both, 0, keepdims=True)
    cnt_out_ref[...] = cnt_ref[...]
    route = jnp.zeros(logits.shape, F32)
    for k, val in enumerate((e1.astype(F32), e2.astype(F32), w1, w2, rank1, rank2)):
        route = jnp.where(lane == k, val, route)
    route_ref[...] = route
    route_t_ref[...] = jnp.transpose(route)[:ROUTE_LANES, :]


def _post(att, c, h0, km, vm, w_out, g1, b1, w_cq, w_co, g2, b2, wr_hi, wr_lo, br):
    b, seq, _ = att.shape
    tm = TM_POST
    tok = lambda width: pl.BlockSpec((None, tm, width), lambda bi, i: (bi, i, 0))
    memb = pl.BlockSpec((None, N_MEM, D_MODEL), lambda bi, i: (bi, 0, 0))
    const = lambda shape: pl.BlockSpec(shape, lambda bi, i: (0, 0))
    vec = const((1, D_MODEL))
    sq = const((D_MODEL, D_MODEL))
    idx = jnp.arange(tm, dtype=jnp.int32)
    tri = jnp.where(idx[:, None] > idx[None, :], 1.0, 0.0).astype(BF16)
    return pl.pallas_call(
        _post_kernel,
        grid=(b, seq // tm),
        in_specs=[tok(ATT_WIDTH), tok(CONV_WIDTH), tok(D_MODEL), memb, memb, sq, vec, vec, sq, sq, vec, vec,
                  const((D_MODEL, 2 * LANES)), const((1, LANES)), const((tm, tm))],
        out_specs=[tok(D_MODEL),
                   pl.BlockSpec((2, None, tm, ROW_WORDS), lambda bi, i: (0, bi, i, 0)),
                   tok(LANES),
                   pl.BlockSpec((ROUTE_LANES, tm), lambda bi, i: (0, bi * (seq // tm) + i)),
                   const((1, LANES))],
        out_shape=[jax.ShapeDtypeStruct((b, seq, D_MODEL), F32),
                   jax.ShapeDtypeStruct((2, b, seq, ROW_WORDS), jnp.uint32),
                   jax.ShapeDtypeStruct((b, seq, LANES), F32),
                   jax.ShapeDtypeStruct((ROUTE_LANES, b * seq), F32),
                   jax.ShapeDtypeStruct((1, LANES), F32)],
        scratch_shapes=[pltpu.VMEM((1, LANES), F32)],
        compiler_params=pltpu.CompilerParams(dimension_semantics=("arbitrary", "arbitrary"),
                                             vmem_limit_bytes=VMEM_LIMIT),
        name="post",
    )(att, c, h0, km, vm, w_out, g1, b1, w_cq, w_co, g2, b2, jnp.concatenate([wr_hi, wr_lo], axis=1), br, tri)


def _pack_bf16_pairs(x):
    k = x.shape[1] // 2
    bits = lax.bitcast_convert_type(x, jnp.uint32)
    return (bits[:, :k] >> 16) | (bits[:, k:] & jnp.uint32(0xFFFF0000))


def _unpack_bf16_pairs(w):
    lo = lax.bitcast_convert_type(w << 16, F32)
    hi = lax.bitcast_convert_type(w & jnp.uint32(0xFFFF0000), F32)
    return jnp.concatenate([lo, hi], axis=1)


def _sc_mesh():
    return plsc.VectorSubcoreMesh(core_axis_name="c", subcore_axis_name="s")


def _sc_scatter2(x, idx_a, idx_b, n_out):
    m = x.shape[0]

    @pl.kernel(out_type=jax.ShapeDtypeStruct((n_out, ROW_WORDS), x.dtype), mesh=_sc_mesh(), scratch_types=[])
    def scatter(x_hbm, ia_hbm, ib_hbm, o_hbm):
        def body(x_vmem, ia_vmem, ib_vmem):
            pltpu.sync_copy(x_vmem, o_hbm.at[ia_vmem.at[0]])
            pltpu.sync_copy(x_vmem, o_hbm.at[ib_vmem.at[0]])

        pltpu.emit_pipeline(
            body, grid=(m // SC_WINDOW,),
            in_specs=[pl.BlockSpec((SC_WINDOW, ROW_WORDS), lambda i: (i, 0)),
                      pl.BlockSpec((1, SC_WINDOW), lambda i: (0, i)),
                      pl.BlockSpec((1, SC_WINDOW), lambda i: (0, i))],
            out_specs=[],
            core_axis_name=("c", "s"), dimension_semantics=(pltpu.PARALLEL,),
        )(x_hbm, ia_hbm, ib_hbm)

    return scatter(x, idx_a.reshape(1, m), idx_b.reshape(1, m))


def _sc_gather(x, idx):
    m = idx.shape[0]

    @pl.kernel(out_type=jax.ShapeDtypeStruct((m, ROW_WORDS), x.dtype), mesh=_sc_mesh(), scratch_types=[])
    def gather(x_hbm, i_hbm, o_hbm):
        def body(i_vmem, o_vmem):
            pltpu.sync_copy(x_hbm.at[i_vmem.at[0]], o_vmem)

        pltpu.emit_pipeline(
            body, grid=(m // SC_WINDOW,),
            in_specs=[pl.BlockSpec((1, SC_WINDOW), lambda i: (0, i))],
            out_specs=[pl.BlockSpec((SC_WINDOW, ROW_WORDS), lambda i: (i, 0))],
            core_axis_name=("c", "s"), dimension_semantics=(pltpu.PARALLEL,),
        )(i_hbm, o_hbm)

    return gather(x, idx.reshape(1, m))


def _experts_kernel(tile_expert_ref, n_used_ref, x_ref, wg32_ref, wu32_ref, wd32_ref, y_ref, wg_ref, wu_ref, wd_ref):
    j = pl.program_id(0)
    prev = tile_expert_ref[jnp.maximum(j - 1, 0)]

    @pl.when((j == 0) | (tile_expert_ref[j] != prev))
    def _():
        wg_ref[...] = wg32_ref[...].astype(BF16)
        wu_ref[...] = wu32_ref[...].astype(BF16)
        wd_ref[...] = wd32_ref[...].astype(BF16)

    @pl.when(j < n_used_ref[0])
    def _():
        u0 = _unpack_bf16_pairs(x_ref[0])
        u1 = _unpack_bf16_pairs(x_ref[1])
        x = jnp.concatenate([u0[:, :ROW_WORDS], u1[:, :ROW_WORDS], u0[:, ROW_WORDS:], u1[:, ROW_WORDS:]],
                            axis=1).astype(BF16)
        gate = jnp.dot(x, wg_ref[...], preferred_element_type=F32)
        up = jnp.dot(x, wu_ref[...], preferred_element_type=F32)
        he = (gate * _sigmoid(gate) * up).astype(BF16)
        y = jnp.dot(he, wd_ref[...], preferred_element_type=F32)
        yw = _pack_bf16_pairs(y.astype(BF16).astype(F32))
        for half in range(2):
            y_ref[half] = yw[:, half * ROW_WORDS:(half + 1) * ROW_WORDS]


def _experts(xs, tile_expert, n_used, w_gate, w_up, w_down):
    _, rows, _ = xs.shape
    tr = TR_EXPERT
    halves = pl.BlockSpec((2, tr, ROW_WORDS), lambda j, te, nu: (0, jnp.minimum(j, nu[0] - 1), 0))
    return pl.pallas_call(
        _experts_kernel,
        grid_spec=pltpu.PrefetchScalarGridSpec(
            num_scalar_prefetch=2,
            grid=(rows // tr,),
            in_specs=[halves,
                      pl.BlockSpec((None, D_MODEL, D_EXPERT), lambda j, te, nu: (te[j], 0, 0)),
                      pl.BlockSpec((None, D_MODEL, D_EXPERT), lambda j, te, nu: (te[j], 0, 0)),
                      pl.BlockSpec((None, D_EXPERT, D_MODEL), lambda j, te, nu: (te[j], 0, 0))],
            out_specs=halves,
            scratch_shapes=[pltpu.VMEM((D_MODEL, D_EXPERT), BF16), pltpu.VMEM((D_MODEL, D_EXPERT), BF16),
                            pltpu.VMEM((D_EXPERT, D_MODEL), BF16)],
        ),
        out_shape=jax.ShapeDtypeStruct(xs.shape, jnp.uint32),
        compiler_params=pltpu.CompilerParams(dimension_semantics=("arbitrary",), vmem_limit_bytes=VMEM_LIMIT),
        name="experts",
    )(tile_expert, n_used, xs, w_gate, w_up, w_down)


def _combine_kernel(h2_ref, y1_ref, y2_ref, route_ref, g3_ref, b3_ref, o_ref):
    route = route_ref[...]
    w1 = route[:, ROUTE_W:ROUTE_W + 1]
    w2 = route[:, ROUTE_W + 1:ROUTE_W + 2]
    halves = []
    for half in range(2):
        halves.append(w1 * _unpack_bf16_pairs(y1_ref[half]) + w2 * _unpack_bf16_pairs(y2_ref[half]))
    ff = jnp.concatenate([halves[0][:, :ROW_WORDS], halves[1][:, :ROW_WORDS],
                          halves[0][:, ROW_WORDS:], halves[1][:, ROW_WORDS:]], axis=1)
    o_ref[...] = _layer_norm(DEEPNORM_ALPHA * h2_ref[...] + ff, g3_ref[...], b3_ref[...])


def _combine(h2, y1, y2, route, g3, b3):
    n = h2.shape[0]
    tm = TM_COMBINE
    row = lambda width: pl.BlockSpec((tm, width), lambda i: (i, 0))
    halves = pl.BlockSpec((2, tm, ROW_WORDS), lambda i: (0, i, 0))
    vec = pl.BlockSpec((1, D_MODEL), lambda i: (0, 0))
    return pl.pallas_call(
        _combine_kernel,
        grid=(n // tm,),
        in_specs=[row(D_MODEL), halves, halves, row(LANES), vec, vec],
        out_specs=row(D_MODEL),
        out_shape=jax.ShapeDtypeStruct((n, D_MODEL), F32),
        compiler_params=pltpu.CompilerParams(dimension_semantics=("parallel",), vmem_limit_bytes=VMEM_LIMIT),
        name="combine",
    )(h2, y1, y2, route, g3, b3)


def _moe(tw, route, route_t, counts, h2, w_gate, w_up, w_down, g3, b3):
    n = h2.shape[0]
    tr = TR_EXPERT
    cap = 2 * n + N_EXPERTS * tr
    cnt = counts[0, :N_EXPERTS].astype(jnp.int32)
    seg = (cnt + tr - 1) // tr * tr
    ends = jnp.cumsum(seg)
    starts = ends - seg
    tile_start = jnp.arange(cap // tr, dtype=jnp.int32) * tr
    tile_expert = jnp.minimum(jnp.sum(tile_start[:, None] >= ends[None, :], axis=1, dtype=jnp.int32), N_EXPERTS - 1)
    n_used = (ends[-1] // tr).reshape(1)

    def half_row_index(k):
        expert = route_t[ROUTE_E + k].astype(jnp.int32)
        hot = expert[None, :] == jnp.arange(N_EXPERTS, dtype=jnp.int32)[:, None]
        pos = route_t[ROUTE_RANK + k].astype(jnp.int32) + jnp.sum(jnp.where(hot, starts[:, None], 0), axis=0)
        return jnp.concatenate([pos, cap + pos])

    idx1, idx2 = half_row_index(0), half_row_index(1)
    xs = _sc_scatter2(tw.reshape(2 * n, ROW_WORDS), idx1, idx2, 2 * cap)
    ys = _experts(xs.reshape(2, cap, ROW_WORDS), tile_expert, n_used, w_gate, w_up, w_down)
    ys = ys.reshape(2 * cap, ROW_WORDS)
    y1 = _sc_gather(ys, idx1).reshape(2, n, ROW_WORDS)
    y2 = _sc_gather(ys, idx2).reshape(2, n, ROW_WORDS)
    return _combine(h2, y1, y2, route, g3, b3)


def _rope_tables(seq):
    rows = seq // GRID_W
    row = jnp.repeat(jnp.arange(rows, dtype=F32), GRID_W)
    col = jnp.tile(jnp.arange(GRID_W, dtype=F32), rows)
    inv_freq = ROPE_THETA ** (-jnp.arange(0, ROPE_AXIS_DIM, 2, dtype=F32) / ROPE_AXIS_DIM)
    ang = jnp.concatenate([row[:, None] * inv_freq, col[:, None] * inv_freq], -1)
    cos = jnp.repeat(jnp.cos(ang), 2, axis=-1)
    sin = jnp.repeat(jnp.sin(ang), 2, axis=-1)
    sign = jnp.tile(jnp.array([-1.0, 1.0], F32), HEAD_DIM // 2)
    sin = sin * sign
    return jnp.tile(cos, (1, LANES // HEAD_DIM)), jnp.tile(sin, (1, LANES // HEAD_DIM))


def _encoder(x, mem, p):
    b, seq, d = x.shape
    cos, sin = _rope_tables(seq)
    h0, q, k, v, u = _proj_in(x.reshape(b * seq, d), p["ln_in_g"], p["ln_in_b"], p["w_in"], p["qg"], p["kg"],
                              cos, sin, p["head_mean"], seq)
    c = _conv_branch(u.reshape(b, seq, CONV_WIDTH), p["conv_w"], p["conv_b"], p["cln_g"], p["cln_b"])
    att = _gqa(q.reshape(ATT_HEADS, b, seq, HEAD_DIM), k.reshape(ATT_KV_HEADS, b, seq, HEAD_DIM),
               v.reshape(b, seq, 2 * KV_WIDTH))
    km, vm = _mem_kv(mem, p["ln_mem_g"], p["ln_mem_b"], p["w_ck"], p["w_cv"])
    h2, tw, route, route_t, counts = _post(att, c, h0.reshape(b, seq, d), km, vm, p["w_out"], p["ln1_g"],
                                           p["ln1_b"], p["w_cq"], p["w_co"], p["ln2_g"], p["ln2_b"],
                                           p["wr_hi"], p["wr_lo"], p["br"])
    y = _moe(tw.reshape(2, b * seq, ROW_WORDS), route.reshape(b * seq, LANES), route_t, counts,
             h2.reshape(b * seq, d), p["w_gate"], p["w_up"], p["w_down"], p["ln3_g"], p["ln3_b"])
    return y.reshape(b, seq, d)


def kernel(x_prompt, x_sample, mem_prompt, mem_sample, ln_in_g, ln_in_b, ln_mem_g, ln_mem_b, w_in, q_norm_g, k_norm_g, conv_w, conv_b, conv_ln_g, conv_ln_b, w_mix_out, ln1_g, ln1_b, w_cq, w_ck, w_cv, w_co, ln2_g, ln2_b, w_router_g, b_router_g, w_router_e, b_router_e, w_e_gate, w_e_up, w_e_down, ln3_g, ln3_b):
    l = 0
    vec = lambda a: a.reshape(1, -1).astype(F32)
    w_r = jnp.concatenate([w_router_g[l], jnp.transpose(w_router_e[l], (1, 0, 2)).reshape(D_MODEL, N_EXPERTS)], axis=1)
    w_r = jnp.pad(w_r, ((0, 0), (0, LANES - w_r.shape[1])))
    wr_hi = w_r.astype(BF16)
    wr_lo = (w_r - wr_hi.astype(F32)).astype(BF16)
    b_r = jnp.concatenate([b_router_g[l], b_router_e[l].reshape(-1)])
    b_r = jnp.pad(b_r, (0, LANES - b_r.shape[0])).reshape(1, LANES)
    head = jnp.arange(LANES) // HEAD_DIM
    head_mean = jnp.where(head[:, None] == head[None, :], 1.0 / HEAD_DIM, 0.0).astype(BF16)
    p = {
        "ln_in_g": vec(ln_in_g), "ln_in_b": vec(ln_in_b), "ln_mem_g": vec(ln_mem_g), "ln_mem_b": vec(ln_mem_b),
        "w_in": w_in[l].astype(BF16),
        "qg": jnp.tile(q_norm_g[l], LANES // HEAD_DIM).reshape(1, LANES),
        "kg": jnp.tile(k_norm_g[l], LANES // HEAD_DIM).reshape(1, LANES),
        "head_mean": head_mean,
        "conv_w": conv_w[l], "conv_b": vec(conv_b[l]), "cln_g": vec(conv_ln_g[l]), "cln_b": vec(conv_ln_b[l]),
        "w_out": w_mix_out[l].astype(BF16), "ln1_g": vec(ln1_g[l]), "ln1_b": vec(ln1_b[l]),
        "w_cq": w_cq[l].astype(BF16), "w_ck": w_ck[l].astype(BF16), "w_cv": w_cv[l].astype(BF16),
        "w_co": w_co[l].astype(BF16), "ln2_g": vec(ln2_g[l]), "ln2_b": vec(ln2_b[l]),
        "wr_hi": wr_hi, "wr_lo": wr_lo, "br": b_r,
        "w_gate": w_e_gate[l], "w_up": w_e_up[l], "w_down": w_e_down[l],
        "ln3_g": vec(ln3_g[l]), "ln3_b": vec(ln3_b[l]),
    }
    return (_encoder(x_prompt, mem_prompt, p), _encoder(x_sample, mem_sample, p))
```

```python
import functools

import jax
import jax.numpy as jnp
from jax import lax
from jax.experimental import pallas as pl
from jax.experimental.pallas import tpu as pltpu
from jax.experimental.pallas import tpu_sc as plsc

F32 = jnp.float32
BF16 = jnp.bfloat16

D_MODEL = 1024
DEPTH = 1
GRID_W = 64
N_MEM = 256
ATT_HEADS = 8
ATT_KV_HEADS = 2
HEAD_DIM = 64
ATT_WIDTH = ATT_HEADS * HEAD_DIM
KV_WIDTH = ATT_KV_HEADS * HEAD_DIM
ROPE_THETA = 10000.0
ROPE_AXIS_DIM = HEAD_DIM // 2
CONV_WIDTH = D_MODEL - ATT_WIDTH
CONV_KSIZE = 31
CONV_PAD = CONV_KSIZE // 2
IN_WIDTH = ATT_WIDTH + 2 * KV_WIDTH + 2 * CONV_WIDTH
MEM_HEADS = 4
MEM_HEAD_DIM = D_MODEL // MEM_HEADS
N_GROUPS = 4
EXPERTS_PER_GROUP = 8
N_EXPERTS = N_GROUPS * EXPERTS_PER_GROUP
D_EXPERT = 256
LN_EPS = 1e-5
RMS_EPS = 1e-6
DEEPNORM_ALPHA = (2.0 * DEPTH) ** 0.25
LOG2_E = 1.4426950408889634
Q_SCALE = HEAD_DIM ** -0.5 * LOG2_E

LANES = 128
SUBLANES = 8
HALO = 16
VMEM_LIMIT = 56 * 1024 * 1024

TM_IN = 1024
IN_SPLIT = 4
TC_CONV = 512
CONV_MXU_TAPS = 16
CONV_MXU_COLS = 256
CONV_VPU_ROWS = 256
TQ_ATT = 512
KV_CHUNK = 512
ATT_SAFE_SPAN = 50.0
TM_POST = 1024
POST_SPLIT = 4
TRI_BLOCKS = 4
TR_EXPERT = 512
TM_COMBINE = 1024
ROW_WORDS = 256
SC_WINDOW = 128
ROUTE_E, ROUTE_W, ROUTE_RANK = 0, 2, 4
ROUTE_LANES = 8


def _layer_norm(x, g, b):
    mu = jnp.mean(x, -1, keepdims=True)
    xc = x - mu
    var = jnp.mean(xc * xc, -1, keepdims=True)
    return xc * lax.rsqrt(var + LN_EPS) * g + b


def _sigmoid(x):
    return 1.0 / (1.0 + jnp.exp(-x))


def _rope(x, cos, sin_signed):
    lane = lax.broadcasted_iota(jnp.int32, x.shape, 1)
    nxt = pltpu.roll(x, LANES - 1, axis=1)
    prv = pltpu.roll(x, 1, axis=1)
    return x * cos + jnp.where((lane & 1) == 0, nxt, prv) * sin_signed


def _proj_in_kernel(x_ref, g_ref, b_ref, w_ref, qg_ref, kg_ref, cos_ref, sin_ref, hm_ref,
                    h_ref, q_ref, k_ref, v_ref, u_ref):
    i1 = ATT_WIDTH
    i2 = i1 + KV_WIDTH
    i3 = i2 + KV_WIDTH
    i4 = i3 + CONV_WIDTH
    hm = hm_ref[...]
    heads_per_tile = LANES // HEAD_DIM
    tm = x_ref.shape[0]
    for rows in [slice(r0, r0 + tm // IN_SPLIT) for r0 in range(0, tm, tm // IN_SPLIT)]:
        h = _layer_norm(x_ref[rows, :], g_ref[...], b_ref[...])
        h_ref[rows, :] = h
        z = jnp.dot(h.astype(BF16), w_ref[...], preferred_element_type=F32)
        cos = cos_ref[rows, :]
        sin = sin_ref[rows, :]
        for j in range(ATT_WIDTH // LANES):
            qj = z[:, j * LANES:(j + 1) * LANES]
            ms = jnp.dot((qj * qj).astype(BF16), hm, preferred_element_type=F32)
            qn = qj * lax.rsqrt(ms + RMS_EPS) * qg_ref[...]
            qr = (_rope(qn, cos, sin) * Q_SCALE).astype(BF16)
            for r in range(heads_per_tile):
                q_ref[j * heads_per_tile + r, rows, :] = qr[:, r * HEAD_DIM:(r + 1) * HEAD_DIM]
        kj = z[:, i1:i2]
        ms = jnp.dot((kj * kj).astype(BF16), hm, preferred_element_type=F32)
        kn = kj * lax.rsqrt(ms + RMS_EPS) * kg_ref[...]
        kr = _rope(kn, cos, sin).astype(BF16)
        vj = z[:, i2:i3].astype(BF16)
        ones = jnp.ones((vj.shape[0], HEAD_DIM), BF16)
        for g in range(ATT_KV_HEADS):
            k_ref[g, rows, :] = kr[:, g * HEAD_DIM:(g + 1) * HEAD_DIM]
        v_ref[rows, :] = jnp.concatenate([vj[:, :HEAD_DIM], ones, vj[:, HEAD_DIM:], ones], axis=1)
        u_ref[rows, :] = (z[:, i3:i4] * _sigmoid(z[:, i4:])).astype(BF16)


def _proj_in(x, ln_g, ln_b, w_in, qg, kg, cos, sin, hm, seq):
    t = x.shape[0]
    tm = TM_IN
    nseq = seq // tm
    row = lambda i: (i, 0)
    const = lambda i: (0, 0)
    return pl.pallas_call(
        _proj_in_kernel,
        grid=(t // tm,),
        in_specs=[
            pl.BlockSpec((tm, D_MODEL), row),
            pl.BlockSpec((1, D_MODEL), const),
            pl.BlockSpec((1, D_MODEL), const),
            pl.BlockSpec((D_MODEL, IN_WIDTH), const),
            pl.BlockSpec((1, LANES), const),
            pl.BlockSpec((1, LANES), const),
            pl.BlockSpec((tm, LANES), lambda i: (i % nseq, 0)),
            pl.BlockSpec((tm, LANES), lambda i: (i % nseq, 0)),
            pl.BlockSpec((LANES, LANES), const),
        ],
        out_specs=[
            pl.BlockSpec((tm, D_MODEL), row),
            pl.BlockSpec((ATT_HEADS, tm, HEAD_DIM), lambda i: (0, i, 0)),
            pl.BlockSpec((ATT_KV_HEADS, tm, HEAD_DIM), lambda i: (0, i, 0)),
            pl.BlockSpec((tm, 2 * KV_WIDTH), row),
            pl.BlockSpec((tm, CONV_WIDTH), row),
        ],
        out_shape=[
            jax.ShapeDtypeStruct((t, D_MODEL), F32),
            jax.ShapeDtypeStruct((ATT_HEADS, t, HEAD_DIM), BF16),
            jax.ShapeDtypeStruct((ATT_KV_HEADS, t, HEAD_DIM), BF16),
            jax.ShapeDtypeStruct((t, 2 * KV_WIDTH), BF16),
            jax.ShapeDtypeStruct((t, CONV_WIDTH), BF16),
        ],
        compiler_params=pltpu.CompilerParams(dimension_semantics=("parallel",), vmem_limit_bytes=VMEM_LIMIT),
        name="proj_in",
    )(x, ln_g, ln_b, w_in, qg, kg, cos, sin, hm)


def _conv_kernel(u_ref, w_ref, wdiag_ref, cb_ref, g_ref, b_ref, o_ref, win_ref, sh_ref, *, seq):
    i = pl.program_id(1)
    tc = TC_CONV
    t0 = pl.multiple_of(i * tc, tc)
    top_start = pl.multiple_of(jnp.maximum(t0 - HALO, 0), HALO)
    bot_start = pl.multiple_of(jnp.minimum(t0 + tc, seq - HALO), HALO)
    top = u_ref[pl.ds(top_start, HALO), :].astype(F32)
    bot = u_ref[pl.ds(bot_start, HALO), :].astype(F32)
    win_ref[0:HALO, :] = jnp.where(i > 0, top, 0.0)
    win_ref[HALO:HALO + tc, :] = u_ref[pl.ds(t0, tc), :].astype(F32)
    win_ref[HALO + tc:, :] = jnp.where(i < pl.num_programs(1) - 1, bot, 0.0)
    sh_rows = sh_ref.shape[1]
    for s in range(1, SUBLANES):
        sh_ref[s] = win_ref[s:s + sh_rows, :]

    def tap(j, r0, rows, cs):
        off = r0 + HALO - CONV_PAD + j
        base, s = off - off % SUBLANES, off % SUBLANES
        return win_ref[base:base + rows, cs] if s == 0 else sh_ref[s, base:base + rows, cs]

    mxu_cols = []
    for cblk in range(CONV_WIDTH // CONV_MXU_COLS):
        cs = slice(cblk * CONV_MXU_COLS, (cblk + 1) * CONV_MXU_COLS)
        acc = None
        for j in range(CONV_MXU_TAPS):
            d = jnp.dot(tap(j, 0, tc, cs).astype(BF16), wdiag_ref[j, cblk], preferred_element_type=F32)
            acc = d if acc is None else acc + d
        mxu_cols.append(acc)
    w = w_ref[...]
    for r0 in range(0, tc, CONV_VPU_ROWS):
        cols = []
        for c in range(CONV_WIDTH // LANES):
            cs = slice(c * LANES, (c + 1) * LANES)
            lanes_in_blk = slice(c * LANES % CONV_MXU_COLS, c * LANES % CONV_MXU_COLS + LANES)
            acc = None
            for j in range(CONV_MXU_TAPS, CONV_KSIZE):
                term = tap(j, r0, CONV_VPU_ROWS, cs) * w[j:j + 1, cs]
                acc = term if acc is None else acc + term
            cols.append(acc + mxu_cols[c * LANES // CONV_MXU_COLS][r0:r0 + CONV_VPU_ROWS, lanes_in_blk])
        y = jnp.concatenate(cols, axis=1) + cb_ref[...]
        y = _layer_norm(y, g_ref[...], b_ref[...])
        o_ref[r0:r0 + CONV_VPU_ROWS, :] = (y * _sigmoid(y)).astype(BF16)


def _conv_branch(u, conv_w, conv_b, cln_g, cln_b):
    b, seq, _ = u.shape
    tc = TC_CONV
    const = lambda bi, i: (0, 0)
    nblk = CONV_WIDTH // CONV_MXU_COLS
    eye = jnp.eye(CONV_MXU_COLS, dtype=F32)
    wdiag = (conv_w[:CONV_MXU_TAPS].reshape(CONV_MXU_TAPS, nblk, 1, CONV_MXU_COLS) * eye).astype(BF16)
    return pl.pallas_call(
        functools.partial(_conv_kernel, seq=seq),
        grid=(b, seq // tc),
        in_specs=[
            pl.BlockSpec((None, seq, CONV_WIDTH), lambda bi, i: (bi, 0, 0)),
            pl.BlockSpec((CONV_KSIZE, CONV_WIDTH), const),
            pl.BlockSpec((CONV_MXU_TAPS, nblk, CONV_MXU_COLS, CONV_MXU_COLS), lambda bi, i: (0, 0, 0, 0)),
            pl.BlockSpec((1, CONV_WIDTH), const),
            pl.BlockSpec((1, CONV_WIDTH), const),
            pl.BlockSpec((1, CONV_WIDTH), const),
        ],
        out_specs=pl.BlockSpec((None, tc, CONV_WIDTH), lambda bi, i: (bi, i, 0)),
        out_shape=jax.ShapeDtypeStruct((b, seq, CONV_WIDTH), BF16),
        scratch_shapes=[pltpu.VMEM((tc + 2 * HALO, CONV_WIDTH), F32),
                        pltpu.VMEM((SUBLANES, tc + 2 * HALO - SUBLANES, CONV_WIDTH), F32)],
        compiler_params=pltpu.CompilerParams(dimension_semantics=("parallel", "arbitrary"),
                                             vmem_limit_bytes=VMEM_LIMIT),
        name="conv_branch",
    )(u, conv_w, wdiag, conv_b, cln_g, cln_b)


def _gqa_kernel(q_ref, k_ref, v_ref, o_ref, kmax_ref):
    rep = ATT_HEADS // ATT_KV_HEADS
    tq = q_ref.shape[1]
    seq = k_ref.shape[1]
    n_chunks = seq // KV_CHUNK
    nt_dims = (((1,), (1,)), ((), ()))

    @pl.when(pl.program_id(1) == 0)
    def _():
        for g in range(ATT_KV_HEADS):
            kf = k_ref[g].astype(F32)
            kmax_ref[g] = jnp.sqrt(jnp.max(jnp.sum(kf * kf, -1, keepdims=True), axis=0, keepdims=True))

    qs = [q_ref[g * rep:(g + 1) * rep].reshape(rep * tq, HEAD_DIM) for g in range(ATT_KV_HEADS)]
    bound = []
    for g in range(ATT_KV_HEADS):
        qf = qs[g].astype(F32)
        bound.append(jnp.sqrt(jnp.sum(qf * qf, -1, keepdims=True)) * kmax_ref[g])
    worst = jnp.maximum(jnp.max(bound[0]), jnp.max(bound[1]))

    def finish(acc):
        outs = []
        for g in range(ATT_KV_HEADS):
            o = acc[g] / pltpu.roll(acc[g], HEAD_DIM, axis=1)
            outs.extend(o[r * tq:(r + 1) * tq, :HEAD_DIM] for r in range(rep))
        o_ref[...] = jnp.concatenate(outs, axis=1).astype(BF16)

    @pl.when(worst <= ATT_SAFE_SPAN)
    def _():
        acc = [None] * ATT_KV_HEADS
        for c in range(n_chunks):
            rows = slice(c * KV_CHUNK, (c + 1) * KV_CHUNK)
            for g in range(ATT_KV_HEADS):
                s = lax.dot_general(qs[g], k_ref[g, rows], nt_dims, preferred_element_type=F32)
                p = jnp.exp2(s - bound[g]).astype(BF16)
                d = jnp.dot(p, v_ref[rows, g * LANES:(g + 1) * LANES], preferred_element_type=F32)
                acc[g] = d if c == 0 else acc[g] + d
        finish(acc)

    @pl.when(worst > ATT_SAFE_SPAN)
    def _():
        def step(c, carry):
            rows = pl.ds(pl.multiple_of(c * KV_CHUNK, KV_CHUNK), KV_CHUNK)
            new = []
            for g in range(ATT_KV_HEADS):
                m, acc = carry[g]
                s = lax.dot_general(qs[g], k_ref[g, rows], nt_dims, preferred_element_type=F32)
                m_new = jnp.maximum(m, jnp.max(s, -1, keepdims=True))
                p = jnp.exp2(s - m_new).astype(BF16)
                d = jnp.dot(p, v_ref[rows, g * LANES:(g + 1) * LANES], preferred_element_type=F32)
                new.append((m_new, jnp.exp2(m - m_new) * acc + d))
            return tuple(new)

        init = tuple((jnp.full((rep * tq, 1), -jnp.inf, F32), jnp.zeros((rep * tq, LANES), F32))
                     for _ in range(ATT_KV_HEADS))
        out = lax.fori_loop(0, n_chunks, step, init)
        finish([out[g][1] for g in range(ATT_KV_HEADS)])


def _gqa(q, k, v):
    _, b, seq, _ = q.shape
    tq = TQ_ATT
    return pl.pallas_call(
        _gqa_kernel,
        grid=(b, seq // tq),
        in_specs=[
            pl.BlockSpec((ATT_HEADS, None, tq, HEAD_DIM), lambda bi, i: (0, bi, i, 0)),
            pl.BlockSpec((ATT_KV_HEADS, None, seq, HEAD_DIM), lambda bi, i: (0, bi, 0, 0)),
            pl.BlockSpec((None, seq, ATT_KV_HEADS * LANES), lambda bi, i: (bi, 0, 0)),
        ],
        out_specs=pl.BlockSpec((None, tq, ATT_WIDTH), lambda bi, i: (bi, i, 0)),
        out_shape=jax.ShapeDtypeStruct((b, seq, ATT_WIDTH), BF16),
        scratch_shapes=[pltpu.VMEM((ATT_KV_HEADS, 1, 1), F32)],
        compiler_params=pltpu.CompilerParams(dimension_semantics=("arbitrary", "arbitrary"),
                                             vmem_limit_bytes=VMEM_LIMIT),
        name="gqa",
    )(q, k, v)


def _mem_kv_kernel(m_ref, g_ref, b_ref, wk_ref, wv_ref, k_ref, v_ref):
    m = _layer_norm(m_ref[...], g_ref[...], b_ref[...]).astype(BF16)
    k_ref[...] = jnp.dot(m, wk_ref[...], preferred_element_type=F32).astype(BF16)
    v_ref[...] = jnp.dot(m, wv_ref[...], preferred_element_type=F32).astype(BF16)


def _mem_kv(mem, ln_g, ln_b, w_ck, w_cv):
    b = mem.shape[0]
    const = lambda bi: (0, 0)
    blk = pl.BlockSpec((None, N_MEM, D_MODEL), lambda bi: (bi, 0, 0))
    return pl.pallas_call(
        _mem_kv_kernel,
        grid=(b,),
        in_specs=[blk, pl.BlockSpec((1, D_MODEL), const), pl.BlockSpec((1, D_MODEL), const),
                  pl.BlockSpec((D_MODEL, D_MODEL), const), pl.BlockSpec((D_MODEL, D_MODEL), const)],
        out_specs=[blk, blk],
        out_shape=[jax.ShapeDtypeStruct((b, N_MEM, D_MODEL), BF16)] * 2,
        compiler_params=pltpu.CompilerParams(dimension_semantics=("parallel",), vmem_limit_bytes=VMEM_LIMIT),
        name="mem_kv",
    )(mem, ln_g, ln_b, w_ck, w_cv)


def _post_kernel(att_ref, c_ref, h0_ref, km_ref, vm_ref, wo_ref, g1_ref, b1_ref, wq_ref, wco_ref,
                 g2_ref, b2_ref, wr_ref, br_ref, tri_ref, h2_ref, t_ref, route_ref, route_t_ref,
                 cnt_out_ref, cnt_ref):
    tm = att_ref.shape[0]
    halves = [slice(r0, r0 + tm // POST_SPLIT) for r0 in range(0, tm, tm // POST_SPLIT)]
    mixed = [jnp.concatenate([att_ref[r, :], c_ref[r, :]], axis=1) for r in halves]
    mix = [jnp.dot(m, wo_ref[...], preferred_element_type=F32) for m in mixed]
    h1 = [_layer_norm(DEEPNORM_ALPHA * h0_ref[r, :] + m, g1_ref[...], b1_ref[...]) for r, m in zip(halves, mix)]
    qx = [(jnp.dot(h.astype(BF16), wq_ref[...], preferred_element_type=F32) * (MEM_HEAD_DIM ** -0.5)).astype(BF16)
          for h in h1]
    heads = [[] for _ in halves]
    for hd in range(MEM_HEADS):
        hs = slice(hd * MEM_HEAD_DIM, (hd + 1) * MEM_HEAD_DIM)
        for k in range(POST_SPLIT):
            s = lax.dot_general(qx[k][:, hs], km_ref[:, hs], (((1,), (1,)), ((), ())), preferred_element_type=F32)
            p = jnp.exp(s - jnp.max(s, -1, keepdims=True))
            l = jnp.sum(p, -1, keepdims=True)
            heads[k].append(jnp.dot(p.astype(BF16), vm_ref[:, hs], preferred_element_type=F32) / l)
    o = [jnp.concatenate(hk, axis=1).astype(BF16) for hk in heads]
    xa = [jnp.dot(ok, wco_ref[...], preferred_element_type=F32) for ok in o]
    h2 = jnp.concatenate([_layer_norm(DEEPNORM_ALPHA * h + x, g2_ref[...], b2_ref[...]) for h, x in zip(h1, xa)],
                         axis=0)
    h2_ref[...] = h2
    t_hi = h2.astype(BF16)
    t_rounded = t_hi.astype(F32)
    tw = _pack_bf16_pairs(t_rounded)
    for half in range(2):
        t_ref[half] = tw[:, half * ROW_WORDS:(half + 1) * ROW_WORDS]
    t_lo = (h2 - t_rounded).astype(BF16)
    both_parts = jnp.dot(t_hi, wr_ref[...], preferred_element_type=F32)
    logits = (both_parts[:, :LANES] + both_parts[:, LANES:]
              + jnp.dot(t_lo, wr_ref[:, :LANES], preferred_element_type=F32)) + br_ref[...]
    lane = lax.broadcasted_iota(jnp.int32, logits.shape, 1)
    neg = -jnp.inf
    big = jnp.int32(LANES)
    gl = jnp.where(lane < N_GROUPS, logits, neg)
    gmax = jnp.max(gl, -1, keepdims=True)
    p_grp = 1.0 / jnp.sum(jnp.exp(gl - gmax), -1, keepdims=True)
    grp = jnp.min(jnp.where(gl == gmax, lane, big), -1, keepdims=True)
    lo = N_GROUPS + EXPERTS_PER_GROUP * grp
    el = jnp.where((lane >= lo) & (lane < lo + EXPERTS_PER_GROUP), logits, neg)
    v1 = jnp.max(el, -1, keepdims=True)
    i1 = jnp.min(jnp.where(el == v1, lane, big), -1, keepdims=True)
    el2 = jnp.where(lane == i1, neg, el)
    v2 = jnp.max(el2, -1, keepdims=True)
    i2 = jnp.min(jnp.where(el2 == v2, lane, big), -1, keepdims=True)
    e21 = jnp.exp(v2 - v1)
    w1 = p_grp / (1.0 + e21)
    w2 = p_grp * e21 / (1.0 + e21)
    @pl.when((pl.program_id(0) == 0) & (pl.program_id(1) == 0))
    def _():
        cnt_ref[...] = jnp.zeros_like(cnt_ref)

    e1 = i1 - N_GROUPS
    e2 = i2 - N_GROUPS
    hot1 = jnp.where(lane == e1, 1.0, 0.0)
    hot2 = jnp.where(lane == e2, 1.0, 0.0)
    both = hot1 + hot2
    both16 = both.astype(BF16)
    blk = tri_ref.shape[0] // TRI_BLOCKS
    before = jnp.concatenate(
        [jnp.dot(tri_ref[r * blk:(r + 1) * blk, :(r + 1) * blk], both16[:(r + 1) * blk], preferred_element_type=F32)
         for r in range(TRI_BLOCKS)], axis=0) + cnt_ref[...]
    rank1 = jnp.sum(hot1 * before, -1, keepdims=True)
    rank2 = jnp.sum(hot2 * before, -1, keepdims=True)
    cnt_ref[...] += jnp.sum(both, 0, keepdims=True)
    cnt_out_ref[...] = cnt_ref[...]
    route = jnp.zeros(logits.shape, F32)
    for k, val in enumerate((e1.astype(F32), e2.astype(F32), w1, w2, rank1, rank2)):
        route = jnp.where(lane == k, val, route)
    route_ref[...] = route
    route_t_ref[...] = jnp.transpose(route)[:ROUTE_LANES, :]


def _post(att, c, h0, km, vm, w_out, g1, b1, w_cq, w_co, g2, b2, wr_hi, wr_lo, br):
    b, seq, _ = att.shape
    tm = TM_POST
    tok = lambda width: pl.BlockSpec((None, tm, width), lambda bi, i: (bi, i, 0))
    memb = pl.BlockSpec((None, N_MEM, D_MODEL), lambda bi, i: (bi, 0, 0))
    const = lambda shape: pl.BlockSpec(shape, lambda bi, i: (0, 0))
    vec = const((1, D_MODEL))
    sq = const((D_MODEL, D_MODEL))
    idx = jnp.arange(tm, dtype=jnp.int32)
    tri = jnp.where(idx[:, None] > idx[None, :], 1.0, 0.0).astype(BF16)
    return pl.pallas_call(
        _post_kernel,
        grid=(b, seq // tm),
        in_specs=[tok(ATT_WIDTH), tok(CONV_WIDTH), tok(D_MODEL), memb, memb, sq, vec, vec, sq, sq, vec, vec,
                  const((D_MODEL, 2 * LANES)), const((1, LANES)), const((tm, tm))],
        out_specs=[tok(D_MODEL),
                   pl.BlockSpec((2, None, tm, ROW_WORDS), lambda bi, i: (0, bi, i, 0)),
                   tok(LANES),
                   pl.BlockSpec((ROUTE_LANES, tm), lambda bi, i: (0, bi * (seq // tm) + i)),
                   const((1, LANES))],
        out_shape=[jax.ShapeDtypeStruct((b, seq, D_MODEL), F32),
                   jax.ShapeDtypeStruct((2, b, seq, ROW_WORDS), jnp.uint32),
                   jax.ShapeDtypeStruct((b, seq, LANES), F32),
                   jax.ShapeDtypeStruct((ROUTE_LANES, b * seq), F32),
                   jax.ShapeDtypeStruct((1, LANES), F32)],
        scratch_shapes=[pltpu.VMEM((1, LANES), F32)],
        compiler_params=pltpu.CompilerParams(dimension_semantics=("arbitrary", "arbitrary"),
                                             vmem_limit_bytes=VMEM_LIMIT),
        name="post",
    )(att, c, h0, km, vm, w_out, g1, b1, w_cq, w_co, g2, b2, jnp.concatenate([wr_hi, wr_lo], axis=1), br, tri)


def _pack_bf16_pairs(x):
    k = x.shape[1] // 2
    bits = lax.bitcast_convert_type(x, jnp.uint32)
    return (bits[:, :k] >> 16) | (bits[:, k:] & jnp.uint32(0xFFFF0000))


def _unpack_bf16_pairs(w):
    lo = lax.bitcast_convert_type(w << 16, F32)
    hi = lax.bitcast_convert_type(w & jnp.uint32(0xFFFF0000), F32)
    return jnp.concatenate([lo, hi], axis=1)


def _sc_mesh():
    return plsc.VectorSubcoreMesh(core_axis_name="c", subcore_axis_name="s")


def _sc_scatter2(x, idx_a, idx_b, n_out):
    m = x.shape[0]

    @pl.kernel(out_type=jax.ShapeDtypeStruct((n_out, ROW_WORDS), x.dtype), mesh=_sc_mesh(), scratch_types=[])
    def scatter(x_hbm, ia_hbm, ib_hbm, o_hbm):
        def body(x_vmem, ia_vmem, ib_vmem):
            pltpu.sync_copy(x_vmem, o_hbm.at[ia_vmem.at[0]])
            pltpu.sync_copy(x_vmem, o_hbm.at[ib_vmem.at[0]])

        pltpu.emit_pipeline(
            body, grid=(m // SC_WINDOW,),
            in_specs=[pl.BlockSpec((SC_WINDOW, ROW_WORDS), lambda i: (i, 0)),
                      pl.BlockSpec((1, SC_WINDOW), lambda i: (0, i)),
                      pl.BlockSpec((1, SC_WINDOW), lambda i: (0, i))],
            out_specs=[],
            core_axis_name=("c", "s"), dimension_semantics=(pltpu.PARALLEL,),
        )(x_hbm, ia_hbm, ib_hbm)

    return scatter(x, idx_a.reshape(1, m), idx_b.reshape(1, m))


def _sc_gather(x, idx):
    m = idx.shape[0]

    @pl.kernel(out_type=jax.ShapeDtypeStruct((m, ROW_WORDS), x.dtype), mesh=_sc_mesh(), scratch_types=[])
    def gather(x_hbm, i_hbm, o_hbm):
        def body(i_vmem, o_vmem):
            pltpu.sync_copy(x_hbm.at[i_vmem.at[0]], o_vmem)

        pltpu.emit_pipeline(
            body, grid=(m // SC_WINDOW,),
            in_specs=[pl.BlockSpec((1, SC_WINDOW), lambda i: (0, i))],
            out_specs=[pl.BlockSpec((SC_WINDOW, ROW_WORDS), lambda i: (i, 0))],
            core_axis_name=("c", "s"), dimension_semantics=(pltpu.PARALLEL,),
        )(i_hbm, o_hbm)

    return gather(x, idx.reshape(1, m))


def _experts_kernel(tile_expert_ref, n_used_ref, x_ref, wg32_ref, wu32_ref, wd32_ref, y_ref, wg_ref, wu_ref, wd_ref):
    j = pl.program_id(0)
    prev = tile_expert_ref[jnp.maximum(j - 1, 0)]

    @pl.when((j == 0) | (tile_expert_ref[j] != prev))
    def _():
        wg_ref[...] = wg32_ref[...].astype(BF16)
        wu_ref[...] = wu32_ref[...].astype(BF16)
        wd_ref[...] = wd32_ref[...].astype(BF16)

    @pl.when(j < n_used_ref[0])
    def _():
        u0 = _unpack_bf16_pairs(x_ref[0])
        u1 = _unpack_bf16_pairs(x_ref[1])
        x = jnp.concatenate([u0[:, :ROW_WORDS], u1[:, :ROW_WORDS], u0[:, ROW_WORDS:], u1[:, ROW_WORDS:]],
                            axis=1).astype(BF16)
        gate = jnp.dot(x, wg_ref[...], preferred_element_type=F32)
        up = jnp.dot(x, wu_ref[...], preferred_element_type=F32)
        he = (gate * _sigmoid(gate) * up).astype(BF16)
        y = jnp.dot(he, wd_ref[...], preferred_element_type=F32)
        yw = _pack_bf16_pairs(y.astype(BF16).astype(F32))
        for half in range(2):
            y_ref[half] = yw[:, half * ROW_WORDS:(half + 1) * ROW_WORDS]


def _experts(xs, tile_expert, n_used, w_gate, w_up, w_down):
    _, rows, _ = xs.shape
    tr = TR_EXPERT
    halves = pl.BlockSpec((2, tr, ROW_WORDS), lambda j, te, nu: (0, jnp.minimum(j, nu[0] - 1), 0))
    return pl.pallas_call(
        _experts_kernel,
        grid_spec=pltpu.PrefetchScalarGridSpec(
            num_scalar_prefetch=2,
            grid=(rows // tr,),
            in_specs=[halves,
                      pl.BlockSpec((None, D_MODEL, D_EXPERT), lambda j, te, nu: (te[j], 0, 0)),
                      pl.BlockSpec((None, D_MODEL, D_EXPERT), lambda j, te, nu: (te[j], 0, 0)),
                      pl.BlockSpec((None, D_EXPERT, D_MODEL), lambda j, te, nu: (te[j], 0, 0))],
            out_specs=halves,
            scratch_shapes=[pltpu.VMEM((D_MODEL, D_EXPERT), BF16), pltpu.VMEM((D_MODEL, D_EXPERT), BF16),
                            pltpu.VMEM((D_EXPERT, D_MODEL), BF16)],
        ),
        out_shape=jax.ShapeDtypeStruct(xs.shape, jnp.uint32),
        compiler_params=pltpu.CompilerParams(dimension_semantics=("arbitrary",), vmem_limit_bytes=VMEM_LIMIT),
        name="experts",
    )(tile_expert, n_used, xs, w_gate, w_up, w_down)


def _combine_kernel(h2_ref, y1_ref, y2_ref, route_ref, g3_ref, b3_ref, o_ref):
    route = route_ref[...]
    w1 = route[:, ROUTE_W:ROUTE_W + 1]
    w2 = route[:, ROUTE_W + 1:ROUTE_W + 2]
    halves = []
    for half in range(2):
        halves.append(w1 * _unpack_bf16_pairs(y1_ref[half]) + w2 * _unpack_bf16_pairs(y2_ref[half]))
    ff = jnp.concatenate([halves[0][:, :ROW_WORDS], halves[1][:, :ROW_WORDS],
                          halves[0][:, ROW_WORDS:], halves[1][:, ROW_WORDS:]], axis=1)
    o_ref[...] = _layer_norm(DEEPNORM_ALPHA * h2_ref[...] + ff, g3_ref[...], b3_ref[...])


def _combine(h2, y1, y2, route, g3, b3):
    n = h2.shape[0]
    tm = TM_COMBINE
    row = lambda width: pl.BlockSpec((tm, width), lambda i: (i, 0))
    halves = pl.BlockSpec((2, tm, ROW_WORDS), lambda i: (0, i, 0))
    vec = pl.BlockSpec((1, D_MODEL), lambda i: (0, 0))
    return pl.pallas_call(
        _combine_kernel,
        grid=(n // tm,),
        in_specs=[row(D_MODEL), halves, halves, row(LANES), vec, vec],
        out_specs=row(D_MODEL),
        out_shape=jax.ShapeDtypeStruct((n, D_MODEL), F32),
        compiler_params=pltpu.CompilerParams(dimension_semantics=("parallel",), vmem_limit_bytes=VMEM_LIMIT),
        name="combine",
    )(h2, y1, y2, route, g3, b3)


def _moe(tw, route, route_t, counts, h2, w_gate, w_up, w_down, g3, b3):
    n = h2.shape[0]
    tr = TR_EXPERT
    cap = 2 * n + N_EXPERTS * tr
    cnt = counts[0, :N_EXPERTS].astype(jnp.int32)
    seg = (cnt + tr - 1) // tr * tr
    ends = jnp.cumsum(seg)
    starts = ends - seg
    tile_start = jnp.arange(cap // tr, dtype=jnp.int32) * tr
    tile_expert = jnp.minimum(jnp.sum(tile_start[:, None] >= ends[None, :], axis=1, dtype=jnp.int32), N_EXPERTS - 1)
    n_used = (ends[-1] // tr).reshape(1)

    def half_row_index(k):
        expert = route_t[ROUTE_E + k].astype(jnp.int32)
        hot = expert[None, :] == jnp.arange(N_EXPERTS, dtype=jnp.int32)[:, None]
        pos = route_t[ROUTE_RANK + k].astype(jnp.int32) + jnp.sum(jnp.where(hot, starts[:, None], 0), axis=0)
        return jnp.concatenate([pos, cap + pos])

    idx1, idx2 = half_row_index(0), half_row_index(1)
    xs = _sc_scatter2(tw.reshape(2 * n, ROW_WORDS), idx1, idx2, 2 * cap)
    ys = _experts(xs.reshape(2, cap, ROW_WORDS), tile_expert, n_used, w_gate, w_up, w_down)
    ys = ys.reshape(2 * cap, ROW_WORDS)
    y1 = _sc_gather(ys, idx1).reshape(2, n, ROW_WORDS)
    y2 = _sc_gather(ys, idx2).reshape(2, n, ROW_WORDS)
    return _combine(h2, y1, y2, route, g3, b3)


def _rope_tables(seq):
    rows = seq // GRID_W
    row = jnp.repeat(jnp.arange(rows, dtype=F32), GRID_W)
    col = jnp.tile(jnp.arange(GRID_W, dtype=F32), rows)
    inv_freq = ROPE_THETA ** (-jnp.arange(0, ROPE_AXIS_DIM, 2, dtype=F32) / ROPE_AXIS_DIM)
    ang = jnp.concatenate([row[:, None] * inv_freq, col[:, None] * inv_freq], -1)
    cos = jnp.repeat(jnp.cos(ang), 2, axis=-1)
    sin = jnp.repeat(jnp.sin(ang), 2, axis=-1)
    sign = jnp.tile(jnp.array([-1.0, 1.0], F32), HEAD_DIM // 2)
    sin = sin * sign
    return jnp.tile(cos, (1, LANES // HEAD_DIM)), jnp.tile(sin, (1, LANES // HEAD_DIM))


def _encoder(x, mem, p):
    b, seq, d = x.shape
    cos, sin = _rope_tables(seq)
    h0, q, k, v, u = _proj_in(x.reshape(b * seq, d), p["ln_in_g"], p["ln_in_b"], p["w_in"], p["qg"], p["kg"],
                              cos, sin, p["head_mean"], seq)
    c = _conv_branch(u.reshape(b, seq, CONV_WIDTH), p["conv_w"], p["conv_b"], p["cln_g"], p["cln_b"])
    att = _gqa(q.reshape(ATT_HEADS, b, seq, HEAD_DIM), k.reshape(ATT_KV_HEADS, b, seq, HEAD_DIM),
               v.reshape(b, seq, 2 * KV_WIDTH))
    km, vm = _mem_kv(mem, p["ln_mem_g"], p["ln_mem_b"], p["w_ck"], p["w_cv"])
    h2, tw, route, route_t, counts = _post(att, c, h0.reshape(b, seq, d), km, vm, p["w_out"], p["ln1_g"],
                                           p["ln1_b"], p["w_cq"], p["w_co"], p["ln2_g"], p["ln2_b"],
                                           p["wr_hi"], p["wr_lo"], p["br"])
    y = _moe(tw.reshape(2, b * seq, ROW_WORDS), route.reshape(b * seq, LANES), route_t, counts,
             h2.reshape(b * seq, d), p["w_gate"], p["w_up"], p["w_down"], p["ln3_g"], p["ln3_b"])
    return y.reshape(b, seq, d)


def kernel(x_prompt, x_sample, mem_prompt, mem_sample, ln_in_g, ln_in_b, ln_mem_g, ln_mem_b, w_in, q_norm_g, k_norm_g, conv_w, conv_b, conv_ln_g, conv_ln_b, w_mix_out, ln1_g, ln1_b, w_cq, w_ck, w_cv, w_co, ln2_g, ln2_b, w_router_g, b_router_g, w_router_e, b_router_e, w_e_gate, w_e_up, w_e_down, ln3_g, ln3_b):
    l = 0
    vec = lambda a: a.reshape(1, -1).astype(F32)
    w_r = jnp.concatenate([w_router_g[l], jnp.transpose(w_router_e[l], (1, 0, 2)).reshape(D_MODEL, N_EXPERTS)], axis=1)
    w_r = jnp.pad(w_r, ((0, 0), (0, LANES - w_r.shape[1])))
    wr_hi = w_r.astype(BF16)
    wr_lo = (w_r - wr_hi.astype(F32)).astype(BF16)
    b_r = jnp.concatenate([b_router_g[l], b_router_e[l].reshape(-1)])
    b_r = jnp.pad(b_r, (0, LANES - b_r.shape[0])).reshape(1, LANES)
    head = jnp.arange(LANES) // HEAD_DIM
    head_mean = jnp.where(head[:, None] == head[None, :], 1.0 / HEAD_DIM, 0.0).astype(BF16)
    p = {
        "ln_in_g": vec(ln_in_g), "ln_in_b": vec(ln_in_b), "ln_mem_g": vec(ln_mem_g), "ln_mem_b": vec(ln_mem_b),
        "w_in": w_in[l].astype(BF16),
        "qg": jnp.tile(q_norm_g[l], LANES // HEAD_DIM).reshape(1, LANES),
        "kg": jnp.tile(k_norm_g[l], LANES // HEAD_DIM).reshape(1, LANES),
        "head_mean": head_mean,
        "conv_w": conv_w[l], "conv_b": vec(conv_b[l]), "cln_g": vec(conv_ln_g[l]), "cln_b": vec(conv_ln_b[l]),
        "w_out": w_mix_out[l].astype(BF16), "ln1_g": vec(ln1_g[l]), "ln1_b": vec(ln1_b[l]),
        "w_cq": w_cq[l].astype(BF16), "w_ck": w_ck[l].astype(BF16), "w_cv": w_cv[l].astype(BF16),
        "w_co": w_co[l].astype(BF16), "ln2_g": vec(ln2_g[l]), "ln2_b": vec(ln2_b[l]),
        "wr_hi": wr_hi, "wr_lo": wr_lo, "br": b_r,
        "w_gate": w_e_gate[l], "w_up": w_e_up[l], "w_down": w_e_down[l],
        "ln3_g": vec(ln3_g[l]), "ln3_b": vec(ln3_b[l]),
    }
    return (_encoder(x_prompt, mem_prompt, p), _encoder(x_sample, mem_sample, p))
```

```python
import functools

import jax
import jax.numpy as jnp
from jax import lax
from jax.experimental import pallas as pl
from jax.experimental.pallas import tpu as pltpu
from jax.experimental.pallas import tpu_sc as plsc

F32 = jnp.float32
BF16 = jnp.bfloat16

D_MODEL = 1024
DEPTH = 1
GRID_W = 64
N_MEM = 256
ATT_HEADS = 8
ATT_KV_HEADS = 2
HEAD_DIM = 64
ATT_WIDTH = ATT_HEADS * HEAD_DIM
KV_WIDTH = ATT_KV_HEADS * HEAD_DIM
ROPE_THETA = 10000.0
ROPE_AXIS_DIM = HEAD_DIM // 2
CONV_WIDTH = D_MODEL - ATT_WIDTH
CONV_KSIZE = 31
CONV_PAD = CONV_KSIZE // 2
IN_WIDTH = ATT_WIDTH + 2 * KV_WIDTH + 2 * CONV_WIDTH
MEM_HEADS = 4
MEM_HEAD_DIM = D_MODEL // MEM_HEADS
N_GROUPS = 4
EXPERTS_PER_GROUP = 8
N_EXPERTS = N_GROUPS * EXPERTS_PER_GROUP
D_EXPERT = 256
LN_EPS = 1e-5
RMS_EPS = 1e-6
DEEPNORM_ALPHA = (2.0 * DEPTH) ** 0.25
LOG2_E = 1.4426950408889634
Q_SCALE = HEAD_DIM ** -0.5 * LOG2_E

LANES = 128
SUBLANES = 8
HALO = 16
VMEM_LIMIT = 56 * 1024 * 1024

TM_IN = 1024
IN_SPLIT = 4
TC_CONV = 512
CONV_MXU_TAPS = 16
CONV_MXU_COLS = 256
CONV_VPU_ROWS = 256
TQ_ATT = 512
KV_CHUNK = 512
ATT_SAFE_SPAN = 50.0
NORM_CAP = HEAD_DIM ** 0.5 * (1.0 + 2.0 ** -7)
TM_POST = 1024
POST_SPLIT = 4
TRI_BLOCKS = 4
TR_EXPERT = 512
TM_COMBINE = 1024
ROW_WORDS = 256
SC_WINDOW = 128
ROUTE_E, ROUTE_W, ROUTE_RANK = 0, 2, 4
ROUTE_LANES = 8


def _layer_norm(x, g, b):
    mu = jnp.mean(x, -1, keepdims=True)
    xc = x - mu
    var = jnp.mean(xc * xc, -1, keepdims=True)
    return xc * lax.rsqrt(var + LN_EPS) * g + b


def _sigmoid(x):
    return 1.0 / (1.0 + jnp.exp(-x))


def _rope(x, cos, sin_signed):
    lane = lax.broadcasted_iota(jnp.int32, x.shape, 1)
    nxt = pltpu.roll(x, LANES - 1, axis=1)
    prv = pltpu.roll(x, 1, axis=1)
    return x * cos + jnp.where((lane & 1) == 0, nxt, prv) * sin_signed


def _proj_in_kernel(x_ref, g_ref, b_ref, w_ref, qg_ref, kg_ref, cos_ref, sin_ref, hm_ref,
                    h_ref, q_ref, k_ref, v_ref, u_ref):
    i1 = ATT_WIDTH
    i2 = i1 + KV_WIDTH
    i3 = i2 + KV_WIDTH
    i4 = i3 + CONV_WIDTH
    hm = hm_ref[...]
    heads_per_tile = LANES // HEAD_DIM
    tm = x_ref.shape[0]
    for rows in [slice(r0, r0 + tm // IN_SPLIT) for r0 in range(0, tm, tm // IN_SPLIT)]:
        h = _layer_norm(x_ref[rows, :], g_ref[...], b_ref[...])
        h_ref[rows, :] = h
        z = jnp.dot(h.astype(BF16), w_ref[...], preferred_element_type=F32)
        cos = cos_ref[rows, :]
        sin = sin_ref[rows, :]
        for j in range(ATT_WIDTH // LANES):
            qj = z[:, j * LANES:(j + 1) * LANES]
            ms = jnp.dot((qj * qj).astype(BF16), hm, preferred_element_type=F32)
            qn = qj * lax.rsqrt(ms + RMS_EPS) * qg_ref[...]
            qr = (_rope(qn, cos, sin) * Q_SCALE).astype(BF16)
            for r in range(heads_per_tile):
                q_ref[j * heads_per_tile + r, rows, :] = qr[:, r * HEAD_DIM:(r + 1) * HEAD_DIM]
        kj = z[:, i1:i2]
        ms = jnp.dot((kj * kj).astype(BF16), hm, preferred_element_type=F32)
        kn = kj * lax.rsqrt(ms + RMS_EPS) * kg_ref[...]
        kr = _rope(kn, cos, sin).astype(BF16)
        vj = z[:, i2:i3].astype(BF16)
        ones = jnp.ones((vj.shape[0], HEAD_DIM), BF16)
        for g in range(ATT_KV_HEADS):
            k_ref[g, rows, :] = kr[:, g * HEAD_DIM:(g + 1) * HEAD_DIM]
        v_ref[rows, :] = jnp.concatenate([vj[:, :HEAD_DIM], ones, vj[:, HEAD_DIM:], ones], axis=1)
        u_ref[rows, :] = (z[:, i3:i4] * _sigmoid(z[:, i4:])).astype(BF16)


def _proj_in(x, ln_g, ln_b, w_in, qg, kg, cos, sin, hm, seq):
    t = x.shape[0]
    tm = TM_IN
    nseq = seq // tm
    row = lambda i: (i, 0)
    const = lambda i: (0, 0)
    return pl.pallas_call(
        _proj_in_kernel,
        grid=(t // tm,),
        in_specs=[
            pl.BlockSpec((tm, D_MODEL), row),
            pl.BlockSpec((1, D_MODEL), const),
            pl.BlockSpec((1, D_MODEL), const),
            pl.BlockSpec((D_MODEL, IN_WIDTH), const),
            pl.BlockSpec((1, LANES), const),
            pl.BlockSpec((1, LANES), const),
            pl.BlockSpec((tm, LANES), lambda i: (i % nseq, 0)),
            pl.BlockSpec((tm, LANES), lambda i: (i % nseq, 0)),
            pl.BlockSpec((LANES, LANES), const),
        ],
        out_specs=[
            pl.BlockSpec((tm, D_MODEL), row),
            pl.BlockSpec((ATT_HEADS, tm, HEAD_DIM), lambda i: (0, i, 0)),
            pl.BlockSpec((ATT_KV_HEADS, tm, HEAD_DIM), lambda i: (0, i, 0)),
            pl.BlockSpec((tm, 2 * KV_WIDTH), row),
            pl.BlockSpec((tm, CONV_WIDTH), row),
        ],
        out_shape=[
            jax.ShapeDtypeStruct((t, D_MODEL), F32),
            jax.ShapeDtypeStruct((ATT_HEADS, t, HEAD_DIM), BF16),
            jax.ShapeDtypeStruct((ATT_KV_HEADS, t, HEAD_DIM), BF16),
            jax.ShapeDtypeStruct((t, 2 * KV_WIDTH), BF16),
            jax.ShapeDtypeStruct((t, CONV_WIDTH), BF16),
        ],
        compiler_params=pltpu.CompilerParams(dimension_semantics=("parallel",), vmem_limit_bytes=VMEM_LIMIT),
        name="proj_in",
    )(x, ln_g, ln_b, w_in, qg, kg, cos, sin, hm)


def _conv_kernel(u_ref, w_ref, wdiag_ref, cb_ref, g_ref, b_ref, o_ref, win_ref, sh_ref, *, seq):
    i = pl.program_id(1)
    tc = TC_CONV
    t0 = pl.multiple_of(i * tc, tc)
    top_start = pl.multiple_of(jnp.maximum(t0 - HALO, 0), HALO)
    bot_start = pl.multiple_of(jnp.minimum(t0 + tc, seq - HALO), HALO)
    top = u_ref[pl.ds(top_start, HALO), :].astype(F32)
    bot = u_ref[pl.ds(bot_start, HALO), :].astype(F32)
    win_ref[0:HALO, :] = jnp.where(i > 0, top, 0.0)
    win_ref[HALO:HALO + tc, :] = u_ref[pl.ds(t0, tc), :].astype(F32)
    win_ref[HALO + tc:, :] = jnp.where(i < pl.num_programs(1) - 1, bot, 0.0)
    sh_rows = sh_ref.shape[1]
    for s in range(1, SUBLANES):
        sh_ref[s] = win_ref[s:s + sh_rows, :]

    def tap(j, r0, rows, cs):
        off = r0 + HALO - CONV_PAD + j
        base, s = off - off % SUBLANES, off % SUBLANES
        return win_ref[base:base + rows, cs] if s == 0 else sh_ref[s, base:base + rows, cs]

    mxu_cols = []
    for cblk in range(CONV_WIDTH // CONV_MXU_COLS):
        cs = slice(cblk * CONV_MXU_COLS, (cblk + 1) * CONV_MXU_COLS)
        acc = None
        for j in range(CONV_MXU_TAPS):
            d = jnp.dot(tap(j, 0, tc, cs).astype(BF16), wdiag_ref[j, cblk], preferred_element_type=F32)
            acc = d if acc is None else acc + d
        mxu_cols.append(acc)
    w = w_ref[...]
    for r0 in range(0, tc, CONV_VPU_ROWS):
        cols = []
        for c in range(CONV_WIDTH // LANES):
            cs = slice(c * LANES, (c + 1) * LANES)
            lanes_in_blk = slice(c * LANES % CONV_MXU_COLS, c * LANES % CONV_MXU_COLS + LANES)
            acc = None
            for j in range(CONV_MXU_TAPS, CONV_KSIZE):
                term = tap(j, r0, CONV_VPU_ROWS, cs) * w[j:j + 1, cs]
                acc = term if acc is None else acc + term
            cols.append(acc + mxu_cols[c * LANES // CONV_MXU_COLS][r0:r0 + CONV_VPU_ROWS, lanes_in_blk])
        y = jnp.concatenate(cols, axis=1) + cb_ref[...]
        y = _layer_norm(y, g_ref[...], b_ref[...])
        o_ref[r0:r0 + CONV_VPU_ROWS, :] = (y * _sigmoid(y)).astype(BF16)


def _conv_branch(u, conv_w, conv_b, cln_g, cln_b):
    b, seq, _ = u.shape
    tc = TC_CONV
    const = lambda bi, i: (0, 0)
    nblk = CONV_WIDTH // CONV_MXU_COLS
    eye = jnp.eye(CONV_MXU_COLS, dtype=F32)
    wdiag = (conv_w[:CONV_MXU_TAPS].reshape(CONV_MXU_TAPS, nblk, 1, CONV_MXU_COLS) * eye).astype(BF16)
    return pl.pallas_call(
        functools.partial(_conv_kernel, seq=seq),
        grid=(b, seq // tc),
        in_specs=[
            pl.BlockSpec((None, seq, CONV_WIDTH), lambda bi, i: (bi, 0, 0)),
            pl.BlockSpec((CONV_KSIZE, CONV_WIDTH), const),
            pl.BlockSpec((CONV_MXU_TAPS, nblk, CONV_MXU_COLS, CONV_MXU_COLS), lambda bi, i: (0, 0, 0, 0)),
            pl.BlockSpec((1, CONV_WIDTH), const),
            pl.BlockSpec((1, CONV_WIDTH), const),
            pl.BlockSpec((1, CONV_WIDTH), const),
        ],
        out_specs=pl.BlockSpec((None, tc, CONV_WIDTH), lambda bi, i: (bi, i, 0)),
        out_shape=jax.ShapeDtypeStruct((b, seq, CONV_WIDTH), BF16),
        scratch_shapes=[pltpu.VMEM((tc + 2 * HALO, CONV_WIDTH), F32),
                        pltpu.VMEM((SUBLANES, tc + 2 * HALO - SUBLANES, CONV_WIDTH), F32)],
        compiler_params=pltpu.CompilerParams(dimension_semantics=("parallel", "arbitrary"),
                                             vmem_limit_bytes=VMEM_LIMIT),
        name="conv_branch",
    )(u, conv_w, wdiag, conv_b, cln_g, cln_b)


def _gqa_kernel(q_ref, k_ref, v_ref, o_ref, *, bounded):
    rep = ATT_HEADS // ATT_KV_HEADS
    tq = q_ref.shape[1]
    seq = k_ref.shape[1]
    n_chunks = seq // KV_CHUNK
    nt_dims = (((1,), (1,)), ((), ()))
    qs = [q_ref[g * rep:(g + 1) * rep].reshape(rep * tq, HEAD_DIM) for g in range(ATT_KV_HEADS)]

    if bounded:
        acc = [None] * ATT_KV_HEADS
        for c in range(n_chunks):
            rows = slice(c * KV_CHUNK, (c + 1) * KV_CHUNK)
            for g in range(ATT_KV_HEADS):
                s = lax.dot_general(qs[g], k_ref[g, rows], nt_dims, preferred_element_type=F32)
                d = jnp.dot(jnp.exp2(s).astype(BF16), v_ref[rows, g * LANES:(g + 1) * LANES],
                            preferred_element_type=F32)
                acc[g] = d if c == 0 else acc[g] + d
    else:
        def step(c, carry):
            rows = pl.ds(pl.multiple_of(c * KV_CHUNK, KV_CHUNK), KV_CHUNK)
            new = []
            for g in range(ATT_KV_HEADS):
                m, a = carry[g]
                s = lax.dot_general(qs[g], k_ref[g, rows], nt_dims, preferred_element_type=F32)
                m_new = jnp.maximum(m, jnp.max(s, -1, keepdims=True))
                p = jnp.exp2(s - m_new).astype(BF16)
                d = jnp.dot(p, v_ref[rows, g * LANES:(g + 1) * LANES], preferred_element_type=F32)
                new.append((m_new, jnp.exp2(m - m_new) * a + d))
            return tuple(new)

        init = tuple((jnp.full((rep * tq, 1), -jnp.inf, F32), jnp.zeros((rep * tq, LANES), F32))
                     for _ in range(ATT_KV_HEADS))
        out = lax.fori_loop(0, n_chunks, step, init)
        acc = [out[g][1] for g in range(ATT_KV_HEADS)]

    outs = []
    for g in range(ATT_KV_HEADS):
        o = acc[g] / pltpu.roll(acc[g], HEAD_DIM, axis=1)
        outs.extend(o[r * tq:(r + 1) * tq, :HEAD_DIM] for r in range(rep))
    o_ref[...] = jnp.concatenate(outs, axis=1).astype(BF16)


def _gqa(q, k, v, score_bound):
    _, b, seq, _ = q.shape
    tq = TQ_ATT

    def call(bounded):
        return pl.pallas_call(
            functools.partial(_gqa_kernel, bounded=bounded),
            grid=(b, seq // tq),
            in_specs=[
                pl.BlockSpec((ATT_HEADS, None, tq, HEAD_DIM), lambda bi, i: (0, bi, i, 0)),
                pl.BlockSpec((ATT_KV_HEADS, None, seq, HEAD_DIM), lambda bi, i: (0, bi, 0, 0)),
                pl.BlockSpec((None, seq, ATT_KV_HEADS * LANES), lambda bi, i: (bi, 0, 0)),
            ],
            out_specs=pl.BlockSpec((None, tq, ATT_WIDTH), lambda bi, i: (bi, i, 0)),
            out_shape=jax.ShapeDtypeStruct((b, seq, ATT_WIDTH), BF16),
            compiler_params=pltpu.CompilerParams(dimension_semantics=("parallel", "arbitrary"),
                                                 vmem_limit_bytes=VMEM_LIMIT),
            name="gqa" if bounded else "gqa_online",
        )

    return lax.cond(score_bound <= ATT_SAFE_SPAN, call(True), call(False), q, k, v)


def _mem_kv_kernel(m_ref, g_ref, b_ref, wk_ref, wv_ref, k_ref, v_ref):
    m = _layer_norm(m_ref[...], g_ref[...], b_ref[...]).astype(BF16)
    k_ref[...] = jnp.dot(m, wk_ref[...], preferred_element_type=F32).astype(BF16)
    v_ref[...] = jnp.dot(m, wv_ref[...], preferred_element_type=F32).astype(BF16)


def _mem_kv(mem, ln_g, ln_b, w_ck, w_cv):
    b = mem.shape[0]
    const = lambda bi: (0, 0)
    blk = pl.BlockSpec((None, N_MEM, D_MODEL), lambda bi: (bi, 0, 0))
    return pl.pallas_call(
        _mem_kv_kernel,
        grid=(b,),
        in_specs=[blk, pl.BlockSpec((1, D_MODEL), const), pl.BlockSpec((1, D_MODEL), const),
                  pl.BlockSpec((D_MODEL, D_MODEL), const), pl.BlockSpec((D_MODEL, D_MODEL), const)],
        out_specs=[blk, blk],
        out_shape=[jax.ShapeDtypeStruct((b, N_MEM, D_MODEL), BF16)] * 2,
        compiler_params=pltpu.CompilerParams(dimension_semantics=("parallel",), vmem_limit_bytes=VMEM_LIMIT),
        name="mem_kv",
    )(mem, ln_g, ln_b, w_ck, w_cv)


def _post_kernel(att_ref, c_ref, h0_ref, km_ref, vm_ref, wo_ref, g1_ref, b1_ref, wq_ref, wco_ref,
                 g2_ref, b2_ref, wr_ref, br_ref, tri_ref, h2_ref, t_ref, route_ref, route_t_ref,
                 cnt_out_ref, cnt_ref):
    tm = att_ref.shape[0]
    halves = [slice(r0, r0 + tm // POST_SPLIT) for r0 in range(0, tm, tm // POST_SPLIT)]
    mixed = [jnp.concatenate([att_ref[r, :], c_ref[r, :]], axis=1) for r in halves]
    mix = [jnp.dot(m, wo_ref[...], preferred_element_type=F32) for m in mixed]
    h1 = [_layer_norm(DEEPNORM_ALPHA * h0_ref[r, :] + m, g1_ref[...], b1_ref[...]) for r, m in zip(halves, mix)]
    qx = [(jnp.dot(h.astype(BF16), wq_ref[...], preferred_element_type=F32) * (MEM_HEAD_DIM ** -0.5)).astype(BF16)
          for h in h1]
    heads = [[] for _ in halves]
    for hd in range(MEM_HEADS):
        hs = slice(hd * MEM_HEAD_DIM, (hd + 1) * MEM_HEAD_DIM)
        for k in range(POST_SPLIT):
            s = lax.dot_general(qx[k][:, hs], km_ref[:, hs], (((1,), (1,)), ((), ())), preferred_element_type=F32)
            p = jnp.exp(s - jnp.max(s, -1, keepdims=True))
            l = jnp.sum(p, -1, keepdims=True)
            heads[k].append(jnp.dot(p.astype(BF16), vm_ref[:, hs], preferred_element_type=F32) / l)
    o = [jnp.concatenate(hk, axis=1).astype(BF16) for hk in heads]
    xa = [jnp.dot(ok, wco_ref[...], preferred_element_type=F32) for ok in o]
    h2 = jnp.concatenate([_layer_norm(DEEPNORM_ALPHA * h + x, g2_ref[...], b2_ref[...]) for h, x in zip(h1, xa)],
                         axis=0)
    h2_ref[...] = h2
    t_hi = h2.astype(BF16)
    t_rounded = t_hi.astype(F32)
    tw = _pack_bf16_pairs(t_rounded)
    for half in range(2):
        t_ref[half] = tw[:, half * ROW_WORDS:(half + 1) * ROW_WORDS]
    t_lo = (h2 - t_rounded).astype(BF16)
    both_parts = jnp.dot(t_hi, wr_ref[...], preferred_element_type=F32)
    logits = (both_parts[:, :LANES] + both_parts[:, LANES:]
              + jnp.dot(t_lo, wr_ref[:, :LANES], preferred_element_type=F32)) + br_ref[...]
    lane = lax.broadcasted_iota(jnp.int32, logits.shape, 1)
    neg = -jnp.inf
    big = jnp.int32(LANES)
    gl = jnp.where(lane < N_GROUPS, logits, neg)
    gmax = jnp.max(gl, -1, keepdims=True)
    p_grp = 1.0 / jnp.sum(jnp.exp(gl - gmax), -1, keepdims=True)
    grp = jnp.min(jnp.where(gl == gmax, lane, big), -1, keepdims=True)
    lo = N_GROUPS + EXPERTS_PER_GROUP * grp
    el = jnp.where((lane >= lo) & (lane < lo + EXPERTS_PER_GROUP), logits, neg)
    v1 = jnp.max(el, -1, keepdims=True)
    i1 = jnp.min(jnp.where(el == v1, lane, big), -1, keepdims=True)
    el2 = jnp.where(lane == i1, neg, el)
    v2 = jnp.max(el2, -1, keepdims=True)
    i2 = jnp.min(jnp.where(el2 == v2, lane, big), -1, keepdims=True)
    e21 = jnp.exp(v2 - v1)
    w1 = p_grp / (1.0 + e21)
    w2 = p_grp * e21 / (1.0 + e21)
    @pl.when((pl.program_id(0) == 0) & (pl.program_id(1) == 0))
    def _():
        cnt_ref[...] = jnp.zeros_like(cnt_ref)

    e1 = i1 - N_GROUPS
    e2 = i2 - N_GROUPS
    hot1 = jnp.where(lane == e1, 1.0, 0.0)
    hot2 = jnp.where(lane == e2, 1.0, 0.0)
    both = hot1 + hot2
    both16 = both.astype(BF16)
    blk = tri_ref.shape[0] // TRI_BLOCKS
    before = jnp.concatenate(
        [jnp.dot(tri_ref[r * blk:(r + 1) * blk, :(r + 1) * blk], both16[:(r + 1) * blk], preferred_element_type=F32)
         for r in range(TRI_BLOCKS)], axis=0) + cnt_ref[...]
    rank1 = jnp.sum(hot1 * before, -1, keepdims=True)
    rank2 = jnp.sum(hot2 * before, -1, keepdims=True)
    cnt_ref[...] += jnp.sum(both, 0, keepdims=True)
    cnt_out_ref[...] = cnt_ref[...]
    route = jnp.zeros(logits.shape, F32)
    for k, val in enumerate((e1.astype(F32), e2.astype(F32), w1, w2, rank1, rank2)):
        route = jnp.where(lane == k, val, route)
    route_ref[...] = route
    route_t_ref[...] = jnp.transpose(route)[:ROUTE_LANES, :]


def _post(att, c, h0, km, vm, w_out, g1, b1, w_cq, w_co, g2, b2, wr_hi, wr_lo, br):
    b, seq, _ = att.shape
    tm = TM_POST
    tok = lambda width: pl.BlockSpec((None, tm, width), lambda bi, i: (bi, i, 0))
    memb = pl.BlockSpec((None, N_MEM, D_MODEL), lambda bi, i: (bi, 0, 0))
    const = lambda shape: pl.BlockSpec(shape, lambda bi, i: (0, 0))
    vec = const((1, D_MODEL))
    sq = const((D_MODEL, D_MODEL))
    idx = jnp.arange(tm, dtype=jnp.int32)
    tri = jnp.where(idx[:, None] > idx[None, :], 1.0, 0.0).astype(BF16)
    return pl.pallas_call(
        _post_kernel,
        grid=(b, seq // tm),
        in_specs=[tok(ATT_WIDTH), tok(CONV_WIDTH), tok(D_MODEL), memb, memb, sq, vec, vec, sq, sq, vec, vec,
                  const((D_MODEL, 2 * LANES)), const((1, LANES)), const((tm, tm))],
        out_specs=[tok(D_MODEL),
                   pl.BlockSpec((2, None, tm, ROW_WORDS), lambda bi, i: (0, bi, i, 0)),
                   tok(LANES),
                   pl.BlockSpec((ROUTE_LANES, tm), lambda bi, i: (0, bi * (seq // tm) + i)),
                   const((1, LANES))],
        out_shape=[jax.ShapeDtypeStruct((b, seq, D_MODEL), F32),
                   jax.ShapeDtypeStruct((2, b, seq, ROW_WORDS), jnp.uint32),
                   jax.ShapeDtypeStruct((b, seq, LANES), F32),
                   jax.ShapeDtypeStruct((ROUTE_LANES, b * seq), F32),
                   jax.ShapeDtypeStruct((1, LANES), F32)],
        scratch_shapes=[pltpu.VMEM((1, LANES), F32)],
        compiler_params=pltpu.CompilerParams(dimension_semantics=("arbitrary", "arbitrary"),
                                             vmem_limit_bytes=VMEM_LIMIT),
        name="post",
    )(att, c, h0, km, vm, w_out, g1, b1, w_cq, w_co, g2, b2, jnp.concatenate([wr_hi, wr_lo], axis=1), br, tri)


def _pack_bf16_pairs(x):
    k = x.shape[1] // 2
    bits = lax.bitcast_convert_type(x, jnp.uint32)
    return (bits[:, :k] >> 16) | (bits[:, k:] & jnp.uint32(0xFFFF0000))


def _unpack_bf16_pairs(w):
    lo = lax.bitcast_convert_type(w << 16, F32)
    hi = lax.bitcast_convert_type(w & jnp.uint32(0xFFFF0000), F32)
    return jnp.concatenate([lo, hi], axis=1)


def _sc_mesh():
    return plsc.VectorSubcoreMesh(core_axis_name="c", subcore_axis_name="s")


def _sc_scatter2(x, idx_a, idx_b, n_out):
    m = x.shape[0]

    @pl.kernel(out_type=jax.ShapeDtypeStruct((n_out, ROW_WORDS), x.dtype), mesh=_sc_mesh(), scratch_types=[])
    def scatter(x_hbm, ia_hbm, ib_hbm, o_hbm):
        def body(x_vmem, ia_vmem, ib_vmem):
            pltpu.sync_copy(x_vmem, o_hbm.at[ia_vmem.at[0]])
            pltpu.sync_copy(x_vmem, o_hbm.at[ib_vmem.at[0]])

        pltpu.emit_pipeline(
            body, grid=(m // SC_WINDOW,),
            in_specs=[pl.BlockSpec((SC_WINDOW, ROW_WORDS), lambda i: (i, 0)),
                      pl.BlockSpec((1, SC_WINDOW), lambda i: (0, i)),
                      pl.BlockSpec((1, SC_WINDOW), lambda i: (0, i))],
            out_specs=[],
            core_axis_name=("c", "s"), dimension_semantics=(pltpu.PARALLEL,),
        )(x_hbm, ia_hbm, ib_hbm)

    return scatter(x, idx_a.reshape(1, m), idx_b.reshape(1, m))


def _sc_gather(x, idx):
    m = idx.shape[0]

    @pl.kernel(out_type=jax.ShapeDtypeStruct((m, ROW_WORDS), x.dtype), mesh=_sc_mesh(), scratch_types=[])
    def gather(x_hbm, i_hbm, o_hbm):
        def body(i_vmem, o_vmem):
            pltpu.sync_copy(x_hbm.at[i_vmem.at[0]], o_vmem)

        pltpu.emit_pipeline(
            body, grid=(m // SC_WINDOW,),
            in_specs=[pl.BlockSpec((1, SC_WINDOW), lambda i: (0, i))],
            out_specs=[pl.BlockSpec((SC_WINDOW, ROW_WORDS), lambda i: (i, 0))],
            core_axis_name=("c", "s"), dimension_semantics=(pltpu.PARALLEL,),
        )(i_hbm, o_hbm)

    return gather(x, idx.reshape(1, m))


def _experts_kernel(tile_expert_ref, n_used_ref, x_ref, wg32_ref, wu32_ref, wd32_ref, y_ref, wg_ref, wu_ref, wd_ref):
    j = pl.program_id(0)
    prev = tile_expert_ref[jnp.maximum(j - 1, 0)]

    @pl.when((j == 0) | (tile_expert_ref[j] != prev))
    def _():
        wg_ref[...] = wg32_ref[...].astype(BF16)
        wu_ref[...] = wu32_ref[...].astype(BF16)
        wd_ref[...] = wd32_ref[...].astype(BF16)

    @pl.when(j < n_used_ref[0])
    def _():
        u0 = _unpack_bf16_pairs(x_ref[0])
        u1 = _unpack_bf16_pairs(x_ref[1])
        x = jnp.concatenate([u0[:, :ROW_WORDS], u1[:, :ROW_WORDS], u0[:, ROW_WORDS:], u1[:, ROW_WORDS:]],
                            axis=1).astype(BF16)
        gate = jnp.dot(x, wg_ref[...], preferred_element_type=F32)
        up = jnp.dot(x, wu_ref[...], preferred_element_type=F32)
        he = (gate * _sigmoid(gate) * up).astype(BF16)
        y = jnp.dot(he, wd_ref[...], preferred_element_type=F32)
        yw = _pack_bf16_pairs(y.astype(BF16).astype(F32))
        for half in range(2):
            y_ref[half] = yw[:, half * ROW_WORDS:(half + 1) * ROW_WORDS]


def _experts(xs, tile_expert, n_used, w_gate, w_up, w_down):
    _, rows, _ = xs.shape
    tr = TR_EXPERT
    halves = pl.BlockSpec((2, tr, ROW_WORDS), lambda j, te, nu: (0, jnp.minimum(j, nu[0] - 1), 0))
    return pl.pallas_call(
        _experts_kernel,
        grid_spec=pltpu.PrefetchScalarGridSpec(
            num_scalar_prefetch=2,
            grid=(rows // tr,),
            in_specs=[halves,
                      pl.BlockSpec((None, D_MODEL, D_EXPERT), lambda j, te, nu: (te[j], 0, 0)),
                      pl.BlockSpec((None, D_MODEL, D_EXPERT), lambda j, te, nu: (te[j], 0, 0)),
                      pl.BlockSpec((None, D_EXPERT, D_MODEL), lambda j, te, nu: (te[j], 0, 0))],
            out_specs=halves,
            scratch_shapes=[pltpu.VMEM((D_MODEL, D_EXPERT), BF16), pltpu.VMEM((D_MODEL, D_EXPERT), BF16),
                            pltpu.VMEM((D_EXPERT, D_MODEL), BF16)],
        ),
        out_shape=jax.ShapeDtypeStruct(xs.shape, jnp.uint32),
        compiler_params=pltpu.CompilerParams(dimension_semantics=("arbitrary",), vmem_limit_bytes=VMEM_LIMIT),
        name="experts",
    )(tile_expert, n_used, xs, w_gate, w_up, w_down)


def _combine_kernel(h2_ref, y1_ref, y2_ref, route_ref, g3_ref, b3_ref, o_ref):
    route = route_ref[...]
    w1 = route[:, ROUTE_W:ROUTE_W + 1]
    w2 = route[:, ROUTE_W + 1:ROUTE_W + 2]
    halves = []
    for half in range(2):
        halves.append(w1 * _unpack_bf16_pairs(y1_ref[half]) + w2 * _unpack_bf16_pairs(y2_ref[half]))
    ff = jnp.concatenate([halves[0][:, :ROW_WORDS], halves[1][:, :ROW_WORDS],
                          halves[0][:, ROW_WORDS:], halves[1][:, ROW_WORDS:]], axis=1)
    o_ref[...] = _layer_norm(DEEPNORM_ALPHA * h2_ref[...] + ff, g3_ref[...], b3_ref[...])


def _combine(h2, y1, y2, route, g3, b3):
    n = h2.shape[0]
    tm = TM_COMBINE
    row = lambda width: pl.BlockSpec((tm, width), lambda i: (i, 0))
    halves = pl.BlockSpec((2, tm, ROW_WORDS), lambda i: (0, i, 0))
    vec = pl.BlockSpec((1, D_MODEL), lambda i: (0, 0))
    return pl.pallas_call(
        _combine_kernel,
        grid=(n // tm,),
        in_specs=[row(D_MODEL), halves, halves, row(LANES), vec, vec],
        out_specs=row(D_MODEL),
        out_shape=jax.ShapeDtypeStruct((n, D_MODEL), F32),
        compiler_params=pltpu.CompilerParams(dimension_semantics=("parallel",), vmem_limit_bytes=VMEM_LIMIT),
        name="combine",
    )(h2, y1, y2, route, g3, b3)


def _moe(tw, route, route_t, counts, h2, w_gate, w_up, w_down, g3, b3):
    n = h2.shape[0]
    tr = TR_EXPERT
    cap = 2 * n + N_EXPERTS * tr
    cnt = counts[0, :N_EXPERTS].astype(jnp.int32)
    seg = (cnt + tr - 1) // tr * tr
    ends = jnp.cumsum(seg)
    starts = ends - seg
    tile_start = jnp.arange(cap // tr, dtype=jnp.int32) * tr
    tile_expert = jnp.minimum(jnp.sum(tile_start[:, None] >= ends[None, :], axis=1, dtype=jnp.int32), N_EXPERTS - 1)
    n_used = (ends[-1] // tr).reshape(1)

    def half_row_index(k):
        expert = route_t[ROUTE_E + k].astype(jnp.int32)
        hot = expert[None, :] == jnp.arange(N_EXPERTS, dtype=jnp.int32)[:, None]
        pos = route_t[ROUTE_RANK + k].astype(jnp.int32) + jnp.sum(jnp.where(hot, starts[:, None], 0), axis=0)
        return jnp.concatenate([pos, cap + pos])

    idx1, idx2 = half_row_index(0), half_row_index(1)
    xs = _sc_scatter2(tw.reshape(2 * n, ROW_WORDS), idx1, idx2, 2 * cap)
    ys = _experts(xs.reshape(2, cap, ROW_WORDS), tile_expert, n_used, w_gate, w_up, w_down)
    ys = ys.reshape(2 * cap, ROW_WORDS)
    y1 = _sc_gather(ys, idx1).reshape(2, n, ROW_WORDS)
    y2 = _sc_gather(ys, idx2).reshape(2, n, ROW_WORDS)
    return _combine(h2, y1, y2, route, g3, b3)


def _rope_tables(seq):
    rows = seq // GRID_W
    row = jnp.repeat(jnp.arange(rows, dtype=F32), GRID_W)
    col = jnp.tile(jnp.arange(GRID_W, dtype=F32), rows)
    inv_freq = ROPE_THETA ** (-jnp.arange(0, ROPE_AXIS_DIM, 2, dtype=F32) / ROPE_AXIS_DIM)
    ang = jnp.concatenate([row[:, None] * inv_freq, col[:, None] * inv_freq], -1)
    cos = jnp.repeat(jnp.cos(ang), 2, axis=-1)
    sin = jnp.repeat(jnp.sin(ang), 2, axis=-1)
    sign = jnp.tile(jnp.array([-1.0, 1.0], F32), HEAD_DIM // 2)
    sin = sin * sign
    return jnp.tile(cos, (1, LANES // HEAD_DIM)), jnp.tile(sin, (1, LANES // HEAD_DIM))


def _encoder(x, mem, p):
    b, seq, d = x.shape
    cos, sin = _rope_tables(seq)
    h0, q, k, v, u = _proj_in(x.reshape(b * seq, d), p["ln_in_g"], p["ln_in_b"], p["w_in"], p["qg"], p["kg"],
                              cos, sin, p["head_mean"], seq)
    c = _conv_branch(u.reshape(b, seq, CONV_WIDTH), p["conv_w"], p["conv_b"], p["cln_g"], p["cln_b"])
    att = _gqa(q.reshape(ATT_HEADS, b, seq, HEAD_DIM), k.reshape(ATT_KV_HEADS, b, seq, HEAD_DIM),
               v.reshape(b, seq, 2 * KV_WIDTH), p["score_bound"])
    km, vm = _mem_kv(mem, p["ln_mem_g"], p["ln_mem_b"], p["w_ck"], p["w_cv"])
    h2, tw, route, route_t, counts = _post(att, c, h0.reshape(b, seq, d), km, vm, p["w_out"], p["ln1_g"],
                                           p["ln1_b"], p["w_cq"], p["w_co"], p["ln2_g"], p["ln2_b"],
                                           p["wr_hi"], p["wr_lo"], p["br"])
    y = _moe(tw.reshape(2, b * seq, ROW_WORDS), route.reshape(b * seq, LANES), route_t, counts,
             h2.reshape(b * seq, d), p["w_gate"], p["w_up"], p["w_down"], p["ln3_g"], p["ln3_b"])
    return y.reshape(b, seq, d)


def kernel(x_prompt, x_sample, mem_prompt, mem_sample, ln_in_g, ln_in_b, ln_mem_g, ln_mem_b, w_in, q_norm_g, k_norm_g, conv_w, conv_b, conv_ln_g, conv_ln_b, w_mix_out, ln1_g, ln1_b, w_cq, w_ck, w_cv, w_co, ln2_g, ln2_b, w_router_g, b_router_g, w_router_e, b_router_e, w_e_gate, w_e_up, w_e_down, ln3_g, ln3_b):
    l = 0
    vec = lambda a: a.reshape(1, -1).astype(F32)
    w_r = jnp.concatenate([w_router_g[l], jnp.transpose(w_router_e[l], (1, 0, 2)).reshape(D_MODEL, N_EXPERTS)], axis=1)
    w_r = jnp.pad(w_r, ((0, 0), (0, LANES - w_r.shape[1])))
    wr_hi = w_r.astype(BF16)
    wr_lo = (w_r - wr_hi.astype(F32)).astype(BF16)
    b_r = jnp.concatenate([b_router_g[l], b_router_e[l].reshape(-1)])
    b_r = jnp.pad(b_r, (0, LANES - b_r.shape[0])).reshape(1, LANES)
    head = jnp.arange(LANES) // HEAD_DIM
    head_mean = jnp.where(head[:, None] == head[None, :], 1.0 / HEAD_DIM, 0.0).astype(BF16)
    p = {
        "ln_in_g": vec(ln_in_g), "ln_in_b": vec(ln_in_b), "ln_mem_g": vec(ln_mem_g), "ln_mem_b": vec(ln_mem_b),
        "w_in": w_in[l].astype(BF16),
        "qg": jnp.tile(q_norm_g[l], LANES // HEAD_DIM).reshape(1, LANES),
        "kg": jnp.tile(k_norm_g[l], LANES // HEAD_DIM).reshape(1, LANES),
        "head_mean": head_mean,
        "score_bound": (NORM_CAP * jnp.max(jnp.abs(q_norm_g[l])) * Q_SCALE) * (NORM_CAP * jnp.max(jnp.abs(k_norm_g[l]))),
        "conv_w": conv_w[l], "conv_b": vec(conv_b[l]), "cln_g": vec(conv_ln_g[l]), "cln_b": vec(conv_ln_b[l]),
        "w_out": w_mix_out[l].astype(BF16), "ln1_g": vec(ln1_g[l]), "ln1_b": vec(ln1_b[l]),
        "w_cq": w_cq[l].astype(BF16), "w_ck": w_ck[l].astype(BF16), "w_cv": w_cv[l].astype(BF16),
        "w_co": w_co[l].astype(BF16), "ln2_g": vec(ln2_g[l]), "ln2_b": vec(ln2_b[l]),
        "wr_hi": wr_hi, "wr_lo": wr_lo, "br": b_r,
        "w_gate": w_e_gate[l], "w_up": w_e_up[l], "w_down": w_e_down[l],
        "ln3_g": vec(ln3_g[l]), "ln3_b": vec(ln3_b[l]),
    }
    return (_encoder(x_prompt, mem_prompt, p), _encoder(x_sample, mem_sample, p))
```

```python
import functools

import jax
import jax.numpy as jnp
from jax import lax
from jax.experimental import pallas as pl
from jax.experimental.pallas import tpu as pltpu
from jax.experimental.pallas import tpu_sc as plsc

F32 = jnp.float32
BF16 = jnp.bfloat16

D_MODEL = 1024
DEPTH = 1
GRID_W = 64
N_MEM = 256
ATT_HEADS = 8
ATT_KV_HEADS = 2
HEAD_DIM = 64
ATT_WIDTH = ATT_HEADS * HEAD_DIM
KV_WIDTH = ATT_KV_HEADS * HEAD_DIM
ROPE_THETA = 10000.0
ROPE_AXIS_DIM = HEAD_DIM // 2
CONV_WIDTH = D_MODEL - ATT_WIDTH
CONV_KSIZE = 31
CONV_PAD = CONV_KSIZE // 2
IN_WIDTH = ATT_WIDTH + 2 * KV_WIDTH + 2 * CONV_WIDTH
MEM_HEADS = 4
MEM_HEAD_DIM = D_MODEL // MEM_HEADS
N_GROUPS = 4
EXPERTS_PER_GROUP = 8
N_EXPERTS = N_GROUPS * EXPERTS_PER_GROUP
D_EXPERT = 256
LN_EPS = 1e-5
RMS_EPS = 1e-6
DEEPNORM_ALPHA = (2.0 * DEPTH) ** 0.25
LOG2_E = 1.4426950408889634
Q_SCALE = HEAD_DIM ** -0.5 * LOG2_E

LANES = 128
SUBLANES = 8
HALO = 16
VMEM_LIMIT = 56 * 1024 * 1024

TM_IN = 1024
IN_SPLIT = 4
TC_CONV = 512
CONV_MXU_TAPS = 16
CONV_MXU_COLS = 256
CONV_VPU_ROWS = 256
TQ_ATT = 512
KV_CHUNK = 512
ATT_SAFE_SPAN = 50.0
NORM_CAP = HEAD_DIM ** 0.5 * (1.0 + 2.0 ** -7)
TM_POST = 1024
POST_SPLIT = 4
TRI_BLOCKS = 4
TR_EXPERT = 1024
TM_COMBINE = 1024
ROW_WORDS = 256
SC_WINDOW = 128
ROUTE_E, ROUTE_W, ROUTE_RANK = 0, 2, 4
ROUTE_LANES = 8


def _layer_norm(x, g, b):
    mu = jnp.mean(x, -1, keepdims=True)
    xc = x - mu
    var = jnp.mean(xc * xc, -1, keepdims=True)
    return xc * lax.rsqrt(var + LN_EPS) * g + b


def _sigmoid(x):
    return 1.0 / (1.0 + jnp.exp(-x))


def _rope(x, cos, sin_signed):
    lane = lax.broadcasted_iota(jnp.int32, x.shape, 1)
    nxt = pltpu.roll(x, LANES - 1, axis=1)
    prv = pltpu.roll(x, 1, axis=1)
    return x * cos + jnp.where((lane & 1) == 0, nxt, prv) * sin_signed


def _proj_in_kernel(x_ref, g_ref, b_ref, w_ref, qg_ref, kg_ref, cos_ref, sin_ref, hm_ref,
                    h_ref, q_ref, k_ref, v_ref, u_ref):
    i1 = ATT_WIDTH
    i2 = i1 + KV_WIDTH
    i3 = i2 + KV_WIDTH
    i4 = i3 + CONV_WIDTH
    hm = hm_ref[...]
    heads_per_tile = LANES // HEAD_DIM
    tm = x_ref.shape[0]
    for rows in [slice(r0, r0 + tm // IN_SPLIT) for r0 in range(0, tm, tm // IN_SPLIT)]:
        h = _layer_norm(x_ref[rows, :], g_ref[...], b_ref[...])
        h_ref[rows, :] = h
        z = jnp.dot(h.astype(BF16), w_ref[...], preferred_element_type=F32)
        cos = cos_ref[rows, :]
        sin = sin_ref[rows, :]
        for j in range(ATT_WIDTH // LANES):
            qj = z[:, j * LANES:(j + 1) * LANES]
            ms = jnp.dot((qj * qj).astype(BF16), hm, preferred_element_type=F32)
            qn = qj * lax.rsqrt(ms + RMS_EPS) * qg_ref[...]
            qr = (_rope(qn, cos, sin) * Q_SCALE).astype(BF16)
            for r in range(heads_per_tile):
                q_ref[j * heads_per_tile + r, rows, :] = qr[:, r * HEAD_DIM:(r + 1) * HEAD_DIM]
        kj = z[:, i1:i2]
        ms = jnp.dot((kj * kj).astype(BF16), hm, preferred_element_type=F32)
        kn = kj * lax.rsqrt(ms + RMS_EPS) * kg_ref[...]
        kr = _rope(kn, cos, sin).astype(BF16)
        vj = z[:, i2:i3].astype(BF16)
        ones = jnp.ones((vj.shape[0], HEAD_DIM), BF16)
        for g in range(ATT_KV_HEADS):
            k_ref[g, rows, :] = kr[:, g * HEAD_DIM:(g + 1) * HEAD_DIM]
        v_ref[rows, :] = jnp.concatenate([vj[:, :HEAD_DIM], ones, vj[:, HEAD_DIM:], ones], axis=1)
        u_ref[rows, :] = (z[:, i3:i4] * _sigmoid(z[:, i4:])).astype(BF16)


def _proj_in(x, ln_g, ln_b, w_in, qg, kg, cos, sin, hm, seq):
    t = x.shape[0]
    tm = TM_IN
    nseq = seq // tm
    row = lambda i: (i, 0)
    const = lambda i: (0, 0)
    return pl.pallas_call(
        _proj_in_kernel,
        grid=(t // tm,),
        in_specs=[
            pl.BlockSpec((tm, D_MODEL), row),
            pl.BlockSpec((1, D_MODEL), const),
            pl.BlockSpec((1, D_MODEL), const),
            pl.BlockSpec((D_MODEL, IN_WIDTH), const),
            pl.BlockSpec((1, LANES), const),
            pl.BlockSpec((1, LANES), const),
            pl.BlockSpec((tm, LANES), lambda i: (i % nseq, 0)),
            pl.BlockSpec((tm, LANES), lambda i: (i % nseq, 0)),
            pl.BlockSpec((LANES, LANES), const),
        ],
        out_specs=[
            pl.BlockSpec((tm, D_MODEL), row),
            pl.BlockSpec((ATT_HEADS, tm, HEAD_DIM), lambda i: (0, i, 0)),
            pl.BlockSpec((ATT_KV_HEADS, tm, HEAD_DIM), lambda i: (0, i, 0)),
            pl.BlockSpec((tm, 2 * KV_WIDTH), row),
            pl.BlockSpec((tm, CONV_WIDTH), row),
        ],
        out_shape=[
            jax.ShapeDtypeStruct((t, D_MODEL), F32),
            jax.ShapeDtypeStruct((ATT_HEADS, t, HEAD_DIM), BF16),
            jax.ShapeDtypeStruct((ATT_KV_HEADS, t, HEAD_DIM), BF16),
            jax.ShapeDtypeStruct((t, 2 * KV_WIDTH), BF16),
            jax.ShapeDtypeStruct((t, CONV_WIDTH), BF16),
        ],
        compiler_params=pltpu.CompilerParams(dimension_semantics=("parallel",), vmem_limit_bytes=VMEM_LIMIT),
        name="proj_in",
    )(x, ln_g, ln_b, w_in, qg, kg, cos, sin, hm)


def _conv_kernel(u_ref, w_ref, wdiag_ref, cb_ref, g_ref, b_ref, o_ref, win_ref, sh_ref, *, seq):
    i = pl.program_id(1)
    tc = TC_CONV
    t0 = pl.multiple_of(i * tc, tc)
    top_start = pl.multiple_of(jnp.maximum(t0 - HALO, 0), HALO)
    bot_start = pl.multiple_of(jnp.minimum(t0 + tc, seq - HALO), HALO)
    top = u_ref[pl.ds(top_start, HALO), :].astype(F32)
    bot = u_ref[pl.ds(bot_start, HALO), :].astype(F32)
    win_ref[0:HALO, :] = jnp.where(i > 0, top, 0.0)
    win_ref[HALO:HALO + tc, :] = u_ref[pl.ds(t0, tc), :].astype(F32)
    win_ref[HALO + tc:, :] = jnp.where(i < pl.num_programs(1) - 1, bot, 0.0)
    sh_rows = sh_ref.shape[1]
    for s in range(1, SUBLANES):
        sh_ref[s] = win_ref[s:s + sh_rows, :]

    def tap(j, r0, rows, cs):
        off = r0 + HALO - CONV_PAD + j
        base, s = off - off % SUBLANES, off % SUBLANES
        return win_ref[base:base + rows, cs] if s == 0 else sh_ref[s, base:base + rows, cs]

    mxu_cols = []
    for cblk in range(CONV_WIDTH // CONV_MXU_COLS):
        cs = slice(cblk * CONV_MXU_COLS, (cblk + 1) * CONV_MXU_COLS)
        acc = None
        for j in range(CONV_MXU_TAPS):
            d = jnp.dot(tap(j, 0, tc, cs).astype(BF16), wdiag_ref[j, cblk], preferred_element_type=F32)
            acc = d if acc is None else acc + d
        mxu_cols.append(acc)
    w = w_ref[...]
    for r0 in range(0, tc, CONV_VPU_ROWS):
        cols = []
        for c in range(CONV_WIDTH // LANES):
            cs = slice(c * LANES, (c + 1) * LANES)
            lanes_in_blk = slice(c * LANES % CONV_MXU_COLS, c * LANES % CONV_MXU_COLS + LANES)
            acc = None
            for j in range(CONV_MXU_TAPS, CONV_KSIZE):
                term = tap(j, r0, CONV_VPU_ROWS, cs) * w[j:j + 1, cs]
                acc = term if acc is None else acc + term
            cols.append(acc + mxu_cols[c * LANES // CONV_MXU_COLS][r0:r0 + CONV_VPU_ROWS, lanes_in_blk])
        y = jnp.concatenate(cols, axis=1) + cb_ref[...]
        y = _layer_norm(y, g_ref[...], b_ref[...])
        o_ref[r0:r0 + CONV_VPU_ROWS, :] = (y * _sigmoid(y)).astype(BF16)


def _conv_branch(u, conv_w, conv_b, cln_g, cln_b):
    b, seq, _ = u.shape
    tc = TC_CONV
    const = lambda bi, i: (0, 0)
    nblk = CONV_WIDTH // CONV_MXU_COLS
    eye = jnp.eye(CONV_MXU_COLS, dtype=F32)
    wdiag = (conv_w[:CONV_MXU_TAPS].reshape(CONV_MXU_TAPS, nblk, 1, CONV_MXU_COLS) * eye).astype(BF16)
    return pl.pallas_call(
        functools.partial(_conv_kernel, seq=seq),
        grid=(b, seq // tc),
        in_specs=[
            pl.BlockSpec((None, seq, CONV_WIDTH), lambda bi, i: (bi, 0, 0)),
            pl.BlockSpec((CONV_KSIZE, CONV_WIDTH), const),
            pl.BlockSpec((CONV_MXU_TAPS, nblk, CONV_MXU_COLS, CONV_MXU_COLS), lambda bi, i: (0, 0, 0, 0)),
            pl.BlockSpec((1, CONV_WIDTH), const),
            pl.BlockSpec((1, CONV_WIDTH), const),
            pl.BlockSpec((1, CONV_WIDTH), const),
        ],
        out_specs=pl.BlockSpec((None, tc, CONV_WIDTH), lambda bi, i: (bi, i, 0)),
        out_shape=jax.ShapeDtypeStruct((b, seq, CONV_WIDTH), BF16),
        scratch_shapes=[pltpu.VMEM((tc + 2 * HALO, CONV_WIDTH), F32),
                        pltpu.VMEM((SUBLANES, tc + 2 * HALO - SUBLANES, CONV_WIDTH), F32)],
        compiler_params=pltpu.CompilerParams(dimension_semantics=("parallel", "arbitrary"),
                                             vmem_limit_bytes=VMEM_LIMIT),
        name="conv_branch",
    )(u, conv_w, wdiag, conv_b, cln_g, cln_b)


def _gqa_kernel(q_ref, k_ref, v_ref, o_ref, *, bounded):
    rep = ATT_HEADS // ATT_KV_HEADS
    tq = q_ref.shape[1]
    seq = k_ref.shape[1]
    n_chunks = seq // KV_CHUNK
    nt_dims = (((1,), (1,)), ((), ()))
    qs = [q_ref[g * rep:(g + 1) * rep].reshape(rep * tq, HEAD_DIM) for g in range(ATT_KV_HEADS)]

    if bounded:
        acc = [None] * ATT_KV_HEADS
        for c in range(n_chunks):
            rows = slice(c * KV_CHUNK, (c + 1) * KV_CHUNK)
            for g in range(ATT_KV_HEADS):
                s = lax.dot_general(qs[g], k_ref[g, rows], nt_dims, preferred_element_type=F32)
                d = jnp.dot(jnp.exp2(s).astype(BF16), v_ref[rows, g * LANES:(g + 1) * LANES],
                            preferred_element_type=F32)
                acc[g] = d if c == 0 else acc[g] + d
    else:
        def step(c, carry):
            rows = pl.ds(pl.multiple_of(c * KV_CHUNK, KV_CHUNK), KV_CHUNK)
            new = []
            for g in range(ATT_KV_HEADS):
                m, a = carry[g]
                s = lax.dot_general(qs[g], k_ref[g, rows], nt_dims, preferred_element_type=F32)
                m_new = jnp.maximum(m, jnp.max(s, -1, keepdims=True))
                p = jnp.exp2(s - m_new).astype(BF16)
                d = jnp.dot(p, v_ref[rows, g * LANES:(g + 1) * LANES], preferred_element_type=F32)
                new.append((m_new, jnp.exp2(m - m_new) * a + d))
            return tuple(new)

        init = tuple((jnp.full((rep * tq, 1), -jnp.inf, F32), jnp.zeros((rep * tq, LANES), F32))
                     for _ in range(ATT_KV_HEADS))
        out = lax.fori_loop(0, n_chunks, step, init)
        acc = [out[g][1] for g in range(ATT_KV_HEADS)]

    outs = []
    for g in range(ATT_KV_HEADS):
        o = acc[g] / pltpu.roll(acc[g], HEAD_DIM, axis=1)
        outs.extend(o[r * tq:(r + 1) * tq, :HEAD_DIM] for r in range(rep))
    o_ref[...] = jnp.concatenate(outs, axis=1).astype(BF16)


def _gqa(q, k, v, score_bound):
    _, b, seq, _ = q.shape
    tq = TQ_ATT

    def call(bounded):
        return pl.pallas_call(
            functools.partial(_gqa_kernel, bounded=bounded),
            grid=(b, seq // tq),
            in_specs=[
                pl.BlockSpec((ATT_HEADS, None, tq, HEAD_DIM), lambda bi, i: (0, bi, i, 0)),
                pl.BlockSpec((ATT_KV_HEADS, None, seq, HEAD_DIM), lambda bi, i: (0, bi, 0, 0)),
                pl.BlockSpec((None, seq, ATT_KV_HEADS * LANES), lambda bi, i: (bi, 0, 0)),
            ],
            out_specs=pl.BlockSpec((None, tq, ATT_WIDTH), lambda bi, i: (bi, i, 0)),
            out_shape=jax.ShapeDtypeStruct((b, seq, ATT_WIDTH), BF16),
            compiler_params=pltpu.CompilerParams(dimension_semantics=("parallel", "arbitrary"),
                                                 vmem_limit_bytes=VMEM_LIMIT),
            name="gqa" if bounded else "gqa_online",
        )

    return lax.cond(score_bound <= ATT_SAFE_SPAN, call(True), call(False), q, k, v)


def _mem_kv_kernel(m_ref, g_ref, b_ref, wk_ref, wv_ref, k_ref, v_ref):
    m = _layer_norm(m_ref[...], g_ref[...], b_ref[...]).astype(BF16)
    k_ref[...] = jnp.dot(m, wk_ref[...], preferred_element_type=F32).astype(BF16)
    v_ref[...] = jnp.dot(m, wv_ref[...], preferred_element_type=F32).astype(BF16)


def _mem_kv(mem, ln_g, ln_b, w_ck, w_cv):
    b = mem.shape[0]
    const = lambda bi: (0, 0)
    blk = pl.BlockSpec((None, N_MEM, D_MODEL), lambda bi: (bi, 0, 0))
    return pl.pallas_call(
        _mem_kv_kernel,
        grid=(b,),
        in_specs=[blk, pl.BlockSpec((1, D_MODEL), const), pl.BlockSpec((1, D_MODEL), const),
                  pl.BlockSpec((D_MODEL, D_MODEL), const), pl.BlockSpec((D_MODEL, D_MODEL), const)],
        out_specs=[blk, blk],
        out_shape=[jax.ShapeDtypeStruct((b, N_MEM, D_MODEL), BF16)] * 2,
        compiler_params=pltpu.CompilerParams(dimension_semantics=("parallel",), vmem_limit_bytes=VMEM_LIMIT),
        name="mem_kv",
    )(mem, ln_g, ln_b, w_ck, w_cv)


def _post_kernel(att_ref, c_ref, h0_ref, km_ref, vm_ref, wo_ref, g1_ref, b1_ref, wq_ref, wco_ref,
                 g2_ref, b2_ref, wr_ref, br_ref, tri_ref, h2_ref, t_ref, route_ref, route_t_ref,
                 cnt_out_ref, cnt_ref):
    tm = att_ref.shape[0]
    halves = [slice(r0, r0 + tm // POST_SPLIT) for r0 in range(0, tm, tm // POST_SPLIT)]
    mixed = [jnp.concatenate([att_ref[r, :], c_ref[r, :]], axis=1) for r in halves]
    mix = [jnp.dot(m, wo_ref[...], preferred_element_type=F32) for m in mixed]
    h1 = [_layer_norm(DEEPNORM_ALPHA * h0_ref[r, :] + m, g1_ref[...], b1_ref[...]) for r, m in zip(halves, mix)]
    qx = [(jnp.dot(h.astype(BF16), wq_ref[...], preferred_element_type=F32) * (MEM_HEAD_DIM ** -0.5)).astype(BF16)
          for h in h1]
    heads = [[] for _ in halves]
    for hd in range(MEM_HEADS):
        hs = slice(hd * MEM_HEAD_DIM, (hd + 1) * MEM_HEAD_DIM)
        for k in range(POST_SPLIT):
            s = lax.dot_general(qx[k][:, hs], km_ref[:, hs], (((1,), (1,)), ((), ())), preferred_element_type=F32)
            p = jnp.exp(s - jnp.max(s, -1, keepdims=True))
            l = jnp.sum(p, -1, keepdims=True)
            heads[k].append(jnp.dot(p.astype(BF16), vm_ref[:, hs], preferred_element_type=F32) / l)
    o = [jnp.concatenate(hk, axis=1).astype(BF16) for hk in heads]
    xa = [jnp.dot(ok, wco_ref[...], preferred_element_type=F32) for ok in o]
    h2 = jnp.concatenate([_layer_norm(DEEPNORM_ALPHA * h + x, g2_ref[...], b2_ref[...]) for h, x in zip(h1, xa)],
                         axis=0)
    h2_ref[...] = h2
    t_hi = h2.astype(BF16)
    t_rounded = t_hi.astype(F32)
    tw = _pack_bf16_pairs(t_rounded)
    for half in range(2):
        t_ref[half] = tw[:, half * ROW_WORDS:(half + 1) * ROW_WORDS]
    t_lo = (h2 - t_rounded).astype(BF16)
    both_parts = jnp.dot(t_hi, wr_ref[...], preferred_element_type=F32)
    logits = (both_parts[:, :LANES] + both_parts[:, LANES:]
              + jnp.dot(t_lo, wr_ref[:, :LANES], preferred_element_type=F32)) + br_ref[...]
    lane = lax.broadcasted_iota(jnp.int32, logits.shape, 1)
    neg = -jnp.inf
    big = jnp.int32(LANES)
    gl = jnp.where(lane < N_GROUPS, logits, neg)
    gmax = jnp.max(gl, -1, keepdims=True)
    p_grp = 1.0 / jnp.sum(jnp.exp(gl - gmax), -1, keepdims=True)
    grp = jnp.min(jnp.where(gl == gmax, lane, big), -1, keepdims=True)
    lo = N_GROUPS + EXPERTS_PER_GROUP * grp
    el = jnp.where((lane >= lo) & (lane < lo + EXPERTS_PER_GROUP), logits, neg)
    v1 = jnp.max(el, -1, keepdims=True)
    i1 = jnp.min(jnp.where(el == v1, lane, big), -1, keepdims=True)
    el2 = jnp.where(lane == i1, neg, el)
    v2 = jnp.max(el2, -1, keepdims=True)
    i2 = jnp.min(jnp.where(el2 == v2, lane, big), -1, keepdims=True)
    e21 = jnp.exp(v2 - v1)
    w1 = p_grp / (1.0 + e21)
    w2 = p_grp * e21 / (1.0 + e21)
    @pl.when((pl.program_id(0) == 0) & (pl.program_id(1) == 0))
    def _():
        cnt_ref[...] = jnp.zeros_like(cnt_ref)

    e1 = i1 - N_GROUPS
    e2 = i2 - N_GROUPS
    hot1 = jnp.where(lane == e1, 1.0, 0.0)
    hot2 = jnp.where(lane == e2, 1.0, 0.0)
    both = hot1 + hot2
    both16 = both.astype(BF16)
    blk = tri_ref.shape[0] // TRI_BLOCKS
    before = jnp.concatenate(
        [jnp.dot(tri_ref[r * blk:(r + 1) * blk, :(r + 1) * blk], both16[:(r + 1) * blk], preferred_element_type=F32)
         for r in range(TRI_BLOCKS)], axis=0) + cnt_ref[...]
    rank1 = jnp.sum(hot1 * before, -1, keepdims=True)
    rank2 = jnp.sum(hot2 * before, -1, keepdims=True)
    cnt_ref[...] += jnp.sum(both, 0, keepdims=True)
    cnt_out_ref[...] = cnt_ref[...]
    route = jnp.zeros(logits.shape, F32)
    for k, val in enumerate((e1.astype(F32), e2.astype(F32), w1, w2, rank1, rank2)):
        route = jnp.where(lane == k, val, route)
    route_ref[...] = route
    route_t_ref[...] = jnp.transpose(route)[:ROUTE_LANES, :]


def _post(att, c, h0, km, vm, w_out, g1, b1, w_cq, w_co, g2, b2, wr_hi, wr_lo, br):
    b, seq, _ = att.shape
    tm = TM_POST
    tok = lambda width: pl.BlockSpec((None, tm, width), lambda bi, i: (bi, i, 0))
    memb = pl.BlockSpec((None, N_MEM, D_MODEL), lambda bi, i: (bi, 0, 0))
    const = lambda shape: pl.BlockSpec(shape, lambda bi, i: (0, 0))
    vec = const((1, D_MODEL))
    sq = const((D_MODEL, D_MODEL))
    idx = jnp.arange(tm, dtype=jnp.int32)
    tri = jnp.where(idx[:, None] > idx[None, :], 1.0, 0.0).astype(BF16)
    return pl.pallas_call(
        _post_kernel,
        grid=(b, seq // tm),
        in_specs=[tok(ATT_WIDTH), tok(CONV_WIDTH), tok(D_MODEL), memb, memb, sq, vec, vec, sq, sq, vec, vec,
                  const((D_MODEL, 2 * LANES)), const((1, LANES)), const((tm, tm))],
        out_specs=[tok(D_MODEL),
                   pl.BlockSpec((2, None, tm, ROW_WORDS), lambda bi, i: (0, bi, i, 0)),
                   tok(LANES),
                   pl.BlockSpec((ROUTE_LANES, tm), lambda bi, i: (0, bi * (seq // tm) + i)),
                   const((1, LANES))],
        out_shape=[jax.ShapeDtypeStruct((b, seq, D_MODEL), F32),
                   jax.ShapeDtypeStruct((2, b, seq, ROW_WORDS), jnp.uint32),
                   jax.ShapeDtypeStruct((b, seq, LANES), F32),
                   jax.ShapeDtypeStruct((ROUTE_LANES, b * seq), F32),
                   jax.ShapeDtypeStruct((1, LANES), F32)],
        scratch_shapes=[pltpu.VMEM((1, LANES), F32)],
        compiler_params=pltpu.CompilerParams(dimension_semantics=("arbitrary", "arbitrary"),
                                             vmem_limit_bytes=VMEM_LIMIT),
        name="post",
    )(att, c, h0, km, vm, w_out, g1, b1, w_cq, w_co, g2, b2, jnp.concatenate([wr_hi, wr_lo], axis=1), br, tri)


def _pack_bf16_pairs(x):
    k = x.shape[1] // 2
    bits = lax.bitcast_convert_type(x, jnp.uint32)
    return (bits[:, :k] >> 16) | (bits[:, k:] & jnp.uint32(0xFFFF0000))


def _unpack_bf16_pairs(w):
    lo = lax.bitcast_convert_type(w << 16, F32)
    hi = lax.bitcast_convert_type(w & jnp.uint32(0xFFFF0000), F32)
    return jnp.concatenate([lo, hi], axis=1)


def _sc_mesh():
    return plsc.VectorSubcoreMesh(core_axis_name="c", subcore_axis_name="s")


def _sc_scatter2(x, idx_a, idx_b, n_out):
    m = x.shape[0]

    @pl.kernel(out_type=jax.ShapeDtypeStruct((n_out, ROW_WORDS), x.dtype), mesh=_sc_mesh(), scratch_types=[])
    def scatter(x_hbm, ia_hbm, ib_hbm, o_hbm):
        def body(x_vmem, ia_vmem, ib_vmem):
            pltpu.sync_copy(x_vmem, o_hbm.at[ia_vmem.at[0]])
            pltpu.sync_copy(x_vmem, o_hbm.at[ib_vmem.at[0]])

        pltpu.emit_pipeline(
            body, grid=(m // SC_WINDOW,),
            in_specs=[pl.BlockSpec((SC_WINDOW, ROW_WORDS), lambda i: (i, 0)),
                      pl.BlockSpec((1, SC_WINDOW), lambda i: (0, i)),
                      pl.BlockSpec((1, SC_WINDOW), lambda i: (0, i))],
            out_specs=[],
            core_axis_name=("c", "s"), dimension_semantics=(pltpu.PARALLEL,),
        )(x_hbm, ia_hbm, ib_hbm)

    return scatter(x, idx_a.reshape(1, m), idx_b.reshape(1, m))


def _sc_gather(x, idx):
    m = idx.shape[0]

    @pl.kernel(out_type=jax.ShapeDtypeStruct((m, ROW_WORDS), x.dtype), mesh=_sc_mesh(), scratch_types=[])
    def gather(x_hbm, i_hbm, o_hbm):
        def body(i_vmem, o_vmem):
            pltpu.sync_copy(x_hbm.at[i_vmem.at[0]], o_vmem)

        pltpu.emit_pipeline(
            body, grid=(m // SC_WINDOW,),
            in_specs=[pl.BlockSpec((1, SC_WINDOW), lambda i: (0, i))],
            out_specs=[pl.BlockSpec((SC_WINDOW, ROW_WORDS), lambda i: (i, 0))],
            core_axis_name=("c", "s"), dimension_semantics=(pltpu.PARALLEL,),
        )(i_hbm, o_hbm)

    return gather(x, idx.reshape(1, m))


def _experts_kernel(tile_expert_ref, n_used_ref, x_ref, wg32_ref, wu32_ref, wd32_ref, y_ref, wg_ref, wu_ref, wd_ref):
    j = pl.program_id(0)
    prev = tile_expert_ref[jnp.maximum(j - 1, 0)]

    @pl.when((j == 0) | (tile_expert_ref[j] != prev))
    def _():
        wg_ref[...] = wg32_ref[...].astype(BF16)
        wu_ref[...] = wu32_ref[...].astype(BF16)
        wd_ref[...] = wd32_ref[...].astype(BF16)

    @pl.when(j < n_used_ref[0])
    def _():
        u0 = _unpack_bf16_pairs(x_ref[0])
        u1 = _unpack_bf16_pairs(x_ref[1])
        x = jnp.concatenate([u0[:, :ROW_WORDS], u1[:, :ROW_WORDS], u0[:, ROW_WORDS:], u1[:, ROW_WORDS:]],
                            axis=1).astype(BF16)
        gate = jnp.dot(x, wg_ref[...], preferred_element_type=F32)
        up = jnp.dot(x, wu_ref[...], preferred_element_type=F32)
        he = (gate * _sigmoid(gate) * up).astype(BF16)
        y = jnp.dot(he, wd_ref[...], preferred_element_type=F32)
        yw = _pack_bf16_pairs(y.astype(BF16).astype(F32))
        for half in range(2):
            y_ref[half] = yw[:, half * ROW_WORDS:(half + 1) * ROW_WORDS]


def _experts(xs, tile_expert, n_used, w_gate, w_up, w_down):
    _, rows, _ = xs.shape
    tr = TR_EXPERT
    halves = pl.BlockSpec((2, tr, ROW_WORDS), lambda j, te, nu: (0, jnp.minimum(j, nu[0] - 1), 0))
    return pl.pallas_call(
        _experts_kernel,
        grid_spec=pltpu.PrefetchScalarGridSpec(
            num_scalar_prefetch=2,
            grid=(rows // tr,),
            in_specs=[halves,
                      pl.BlockSpec((None, D_MODEL, D_EXPERT), lambda j, te, nu: (te[j], 0, 0)),
                      pl.BlockSpec((None, D_MODEL, D_EXPERT), lambda j, te, nu: (te[j], 0, 0)),
                      pl.BlockSpec((None, D_EXPERT, D_MODEL), lambda j, te, nu: (te[j], 0, 0))],
            out_specs=halves,
            scratch_shapes=[pltpu.VMEM((D_MODEL, D_EXPERT), BF16), pltpu.VMEM((D_MODEL, D_EXPERT), BF16),
                            pltpu.VMEM((D_EXPERT, D_MODEL), BF16)],
        ),
        out_shape=jax.ShapeDtypeStruct(xs.shape, jnp.uint32),
        compiler_params=pltpu.CompilerParams(dimension_semantics=("arbitrary",), vmem_limit_bytes=VMEM_LIMIT),
        name="experts",
    )(tile_expert, n_used, xs, w_gate, w_up, w_down)


def _combine_kernel(h2_ref, y1_ref, y2_ref, route_ref, g3_ref, b3_ref, o_ref):
    route = route_ref[...]
    w1 = route[:, ROUTE_W:ROUTE_W + 1]
    w2 = route[:, ROUTE_W + 1:ROUTE_W + 2]
    halves = []
    for half in range(2):
        halves.append(w1 * _unpack_bf16_pairs(y1_ref[half]) + w2 * _unpack_bf16_pairs(y2_ref[half]))
    ff = jnp.concatenate([halves[0][:, :ROW_WORDS], halves[1][:, :ROW_WORDS],
                          halves[0][:, ROW_WORDS:], halves[1][:, ROW_WORDS:]], axis=1)
    o_ref[...] = _layer_norm(DEEPNORM_ALPHA * h2_ref[...] + ff, g3_ref[...], b3_ref[...])


def _combine(h2, y1, y2, route, g3, b3):
    n = h2.shape[0]
    tm = TM_COMBINE
    row = lambda width: pl.BlockSpec((tm, width), lambda i: (i, 0))
    halves = pl.BlockSpec((2, tm, ROW_WORDS), lambda i: (0, i, 0))
    vec = pl.BlockSpec((1, D_MODEL), lambda i: (0, 0))
    return pl.pallas_call(
        _combine_kernel,
        grid=(n // tm,),
        in_specs=[row(D_MODEL), halves, halves, row(LANES), vec, vec],
        out_specs=row(D_MODEL),
        out_shape=jax.ShapeDtypeStruct((n, D_MODEL), F32),
        compiler_params=pltpu.CompilerParams(dimension_semantics=("parallel",), vmem_limit_bytes=VMEM_LIMIT),
        name="combine",
    )(h2, y1, y2, route, g3, b3)


def _moe(tw, route, route_t, counts, h2, w_gate, w_up, w_down, g3, b3):
    n = h2.shape[0]
    tr = TR_EXPERT
    cap = 2 * n + N_EXPERTS * tr
    cnt = counts[0, :N_EXPERTS].astype(jnp.int32)
    seg = (cnt + tr - 1) // tr * tr
    ends = jnp.cumsum(seg)
    starts = ends - seg
    tile_start = jnp.arange(cap // tr, dtype=jnp.int32) * tr
    tile_expert = jnp.minimum(jnp.sum(tile_start[:, None] >= ends[None, :], axis=1, dtype=jnp.int32), N_EXPERTS - 1)
    n_used = (ends[-1] // tr).reshape(1)

    def half_row_index(k):
        expert = route_t[ROUTE_E + k].astype(jnp.int32)
        hot = expert[None, :] == jnp.arange(N_EXPERTS, dtype=jnp.int32)[:, None]
        pos = route_t[ROUTE_RANK + k].astype(jnp.int32) + jnp.sum(jnp.where(hot, starts[:, None], 0), axis=0)
        return jnp.concatenate([pos, cap + pos])

    idx1, idx2 = half_row_index(0), half_row_index(1)
    xs = _sc_scatter2(tw.reshape(2 * n, ROW_WORDS), idx1, idx2, 2 * cap)
    ys = _experts(xs.reshape(2, cap, ROW_WORDS), tile_expert, n_used, w_gate, w_up, w_down)
    ys = ys.reshape(2 * cap, ROW_WORDS)
    y1 = _sc_gather(ys, idx1).reshape(2, n, ROW_WORDS)
    y2 = _sc_gather(ys, idx2).reshape(2, n, ROW_WORDS)
    return _combine(h2, y1, y2, route, g3, b3)


def _rope_tables(seq):
    rows = seq // GRID_W
    row = jnp.repeat(jnp.arange(rows, dtype=F32), GRID_W)
    col = jnp.tile(jnp.arange(GRID_W, dtype=F32), rows)
    inv_freq = ROPE_THETA ** (-jnp.arange(0, ROPE_AXIS_DIM, 2, dtype=F32) / ROPE_AXIS_DIM)
    ang = jnp.concatenate([row[:, None] * inv_freq, col[:, None] * inv_freq], -1)
    cos = jnp.repeat(jnp.cos(ang), 2, axis=-1)
    sin = jnp.repeat(jnp.sin(ang), 2, axis=-1)
    sign = jnp.tile(jnp.array([-1.0, 1.0], F32), HEAD_DIM // 2)
    sin = sin * sign
    return jnp.tile(cos, (1, LANES // HEAD_DIM)), jnp.tile(sin, (1, LANES // HEAD_DIM))


def _encoder(x, mem, p):
    b, seq, d = x.shape
    cos, sin = _rope_tables(seq)
    h0, q, k, v, u = _proj_in(x.reshape(b * seq, d), p["ln_in_g"], p["ln_in_b"], p["w_in"], p["qg"], p["kg"],
                              cos, sin, p["head_mean"], seq)
    c = _conv_branch(u.reshape(b, seq, CONV_WIDTH), p["conv_w"], p["conv_b"], p["cln_g"], p["cln_b"])
    att = _gqa(q.reshape(ATT_HEADS, b, seq, HEAD_DIM), k.reshape(ATT_KV_HEADS, b, seq, HEAD_DIM),
               v.reshape(b, seq, 2 * KV_WIDTH), p["score_bound"])
    km, vm = _mem_kv(mem, p["ln_mem_g"], p["ln_mem_b"], p["w_ck"], p["w_cv"])
    h2, tw, route, route_t, counts = _post(att, c, h0.reshape(b, seq, d), km, vm, p["w_out"], p["ln1_g"],
                                           p["ln1_b"], p["w_cq"], p["w_co"], p["ln2_g"], p["ln2_b"],
                                           p["wr_hi"], p["wr_lo"], p["br"])
    y = _moe(tw.reshape(2, b * seq, ROW_WORDS), route.reshape(b * seq, LANES), route_t, counts,
             h2.reshape(b * seq, d), p["w_gate"], p["w_up"], p["w_down"], p["ln3_g"], p["ln3_b"])
    return y.reshape(b, seq, d)


def kernel(x_prompt, x_sample, mem_prompt, mem_sample, ln_in_g, ln_in_b, ln_mem_g, ln_mem_b, w_in, q_norm_g, k_norm_g, conv_w, conv_b, conv_ln_g, conv_ln_b, w_mix_out, ln1_g, ln1_b, w_cq, w_ck, w_cv, w_co, ln2_g, ln2_b, w_router_g, b_router_g, w_router_e, b_router_e, w_e_gate, w_e_up, w_e_down, ln3_g, ln3_b):
    l = 0
    vec = lambda a: a.reshape(1, -1).astype(F32)
    w_r = jnp.concatenate([w_router_g[l], jnp.transpose(w_router_e[l], (1, 0, 2)).reshape(D_MODEL, N_EXPERTS)], axis=1)
    w_r = jnp.pad(w_r, ((0, 0), (0, LANES - w_r.shape[1])))
    wr_hi = w_r.astype(BF16)
    wr_lo = (w_r - wr_hi.astype(F32)).astype(BF16)
    b_r = jnp.concatenate([b_router_g[l], b_router_e[l].reshape(-1)])
    b_r = jnp.pad(b_r, (0, LANES - b_r.shape[0])).reshape(1, LANES)
    head = jnp.arange(LANES) // HEAD_DIM
    head_mean = jnp.where(head[:, None] == head[None, :], 1.0 / HEAD_DIM, 0.0).astype(BF16)
    p = {
        "ln_in_g": vec(ln_in_g), "ln_in_b": vec(ln_in_b), "ln_mem_g": vec(ln_mem_g), "ln_mem_b": vec(ln_mem_b),
        "w_in": w_in[l].astype(BF16),
        "qg": jnp.tile(q_norm_g[l], LANES // HEAD_DIM).reshape(1, LANES),
        "kg": jnp.tile(k_norm_g[l], LANES // HEAD_DIM).reshape(1, LANES),
        "head_mean": head_mean,
        "score_bound": (NORM_CAP * jnp.max(jnp.abs(q_norm_g[l])) * Q_SCALE) * (NORM_CAP * jnp.max(jnp.abs(k_norm_g[l]))),
        "conv_w": conv_w[l], "conv_b": vec(conv_b[l]), "cln_g": vec(conv_ln_g[l]), "cln_b": vec(conv_ln_b[l]),
        "w_out": w_mix_out[l].astype(BF16), "ln1_g": vec(ln1_g[l]), "ln1_b": vec(ln1_b[l]),
        "w_cq": w_cq[l].astype(BF16), "w_ck": w_ck[l].astype(BF16), "w_cv": w_cv[l].astype(BF16),
        "w_co": w_co[l].astype(BF16), "ln2_g": vec(ln2_g[l]), "ln2_b": vec(ln2_b[l]),
        "wr_hi": wr_hi, "wr_lo": wr_lo, "br": b_r,
        "w_gate": w_e_gate[l], "w_up": w_e_up[l], "w_down": w_e_down[l],
        "ln3_g": vec(ln3_g[l]), "ln3_b": vec(ln3_b[l]),
    }
    return (_encoder(x_prompt, mem_prompt, p), _encoder(x_sample, mem_sample, p))
```

```python
import functools

import jax
import jax.numpy as jnp
from jax import lax
from jax.experimental import pallas as pl
from jax.experimental.pallas import tpu as pltpu
from jax.experimental.pallas import tpu_sc as plsc

F32 = jnp.float32
BF16 = jnp.bfloat16

D_MODEL = 1024
DEPTH = 1
GRID_W = 64
N_MEM = 256
ATT_HEADS = 8
ATT_KV_HEADS = 2
HEAD_DIM = 64
ATT_WIDTH = ATT_HEADS * HEAD_DIM
KV_WIDTH = ATT_KV_HEADS * HEAD_DIM
ROPE_THETA = 10000.0
ROPE_AXIS_DIM = HEAD_DIM // 2
CONV_WIDTH = D_MODEL - ATT_WIDTH
CONV_KSIZE = 31
CONV_PAD = CONV_KSIZE // 2
IN_WIDTH = ATT_WIDTH + 2 * KV_WIDTH + 2 * CONV_WIDTH
MEM_HEADS = 4
MEM_HEAD_DIM = D_MODEL // MEM_HEADS
N_GROUPS = 4
EXPERTS_PER_GROUP = 8
N_EXPERTS = N_GROUPS * EXPERTS_PER_GROUP
D_EXPERT = 256
LN_EPS = 1e-5
RMS_EPS = 1e-6
DEEPNORM_ALPHA = (2.0 * DEPTH) ** 0.25
LOG2_E = 1.4426950408889634
Q_SCALE = HEAD_DIM ** -0.5 * LOG2_E

LANES = 128
SUBLANES = 8
HALO = 16
VMEM_LIMIT = 56 * 1024 * 1024

TM_IN = 1024
IN_SPLIT = 4
TC_CONV = 512
CONV_MXU_TAPS = 16
CONV_MXU_COLS = 256
CONV_VPU_ROWS = 256
TQ_ATT = 512
KV_CHUNK = 512
ATT_SAFE_SPAN = 50.0
NORM_CAP = HEAD_DIM ** 0.5 * (1.0 + 2.0 ** -7)
TM_POST = 1024
POST_SPLIT = 4
TRI_BLOCKS = 4
TR_EXPERT = 1024
X_SLOTS = 3
TM_COMBINE = 1024
ROW_WORDS = 256
SC_WINDOW = 128
ROUTE_E, ROUTE_W, ROUTE_RANK = 0, 2, 4
ROUTE_LANES = 8


def _layer_norm(x, g, b):
    mu = jnp.mean(x, -1, keepdims=True)
    xc = x - mu
    var = jnp.mean(xc * xc, -1, keepdims=True)
    return xc * lax.rsqrt(var + LN_EPS) * g + b


def _sigmoid(x):
    return 1.0 / (1.0 + jnp.exp(-x))


def _rope(x, cos, sin_signed):
    lane = lax.broadcasted_iota(jnp.int32, x.shape, 1)
    nxt = pltpu.roll(x, LANES - 1, axis=1)
    prv = pltpu.roll(x, 1, axis=1)
    return x * cos + jnp.where((lane & 1) == 0, nxt, prv) * sin_signed


def _proj_in_kernel(x_ref, g_ref, b_ref, w_ref, qg_ref, kg_ref, cos_ref, sin_ref, hm_ref,
                    h_ref, q_ref, k_ref, v_ref, u_ref):
    i1 = ATT_WIDTH
    i2 = i1 + KV_WIDTH
    i3 = i2 + KV_WIDTH
    i4 = i3 + CONV_WIDTH
    hm = hm_ref[...]
    heads_per_tile = LANES // HEAD_DIM
    tm = x_ref.shape[0]
    for rows in [slice(r0, r0 + tm // IN_SPLIT) for r0 in range(0, tm, tm // IN_SPLIT)]:
        h = _layer_norm(x_ref[rows, :], g_ref[...], b_ref[...])
        h_ref[rows, :] = h
        z = jnp.dot(h.astype(BF16), w_ref[...], preferred_element_type=F32)
        cos = cos_ref[rows, :]
        sin = sin_ref[rows, :]
        for j in range(ATT_WIDTH // LANES):
            qj = z[:, j * LANES:(j + 1) * LANES]
            ms = jnp.dot((qj * qj).astype(BF16), hm, preferred_element_type=F32)
            qn = qj * lax.rsqrt(ms + RMS_EPS) * qg_ref[...]
            qr = (_rope(qn, cos, sin) * Q_SCALE).astype(BF16)
            for r in range(heads_per_tile):
                q_ref[j * heads_per_tile + r, rows, :] = qr[:, r * HEAD_DIM:(r + 1) * HEAD_DIM]
        kj = z[:, i1:i2]
        ms = jnp.dot((kj * kj).astype(BF16), hm, preferred_element_type=F32)
        kn = kj * lax.rsqrt(ms + RMS_EPS) * kg_ref[...]
        kr = _rope(kn, cos, sin).astype(BF16)
        vj = z[:, i2:i3].astype(BF16)
        ones = jnp.ones((vj.shape[0], HEAD_DIM), BF16)
        for g in range(ATT_KV_HEADS):
            k_ref[g, rows, :] = kr[:, g * HEAD_DIM:(g + 1) * HEAD_DIM]
        v_ref[rows, :] = jnp.concatenate([vj[:, :HEAD_DIM], ones, vj[:, HEAD_DIM:], ones], axis=1)
        u_ref[rows, :] = (z[:, i3:i4] * _sigmoid(z[:, i4:])).astype(BF16)


def _proj_in(x, ln_g, ln_b, w_in, qg, kg, cos, sin, hm, seq):
    t = x.shape[0]
    tm = TM_IN
    nseq = seq // tm
    row = lambda i: (i, 0)
    const = lambda i: (0, 0)
    return pl.pallas_call(
        _proj_in_kernel,
        grid=(t // tm,),
        in_specs=[
            pl.BlockSpec((tm, D_MODEL), row),
            pl.BlockSpec((1, D_MODEL), const),
            pl.BlockSpec((1, D_MODEL), const),
            pl.BlockSpec((D_MODEL, IN_WIDTH), const),
            pl.BlockSpec((1, LANES), const),
            pl.BlockSpec((1, LANES), const),
            pl.BlockSpec((tm, LANES), lambda i: (i % nseq, 0)),
            pl.BlockSpec((tm, LANES), lambda i: (i % nseq, 0)),
            pl.BlockSpec((LANES, LANES), const),
        ],
        out_specs=[
            pl.BlockSpec((tm, D_MODEL), row),
            pl.BlockSpec((ATT_HEADS, tm, HEAD_DIM), lambda i: (0, i, 0)),
            pl.BlockSpec((ATT_KV_HEADS, tm, HEAD_DIM), lambda i: (0, i, 0)),
            pl.BlockSpec((tm, 2 * KV_WIDTH), row),
            pl.BlockSpec((tm, CONV_WIDTH), row),
        ],
        out_shape=[
            jax.ShapeDtypeStruct((t, D_MODEL), F32),
            jax.ShapeDtypeStruct((ATT_HEADS, t, HEAD_DIM), BF16),
            jax.ShapeDtypeStruct((ATT_KV_HEADS, t, HEAD_DIM), BF16),
            jax.ShapeDtypeStruct((t, 2 * KV_WIDTH), BF16),
            jax.ShapeDtypeStruct((t, CONV_WIDTH), BF16),
        ],
        compiler_params=pltpu.CompilerParams(dimension_semantics=("parallel",), vmem_limit_bytes=VMEM_LIMIT),
        name="proj_in",
    )(x, ln_g, ln_b, w_in, qg, kg, cos, sin, hm)


def _conv_kernel(u_ref, w_ref, wdiag_ref, cb_ref, g_ref, b_ref, o_ref, win_ref, sh_ref, *, seq):
    i = pl.program_id(1)
    tc = TC_CONV
    t0 = pl.multiple_of(i * tc, tc)
    top_start = pl.multiple_of(jnp.maximum(t0 - HALO, 0), HALO)
    bot_start = pl.multiple_of(jnp.minimum(t0 + tc, seq - HALO), HALO)
    top = u_ref[pl.ds(top_start, HALO), :].astype(F32)
    bot = u_ref[pl.ds(bot_start, HALO), :].astype(F32)
    win_ref[0:HALO, :] = jnp.where(i > 0, top, 0.0)
    win_ref[HALO:HALO + tc, :] = u_ref[pl.ds(t0, tc), :].astype(F32)
    win_ref[HALO + tc:, :] = jnp.where(i < pl.num_programs(1) - 1, bot, 0.0)
    sh_rows = sh_ref.shape[1]
    for s in range(1, SUBLANES):
        sh_ref[s] = win_ref[s:s + sh_rows, :]

    def tap(j, r0, rows, cs):
        off = r0 + HALO - CONV_PAD + j
        base, s = off - off % SUBLANES, off % SUBLANES
        return win_ref[base:base + rows, cs] if s == 0 else sh_ref[s, base:base + rows, cs]

    mxu_cols = []
    for cblk in range(CONV_WIDTH // CONV_MXU_COLS):
        cs = slice(cblk * CONV_MXU_COLS, (cblk + 1) * CONV_MXU_COLS)
        acc = None
        for j in range(CONV_MXU_TAPS):
            d = jnp.dot(tap(j, 0, tc, cs).astype(BF16), wdiag_ref[j, cblk], preferred_element_type=F32)
            acc = d if acc is None else acc + d
        mxu_cols.append(acc)
    w = w_ref[...]
    for r0 in range(0, tc, CONV_VPU_ROWS):
        cols = []
        for c in range(CONV_WIDTH // LANES):
            cs = slice(c * LANES, (c + 1) * LANES)
            lanes_in_blk = slice(c * LANES % CONV_MXU_COLS, c * LANES % CONV_MXU_COLS + LANES)
            acc = None
            for j in range(CONV_MXU_TAPS, CONV_KSIZE):
                term = tap(j, r0, CONV_VPU_ROWS, cs) * w[j:j + 1, cs]
                acc = term if acc is None else acc + term
            cols.append(acc + mxu_cols[c * LANES // CONV_MXU_COLS][r0:r0 + CONV_VPU_ROWS, lanes_in_blk])
        y = jnp.concatenate(cols, axis=1) + cb_ref[...]
        y = _layer_norm(y, g_ref[...], b_ref[...])
        o_ref[r0:r0 + CONV_VPU_ROWS, :] = (y * _sigmoid(y)).astype(BF16)


def _conv_branch(u, conv_w, conv_b, cln_g, cln_b):
    b, seq, _ = u.shape
    tc = TC_CONV
    const = lambda bi, i: (0, 0)
    nblk = CONV_WIDTH // CONV_MXU_COLS
    eye = jnp.eye(CONV_MXU_COLS, dtype=F32)
    wdiag = (conv_w[:CONV_MXU_TAPS].reshape(CONV_MXU_TAPS, nblk, 1, CONV_MXU_COLS) * eye).astype(BF16)
    return pl.pallas_call(
        functools.partial(_conv_kernel, seq=seq),
        grid=(b, seq // tc),
        in_specs=[
            pl.BlockSpec((None, seq, CONV_WIDTH), lambda bi, i: (bi, 0, 0)),
            pl.BlockSpec((CONV_KSIZE, CONV_WIDTH), const),
            pl.BlockSpec((CONV_MXU_TAPS, nblk, CONV_MXU_COLS, CONV_MXU_COLS), lambda bi, i: (0, 0, 0, 0)),
            pl.BlockSpec((1, CONV_WIDTH), const),
            pl.BlockSpec((1, CONV_WIDTH), const),
            pl.BlockSpec((1, CONV_WIDTH), const),
        ],
        out_specs=pl.BlockSpec((None, tc, CONV_WIDTH), lambda bi, i: (bi, i, 0)),
        out_shape=jax.ShapeDtypeStruct((b, seq, CONV_WIDTH), BF16),
        scratch_shapes=[pltpu.VMEM((tc + 2 * HALO, CONV_WIDTH), F32),
                        pltpu.VMEM((SUBLANES, tc + 2 * HALO - SUBLANES, CONV_WIDTH), F32)],
        compiler_params=pltpu.CompilerParams(dimension_semantics=("parallel", "arbitrary"),
                                             vmem_limit_bytes=VMEM_LIMIT),
        name="conv_branch",
    )(u, conv_w, wdiag, conv_b, cln_g, cln_b)


def _gqa_kernel(q_ref, k_ref, v_ref, o_ref, *, bounded):
    rep = ATT_HEADS // ATT_KV_HEADS
    tq = q_ref.shape[1]
    seq = k_ref.shape[1]
    n_chunks = seq // KV_CHUNK
    nt_dims = (((1,), (1,)), ((), ()))
    qs = [q_ref[g * rep:(g + 1) * rep].reshape(rep * tq, HEAD_DIM) for g in range(ATT_KV_HEADS)]

    if bounded:
        acc = [None] * ATT_KV_HEADS
        for c in range(n_chunks):
            rows = slice(c * KV_CHUNK, (c + 1) * KV_CHUNK)
            for g in range(ATT_KV_HEADS):
                s = lax.dot_general(qs[g], k_ref[g, rows], nt_dims, preferred_element_type=F32)
                d = jnp.dot(jnp.exp2(s).astype(BF16), v_ref[rows, g * LANES:(g + 1) * LANES],
                            preferred_element_type=F32)
                acc[g] = d if c == 0 else acc[g] + d
    else:
        def step(c, carry):
            rows = pl.ds(pl.multiple_of(c * KV_CHUNK, KV_CHUNK), KV_CHUNK)
            new = []
            for g in range(ATT_KV_HEADS):
                m, a = carry[g]
                s = lax.dot_general(qs[g], k_ref[g, rows], nt_dims, preferred_element_type=F32)
                m_new = jnp.maximum(m, jnp.max(s, -1, keepdims=True))
                p = jnp.exp2(s - m_new).astype(BF16)
                d = jnp.dot(p, v_ref[rows, g * LANES:(g + 1) * LANES], preferred_element_type=F32)
                new.append((m_new, jnp.exp2(m - m_new) * a + d))
            return tuple(new)

        init = tuple((jnp.full((rep * tq, 1), -jnp.inf, F32), jnp.zeros((rep * tq, LANES), F32))
                     for _ in range(ATT_KV_HEADS))
        out = lax.fori_loop(0, n_chunks, step, init)
        acc = [out[g][1] for g in range(ATT_KV_HEADS)]

    outs = []
    for g in range(ATT_KV_HEADS):
        o = acc[g] / pltpu.roll(acc[g], HEAD_DIM, axis=1)
        outs.extend(o[r * tq:(r + 1) * tq, :HEAD_DIM] for r in range(rep))
    o_ref[...] = jnp.concatenate(outs, axis=1).astype(BF16)


def _gqa(q, k, v, score_bound):
    _, b, seq, _ = q.shape
    tq = TQ_ATT

    def call(bounded):
        return pl.pallas_call(
            functools.partial(_gqa_kernel, bounded=bounded),
            grid=(b, seq // tq),
            in_specs=[
                pl.BlockSpec((ATT_HEADS, None, tq, HEAD_DIM), lambda bi, i: (0, bi, i, 0)),
                pl.BlockSpec((ATT_KV_HEADS, None, seq, HEAD_DIM), lambda bi, i: (0, bi, 0, 0)),
                pl.BlockSpec((None, seq, ATT_KV_HEADS * LANES), lambda bi, i: (bi, 0, 0)),
            ],
            out_specs=pl.BlockSpec((None, tq, ATT_WIDTH), lambda bi, i: (bi, i, 0)),
            out_shape=jax.ShapeDtypeStruct((b, seq, ATT_WIDTH), BF16),
            compiler_params=pltpu.CompilerParams(dimension_semantics=("parallel", "arbitrary"),
                                                 vmem_limit_bytes=VMEM_LIMIT),
            name="gqa" if bounded else "gqa_online",
        )

    return lax.cond(score_bound <= ATT_SAFE_SPAN, call(True), call(False), q, k, v)


def _mem_kv_kernel(m_ref, g_ref, b_ref, wk_ref, wv_ref, k_ref, v_ref):
    m = _layer_norm(m_ref[...], g_ref[...], b_ref[...]).astype(BF16)
    k_ref[...] = jnp.dot(m, wk_ref[...], preferred_element_type=F32).astype(BF16)
    v_ref[...] = jnp.dot(m, wv_ref[...], preferred_element_type=F32).astype(BF16)


def _mem_kv(mem, ln_g, ln_b, w_ck, w_cv):
    b = mem.shape[0]
    const = lambda bi: (0, 0)
    blk = pl.BlockSpec((None, N_MEM, D_MODEL), lambda bi: (bi, 0, 0))
    return pl.pallas_call(
        _mem_kv_kernel,
        grid=(b,),
        in_specs=[blk, pl.BlockSpec((1, D_MODEL), const), pl.BlockSpec((1, D_MODEL), const),
                  pl.BlockSpec((D_MODEL, D_MODEL), const), pl.BlockSpec((D_MODEL, D_MODEL), const)],
        out_specs=[blk, blk],
        out_shape=[jax.ShapeDtypeStruct((b, N_MEM, D_MODEL), BF16)] * 2,
        compiler_params=pltpu.CompilerParams(dimension_semantics=("parallel",), vmem_limit_bytes=VMEM_LIMIT),
        name="mem_kv",
    )(mem, ln_g, ln_b, w_ck, w_cv)


def _post_kernel(att_ref, c_ref, h0_ref, km_ref, vm_ref, wo_ref, g1_ref, b1_ref, wq_ref, wco_ref,
                 g2_ref, b2_ref, wr_ref, br_ref, tri_ref, h2_ref, t_ref, route_ref, route_t_ref,
                 cnt_out_ref, cnt_ref):
    tm = att_ref.shape[0]
    halves = [slice(r0, r0 + tm // POST_SPLIT) for r0 in range(0, tm, tm // POST_SPLIT)]
    mixed = [jnp.concatenate([att_ref[r, :], c_ref[r, :]], axis=1) for r in halves]
    mix = [jnp.dot(m, wo_ref[...], preferred_element_type=F32) for m in mixed]
    h1 = [_layer_norm(DEEPNORM_ALPHA * h0_ref[r, :] + m, g1_ref[...], b1_ref[...]) for r, m in zip(halves, mix)]
    qx = [(jnp.dot(h.astype(BF16), wq_ref[...], preferred_element_type=F32) * (MEM_HEAD_DIM ** -0.5)).astype(BF16)
          for h in h1]
    heads = [[] for _ in halves]
    for hd in range(MEM_HEADS):
        hs = slice(hd * MEM_HEAD_DIM, (hd + 1) * MEM_HEAD_DIM)
        for k in range(POST_SPLIT):
            s = lax.dot_general(qx[k][:, hs], km_ref[:, hs], (((1,), (1,)), ((), ())), preferred_element_type=F32)
            p = jnp.exp(s - jnp.max(s, -1, keepdims=True))
            l = jnp.sum(p, -1, keepdims=True)
            heads[k].append(jnp.dot(p.astype(BF16), vm_ref[:, hs], preferred_element_type=F32) / l)
    o = [jnp.concatenate(hk, axis=1).astype(BF16) for hk in heads]
    xa = [jnp.dot(ok, wco_ref[...], preferred_element_type=F32) for ok in o]
    h2 = jnp.concatenate([_layer_norm(DEEPNORM_ALPHA * h + x, g2_ref[...], b2_ref[...]) for h, x in zip(h1, xa)],
                         axis=0)
    h2_ref[...] = h2
    t_hi = h2.astype(BF16)
    t_rounded = t_hi.astype(F32)
    tw = _pack_bf16_pairs(t_rounded)
    for half in range(2):
        t_ref[half] = tw[:, half * ROW_WORDS:(half + 1) * ROW_WORDS]
    t_lo = (h2 - t_rounded).astype(BF16)
    both_parts = jnp.dot(t_hi, wr_ref[...], preferred_element_type=F32)
    logits = (both_parts[:, :LANES] + both_parts[:, LANES:]
              + jnp.dot(t_lo, wr_ref[:, :LANES], preferred_element_type=F32)) + br_ref[...]
    lane = lax.broadcasted_iota(jnp.int32, logits.shape, 1)
    neg = -jnp.inf
    big = jnp.int32(LANES)
    gl = jnp.where(lane < N_GROUPS, logits, neg)
    gmax = jnp.max(gl, -1, keepdims=True)
    p_grp = 1.0 / jnp.sum(jnp.exp(gl - gmax), -1, keepdims=True)
    grp = jnp.min(jnp.where(gl == gmax, lane, big), -1, keepdims=True)
    lo = N_GROUPS + EXPERTS_PER_GROUP * grp
    el = jnp.where((lane >= lo) & (lane < lo + EXPERTS_PER_GROUP), logits, neg)
    v1 = jnp.max(el, -1, keepdims=True)
    i1 = jnp.min(jnp.where(el == v1, lane, big), -1, keepdims=True)
    el2 = jnp.where(lane == i1, neg, el)
    v2 = jnp.max(el2, -1, keepdims=True)
    i2 = jnp.min(jnp.where(el2 == v2, lane, big), -1, keepdims=True)
    e21 = jnp.exp(v2 - v1)
    w1 = p_grp / (1.0 + e21)
    w2 = p_grp * e21 / (1.0 + e21)
    @pl.when((pl.program_id(0) == 0) & (pl.program_id(1) == 0))
    def _():
        cnt_ref[...] = jnp.zeros_like(cnt_ref)

    e1 = i1 - N_GROUPS
    e2 = i2 - N_GROUPS
    hot1 = jnp.where(lane == e1, 1.0, 0.0)
    hot2 = jnp.where(lane == e2, 1.0, 0.0)
    both = hot1 + hot2
    both16 = both.astype(BF16)
    blk = tri_ref.shape[0] // TRI_BLOCKS
    before = jnp.concatenate(
        [jnp.dot(tri_ref[r * blk:(r + 1) * blk, :(r + 1) * blk], both16[:(r + 1) * blk], preferred_element_type=F32)
         for r in range(TRI_BLOCKS)], axis=0) + cnt_ref[...]
    rank1 = jnp.sum(hot1 * before, -1, keepdims=True)
    rank2 = jnp.sum(hot2 * before, -1, keepdims=True)
    cnt_ref[...] += jnp.sum(both, 0, keepdims=True)
    cnt_out_ref[...] = cnt_ref[...]
    route = jnp.zeros(logits.shape, F32)
    for k, val in enumerate((e1.astype(F32), e2.astype(F32), w1, w2, rank1, rank2)):
        route = jnp.where(lane == k, val, route)
    route_ref[...] = route
    route_t_ref[...] = jnp.transpose(route)[:ROUTE_LANES, :]


def _post(att, c, h0, km, vm, w_out, g1, b1, w_cq, w_co, g2, b2, wr_hi, wr_lo, br):
    b, seq, _ = att.shape
    tm = TM_POST
    tok = lambda width: pl.BlockSpec((None, tm, width), lambda bi, i: (bi, i, 0))
    memb = pl.BlockSpec((None, N_MEM, D_MODEL), lambda bi, i: (bi, 0, 0))
    const = lambda shape: pl.BlockSpec(shape, lambda bi, i: (0, 0))
    vec = const((1, D_MODEL))
    sq = const((D_MODEL, D_MODEL))
    idx = jnp.arange(tm, dtype=jnp.int32)
    tri = jnp.where(idx[:, None] > idx[None, :], 1.0, 0.0).astype(BF16)
    return pl.pallas_call(
        _post_kernel,
        grid=(b, seq // tm),
        in_specs=[tok(ATT_WIDTH), tok(CONV_WIDTH), tok(D_MODEL), memb, memb, sq, vec, vec, sq, sq, vec, vec,
                  const((D_MODEL, 2 * LANES)), const((1, LANES)), const((tm, tm))],
        out_specs=[tok(D_MODEL),
                   pl.BlockSpec((2, None, tm, ROW_WORDS), lambda bi, i: (0, bi, i, 0)),
                   tok(LANES),
                   pl.BlockSpec((ROUTE_LANES, tm), lambda bi, i: (0, bi * (seq // tm) + i)),
                   const((1, LANES))],
        out_shape=[jax.ShapeDtypeStruct((b, seq, D_MODEL), F32),
                   jax.ShapeDtypeStruct((2, b, seq, ROW_WORDS), jnp.uint32),
                   jax.ShapeDtypeStruct((b, seq, LANES), F32),
                   jax.ShapeDtypeStruct((ROUTE_LANES, b * seq), F32),
                   jax.ShapeDtypeStruct((1, LANES), F32)],
        scratch_shapes=[pltpu.VMEM((1, LANES), F32)],
        compiler_params=pltpu.CompilerParams(dimension_semantics=("arbitrary", "arbitrary"),
                                             vmem_limit_bytes=VMEM_LIMIT),
        name="post",
    )(att, c, h0, km, vm, w_out, g1, b1, w_cq, w_co, g2, b2, jnp.concatenate([wr_hi, wr_lo], axis=1), br, tri)


def _pack_bf16_pairs(x):
    k = x.shape[1] // 2
    bits = lax.bitcast_convert_type(x, jnp.uint32)
    return (bits[:, :k] >> 16) | (bits[:, k:] & jnp.uint32(0xFFFF0000))


def _unpack_bf16_pairs(w):
    lo = lax.bitcast_convert_type(w << 16, F32)
    hi = lax.bitcast_convert_type(w & jnp.uint32(0xFFFF0000), F32)
    return jnp.concatenate([lo, hi], axis=1)


def _sc_mesh():
    return plsc.VectorSubcoreMesh(core_axis_name="c", subcore_axis_name="s")


def _sc_scatter2(x, idx_a, idx_b, n_out):
    m = x.shape[0]

    @pl.kernel(out_type=jax.ShapeDtypeStruct((n_out, ROW_WORDS), x.dtype), mesh=_sc_mesh(), scratch_types=[])
    def scatter(x_hbm, ia_hbm, ib_hbm, o_hbm):
        def body(x_vmem, ia_vmem, ib_vmem):
            pltpu.sync_copy(x_vmem, o_hbm.at[ia_vmem.at[0]])
            pltpu.sync_copy(x_vmem, o_hbm.at[ib_vmem.at[0]])

        pltpu.emit_pipeline(
            body, grid=(m // SC_WINDOW,),
            in_specs=[pl.BlockSpec((SC_WINDOW, ROW_WORDS), lambda i: (i, 0)),
                      pl.BlockSpec((1, SC_WINDOW), lambda i: (0, i)),
                      pl.BlockSpec((1, SC_WINDOW), lambda i: (0, i))],
            out_specs=[],
            core_axis_name=("c", "s"), dimension_semantics=(pltpu.PARALLEL,),
        )(x_hbm, ia_hbm, ib_hbm)

    return scatter(x, idx_a.reshape(1, m), idx_b.reshape(1, m))


def _sc_gather(x, idx):
    m = idx.shape[0]

    @pl.kernel(out_type=jax.ShapeDtypeStruct((m, ROW_WORDS), x.dtype), mesh=_sc_mesh(), scratch_types=[])
    def gather(x_hbm, i_hbm, o_hbm):
        def body(i_vmem, o_vmem):
            pltpu.sync_copy(x_hbm.at[i_vmem.at[0]], o_vmem)

        pltpu.emit_pipeline(
            body, grid=(m // SC_WINDOW,),
            in_specs=[pl.BlockSpec((1, SC_WINDOW), lambda i: (0, i))],
            out_specs=[pl.BlockSpec((SC_WINDOW, ROW_WORDS), lambda i: (i, 0))],
            core_axis_name=("c", "s"), dimension_semantics=(pltpu.PARALLEL,),
        )(i_hbm, o_hbm)

    return gather(x, idx.reshape(1, m))


def _experts_kernel(tile_expert_ref, n_used_ref, x_hbm, wg32_ref, wu32_ref, wd32_ref, y_ref, wg_ref, wu_ref, wd_ref,
                    xbuf_ref, sem_ref, *, n_steps):
    j = pl.program_id(0)
    tr = xbuf_ref.shape[2]

    def fetch(step):
        tile = jnp.minimum(step, n_used_ref[0] - 1)
        slot = step % X_SLOTS
        return pltpu.make_async_copy(x_hbm.at[:, pl.ds(pl.multiple_of(tile * tr, tr), tr), :], xbuf_ref.at[slot],
                                     sem_ref.at[slot])

    @pl.when(j == 0)
    def _():
        for s in range(min(X_SLOTS - 1, n_steps)):
            fetch(s).start()

    @pl.when(j + (X_SLOTS - 1) < n_steps)
    def _():
        fetch(j + (X_SLOTS - 1)).start()

    fetch(j).wait()
    x_ref = xbuf_ref.at[j % X_SLOTS]
    prev = tile_expert_ref[jnp.maximum(j - 1, 0)]

    @pl.when((j == 0) | (tile_expert_ref[j] != prev))
    def _():
        wg_ref[...] = wg32_ref[...].astype(BF16)
        wu_ref[...] = wu32_ref[...].astype(BF16)
        wd_ref[...] = wd32_ref[...].astype(BF16)

    @pl.when(j < n_used_ref[0])
    def _():
        u0 = _unpack_bf16_pairs(x_ref[0])
        u1 = _unpack_bf16_pairs(x_ref[1])
        x = jnp.concatenate([u0[:, :ROW_WORDS], u1[:, :ROW_WORDS], u0[:, ROW_WORDS:], u1[:, ROW_WORDS:]],
                            axis=1).astype(BF16)
        gate = jnp.dot(x, wg_ref[...], preferred_element_type=F32)
        up = jnp.dot(x, wu_ref[...], preferred_element_type=F32)
        he = (gate * _sigmoid(gate) * up).astype(BF16)
        y = jnp.dot(he, wd_ref[...], preferred_element_type=F32)
        yw = _pack_bf16_pairs(y.astype(BF16).astype(F32))
        for half in range(2):
            y_ref[half] = yw[:, half * ROW_WORDS:(half + 1) * ROW_WORDS]


def _experts(xs, tile_expert, n_used, w_gate, w_up, w_down):
    _, rows, _ = xs.shape
    tr = TR_EXPERT
    halves = pl.BlockSpec((2, tr, ROW_WORDS), lambda j, te, nu: (0, jnp.minimum(j, nu[0] - 1), 0))
    n_steps = rows // tr
    return pl.pallas_call(
        functools.partial(_experts_kernel, n_steps=n_steps),
        grid_spec=pltpu.PrefetchScalarGridSpec(
            num_scalar_prefetch=2,
            grid=(n_steps,),
            in_specs=[pl.BlockSpec(memory_space=pl.ANY),
                      pl.BlockSpec((None, D_MODEL, D_EXPERT), lambda j, te, nu: (te[j], 0, 0)),
                      pl.BlockSpec((None, D_MODEL, D_EXPERT), lambda j, te, nu: (te[j], 0, 0)),
                      pl.BlockSpec((None, D_EXPERT, D_MODEL), lambda j, te, nu: (te[j], 0, 0))],
            out_specs=halves,
            scratch_shapes=[pltpu.VMEM((D_MODEL, D_EXPERT), BF16), pltpu.VMEM((D_MODEL, D_EXPERT), BF16),
                            pltpu.VMEM((D_EXPERT, D_MODEL), BF16),
                            pltpu.VMEM((X_SLOTS, 2, tr, ROW_WORDS), jnp.uint32),
                            pltpu.SemaphoreType.DMA((X_SLOTS,))],
        ),
        out_shape=jax.ShapeDtypeStruct(xs.shape, jnp.uint32),
        compiler_params=pltpu.CompilerParams(dimension_semantics=("arbitrary",), vmem_limit_bytes=VMEM_LIMIT),
        name="experts",
    )(tile_expert, n_used, xs, w_gate, w_up, w_down)


def _combine_kernel(h2_ref, y1_ref, y2_ref, route_ref, g3_ref, b3_ref, o_ref):
    route = route_ref[...]
    w1 = route[:, ROUTE_W:ROUTE_W + 1]
    w2 = route[:, ROUTE_W + 1:ROUTE_W + 2]
    halves = []
    for half in range(2):
        halves.append(w1 * _unpack_bf16_pairs(y1_ref[half]) + w2 * _unpack_bf16_pairs(y2_ref[half]))
    ff = jnp.concatenate([halves[0][:, :ROW_WORDS], halves[1][:, :ROW_WORDS],
                          halves[0][:, ROW_WORDS:], halves[1][:, ROW_WORDS:]], axis=1)
    o_ref[...] = _layer_norm(DEEPNORM_ALPHA * h2_ref[...] + ff, g3_ref[...], b3_ref[...])


def _combine(h2, y1, y2, route, g3, b3):
    n = h2.shape[0]
    tm = TM_COMBINE
    row = lambda width: pl.BlockSpec((tm, width), lambda i: (i, 0))
    halves = pl.BlockSpec((2, tm, ROW_WORDS), lambda i: (0, i, 0))
    vec = pl.BlockSpec((1, D_MODEL), lambda i: (0, 0))
    return pl.pallas_call(
        _combine_kernel,
        grid=(n // tm,),
        in_specs=[row(D_MODEL), halves, halves, row(LANES), vec, vec],
        out_specs=row(D_MODEL),
        out_shape=jax.ShapeDtypeStruct((n, D_MODEL), F32),
        compiler_params=pltpu.CompilerParams(dimension_semantics=("parallel",), vmem_limit_bytes=VMEM_LIMIT),
        name="combine",
    )(h2, y1, y2, route, g3, b3)


def _moe(tw, route, route_t, counts, h2, w_gate, w_up, w_down, g3, b3):
    n = h2.shape[0]
    tr = TR_EXPERT
    cap = 2 * n + N_EXPERTS * tr
    cnt = counts[0, :N_EXPERTS].astype(jnp.int32)
    seg = (cnt + tr - 1) // tr * tr
    ends = jnp.cumsum(seg)
    starts = ends - seg
    tile_start = jnp.arange(cap // tr, dtype=jnp.int32) * tr
    tile_expert = jnp.minimum(jnp.sum(tile_start[:, None] >= ends[None, :], axis=1, dtype=jnp.int32), N_EXPERTS - 1)
    n_used = (ends[-1] // tr).reshape(1)

    def half_row_index(k):
        expert = route_t[ROUTE_E + k].astype(jnp.int32)
        hot = expert[None, :] == jnp.arange(N_EXPERTS, dtype=jnp.int32)[:, None]
        pos = route_t[ROUTE_RANK + k].astype(jnp.int32) + jnp.sum(jnp.where(hot, starts[:, None], 0), axis=0)
        return jnp.concatenate([pos, cap + pos])

    idx1, idx2 = half_row_index(0), half_row_index(1)
    xs = _sc_scatter2(tw.reshape(2 * n, ROW_WORDS), idx1, idx2, 2 * cap)
    ys = _experts(xs.reshape(2, cap, ROW_WORDS), tile_expert, n_used, w_gate, w_up, w_down)
    ys = ys.reshape(2 * cap, ROW_WORDS)
    y1 = _sc_gather(ys, idx1).reshape(2, n, ROW_WORDS)
    y2 = _sc_gather(ys, idx2).reshape(2, n, ROW_WORDS)
    return _combine(h2, y1, y2, route, g3, b3)


def _rope_tables(seq):
    rows = seq // GRID_W
    row = jnp.repeat(jnp.arange(rows, dtype=F32), GRID_W)
    col = jnp.tile(jnp.arange(GRID_W, dtype=F32), rows)
    inv_freq = ROPE_THETA ** (-jnp.arange(0, ROPE_AXIS_DIM, 2, dtype=F32) / ROPE_AXIS_DIM)
    ang = jnp.concatenate([row[:, None] * inv_freq, col[:, None] * inv_freq], -1)
    cos = jnp.repeat(jnp.cos(ang), 2, axis=-1)
    sin = jnp.repeat(jnp.sin(ang), 2, axis=-1)
    sign = jnp.tile(jnp.array([-1.0, 1.0], F32), HEAD_DIM // 2)
    sin = sin * sign
    return jnp.tile(cos, (1, LANES // HEAD_DIM)), jnp.tile(sin, (1, LANES // HEAD_DIM))


def _encoder(x, mem, p):
    b, seq, d = x.shape
    cos, sin = _rope_tables(seq)
    h0, q, k, v, u = _proj_in(x.reshape(b * seq, d), p["ln_in_g"], p["ln_in_b"], p["w_in"], p["qg"], p["kg"],
                              cos, sin, p["head_mean"], seq)
    c = _conv_branch(u.reshape(b, seq, CONV_WIDTH), p["conv_w"], p["conv_b"], p["cln_g"], p["cln_b"])
    att = _gqa(q.reshape(ATT_HEADS, b, seq, HEAD_DIM), k.reshape(ATT_KV_HEADS, b, seq, HEAD_DIM),
               v.reshape(b, seq, 2 * KV_WIDTH), p["score_bound"])
    km, vm = _mem_kv(mem, p["ln_mem_g"], p["ln_mem_b"], p["w_ck"], p["w_cv"])
    h2, tw, route, route_t, counts = _post(att, c, h0.reshape(b, seq, d), km, vm, p["w_out"], p["ln1_g"],
                                           p["ln1_b"], p["w_cq"], p["w_co"], p["ln2_g"], p["ln2_b"],
                                           p["wr_hi"], p["wr_lo"], p["br"])
    y = _moe(tw.reshape(2, b * seq, ROW_WORDS), route.reshape(b * seq, LANES), route_t, counts,
             h2.reshape(b * seq, d), p["w_gate"], p["w_up"], p["w_down"], p["ln3_g"], p["ln3_b"])
    return y.reshape(b, seq, d)


def kernel(x_prompt, x_sample, mem_prompt, mem_sample, ln_in_g, ln_in_b, ln_mem_g, ln_mem_b, w_in, q_norm_g, k_norm_g, conv_w, conv_b, conv_ln_g, conv_ln_b, w_mix_out, ln1_g, ln1_b, w_cq, w_ck, w_cv, w_co, ln2_g, ln2_b, w_router_g, b_router_g, w_router_e, b_router_e, w_e_gate, w_e_up, w_e_down, ln3_g, ln3_b):
    l = 0
    vec = lambda a: a.reshape(1, -1).astype(F32)
    w_r = jnp.concatenate([w_router_g[l], jnp.transpose(w_router_e[l], (1, 0, 2)).reshape(D_MODEL, N_EXPERTS)], axis=1)
    w_r = jnp.pad(w_r, ((0, 0), (0, LANES - w_r.shape[1])))
    wr_hi = w_r.astype(BF16)
    wr_lo = (w_r - wr_hi.astype(F32)).astype(BF16)
    b_r = jnp.concatenate([b_router_g[l], b_router_e[l].reshape(-1)])
    b_r = jnp.pad(b_r, (0, LANES - b_r.shape[0])).reshape(1, LANES)
    head = jnp.arange(LANES) // HEAD_DIM
    head_mean = jnp.where(head[:, None] == head[None, :], 1.0 / HEAD_DIM, 0.0).astype(BF16)
    p = {
        "ln_in_g": vec(ln_in_g), "ln_in_b": vec(ln_in_b), "ln_mem_g": vec(ln_mem_g), "ln_mem_b": vec(ln_mem_b),
        "w_in": w_in[l].astype(BF16),
        "qg": jnp.tile(q_norm_g[l], LANES // HEAD_DIM).reshape(1, LANES),
        "kg": jnp.tile(k_norm_g[l], LANES // HEAD_DIM).reshape(1, LANES),
        "head_mean": head_mean,
        "score_bound": (NORM_CAP * jnp.max(jnp.abs(q_norm_g[l])) * Q_SCALE) * (NORM_CAP * jnp.max(jnp.abs(k_norm_g[l]))),
        "conv_w": conv_w[l], "conv_b": vec(conv_b[l]), "cln_g": vec(conv_ln_g[l]), "cln_b": vec(conv_ln_b[l]),
        "w_out": w_mix_out[l].astype(BF16), "ln1_g": vec(ln1_g[l]), "ln1_b": vec(ln1_b[l]),
        "w_cq": w_cq[l].astype(BF16), "w_ck": w_ck[l].astype(BF16), "w_cv": w_cv[l].astype(BF16),
        "w_co": w_co[l].astype(BF16), "ln2_g": vec(ln2_g[l]), "ln2_b": vec(ln2_b[l]),
        "wr_hi": wr_hi, "wr_lo": wr_lo, "br": b_r,
        "w_gate": w_e_gate[l], "w_up": w_e_up[l], "w_down": w_e_down[l],
        "ln3_g": vec(ln3_g[l]), "ln3_b": vec(ln3_b[l]),
    }
    return (_encoder(x_prompt, mem_prompt, p), _encoder(x_sample, mem_sample, p))
```
